```python
import jax
import jax.numpy as jnp
from jax import lax
import numpy as np

D_MODEL = 1024
BATCH = 2
SEQ = 8192
DEPTH = 4

HEAD_DIM = 64
ROPE_THETA = 10000.0
NORM_EPS = 1e-6
Q_BLOCK = 128
D_FF = 2816
A_HEADS = 4
IDX_HEADS = 4
IDX_DIM = 64
DSA_TOPK_MAX = 256
B_HEADS = 4
MOBA_BLOCK = 256
MOBA_TOPK = 3
C_HEADS = 8
C_GROUPS = 2
C_REP = C_HEADS // C_GROUPS
CMP_BLOCK = 32
CMP_STRIDE = 16
CMP_HIDDEN = 128
SLC_BLOCK = 64
SLC_TOPN = 16
WINDOW = 512
FORCE_SCORE = 1e30
N_BRANCH = 3
A_WIDTH = A_HEADS * HEAD_DIM
B_WIDTH = B_HEADS * HEAD_DIM
C_WIDTH = C_HEADS * HEAD_DIM
IN_SIZES = (A_WIDTH, 2 * HEAD_DIM, IDX_HEADS * IDX_DIM, IDX_DIM, IDX_HEADS, 3 * B_WIDTH, C_WIDTH, 6 * C_GROUPS * HEAD_DIM, 3 * C_HEADS, N_BRANCH * D_MODEL)
IN_COLS = sum(IN_SIZES)

kernel_name = 'hybrid_dsa_moba_nsa_macaron'


def _rmsnorm(x, g):
    xf = x.astype(jnp.float32)
    y = xf * lax.rsqrt(jnp.mean(xf * xf, axis=-1, keepdims=True) + NORM_EPS)
    return (y * g.astype(jnp.float32)).astype(x.dtype)


def _swiglu(x, w_gate, w_up, w_down):
    return (jax.nn.silu(x @ w_gate) * (x @ w_up)) @ w_down


def _rope(x, pos):
    half = x.shape[-1] // 2
    inv = ROPE_THETA ** (-jnp.arange(half, dtype=jnp.float32) / half)
    ang = pos.astype(jnp.float32)[:, None] * inv[None, :]
    cos = jnp.cos(ang)[:, None, :]
    sin = jnp.sin(ang)[:, None, :]
    xf = x.astype(jnp.float32)
    x1, x2 = xf[..., :half], xf[..., half:]
    return jnp.concatenate([x1 * cos - x2 * sin, x1 * sin + x2 * cos], axis=-1).astype(x.dtype)


def _masked_softmax(s, mask):
    s = jnp.where(mask, s.astype(jnp.float32), -jnp.inf)
    m = jnp.max(s, axis=-1, keepdims=True)
    m = jnp.where(jnp.isfinite(m), m, 0.0)
    p = jnp.exp(s - m)
    den = jnp.sum(p, axis=-1, keepdims=True)
    return p / jnp.where(den > 0, den, 1.0)


def _qslice(a, q0):
    return lax.dynamic_slice_in_dim(a, q0, Q_BLOCK, axis=1)


def _sweep(fn, T):
    out = lax.map(fn, jnp.arange(T // Q_BLOCK))
    nC, B, Qb, H, dh = out.shape
    return jnp.transpose(out, (1, 0, 2, 3, 4)).reshape(B, nC * Qb, H * dh)


def _dsa_attention(q, k, v, iq, ik, iw):
    T = q.shape[1]
    topk = min(DSA_TOPK_MAX, T // 4)
    kpos = jnp.arange(T)
    idx_scale = (IDX_HEADS * IDX_DIM) ** -0.5
    scale = HEAD_DIM ** -0.5
    gather = jax.vmap(lambda a, i: a[i])

    def block(c):
        q0 = c * Q_BLOCK
        tq = q0 + jnp.arange(Q_BLOCK)
        logits = jnp.einsum('bqhe,bse->bqhs', _qslice(iq, q0), ik)
        score = jnp.einsum('bqh,bqhs->bqs', _qslice(iw, q0).astype(jnp.float32), jax.nn.relu(logits).astype(jnp.float32)) * idx_scale
        score = jnp.where(kpos[None, None, :] <= tq[None, :, None], score, -jnp.inf)
        _, sel = lax.top_k(score, topk)
        k_sel = gather(k, sel)
        v_sel = gather(v, sel)
        s = jnp.einsum('bqhd,bqkd->bqhk', _qslice(q, q0), k_sel) * scale
        p = _masked_softmax(s, (sel <= tq[None, :, None])[:, :, None, :])
        return jnp.einsum('bqhk,bqkd->bqhd', p.astype(v.dtype), v_sel)

    return _sweep(block, T)


def _moba_attention(q, k, v):
    B, T, H, dh = q.shape
    scale = dh ** -0.5
    nb = max(-(-T // MOBA_BLOCK), MOBA_TOPK + 1)
    pad = ((0, 0), (0, nb * MOBA_BLOCK - T), (0, 0), (0, 0))
    kp = jnp.pad(k, pad)
    vp = jnp.pad(v, pad)
    kb = kp.reshape(B, nb, MOBA_BLOCK, H, dh)
    kmean = jnp.mean(kb.astype(jnp.float32), axis=2)
    kbh = jnp.transpose(kb, (0, 3, 1, 2, 4))
    vbh = jnp.transpose(vp.reshape(B, nb, MOBA_BLOCK, H, dh), (0, 3, 1, 2, 4))
    bi = jnp.arange(B)[:, None, None, None]
    hi = jnp.arange(H)[None, None, :, None]
    blk = jnp.arange(nb)

    def block(c):
        q0 = c * Q_BLOCK
        tq = q0 + jnp.arange(Q_BLOCK)
        own = q0 // MOBA_BLOCK
        qc = _qslice(q, q0)
        gate = jnp.einsum('bqhd,bnhd->bqhn', qc.astype(jnp.float32), kmean)
        gate = jnp.where(blk < own, gate, -jnp.inf)
        _, sel = lax.top_k(gate, MOBA_TOPK)
        k_sel = kbh[bi, hi, sel]
        v_sel = vbh[bi, hi, sel].reshape(B, Q_BLOCK, H, MOBA_TOPK * MOBA_BLOCK, dh)
        s_sel = jnp.einsum('bqhd,bqhjsd->bqhjs', qc, k_sel).reshape(B, Q_BLOCK, H, MOBA_TOPK * MOBA_BLOCK)
        m_sel = jnp.repeat(sel < own, MOBA_BLOCK, axis=-1)
        k_own = lax.dynamic_slice_in_dim(kp, own * MOBA_BLOCK, MOBA_BLOCK, axis=1)
        v_own = lax.dynamic_slice_in_dim(vp, own * MOBA_BLOCK, MOBA_BLOCK, axis=1)
        s_own = jnp.einsum('bqhd,bshd->bqhs', qc, k_own)
        m_own = (own * MOBA_BLOCK + jnp.arange(MOBA_BLOCK))[None, :] <= tq[:, None]
        s = jnp.concatenate([s_sel, s_own], axis=-1) * scale
        mask = jnp.concatenate([m_sel, jnp.broadcast_to(m_own[None, :, None, :], (B, Q_BLOCK, H, MOBA_BLOCK))], axis=-1)
        p = _masked_softmax(s, mask).astype(v.dtype)
        n_sel_keys = MOBA_TOPK * MOBA_BLOCK
        return (jnp.einsum('bqhm,bqhmd->bqhd', p[..., :n_sel_keys], v_sel)
                + jnp.einsum('bqhs,bshd->bqhd', p[..., n_sel_keys:], v_own))

    return _sweep(block, T)


def _compress(kv, pos_emb, w1, w2):
    B, T, G, dh = kv.shape
    nc = (T - CMP_BLOCK) // CMP_STRIDE + 1
    idx = jnp.arange(nc)[:, None] * CMP_STRIDE + jnp.arange(CMP_BLOCK)[None, :]
    blk = kv[:, idx] + pos_emb[:, None, :]
    flat = jnp.transpose(blk, (0, 1, 3, 2, 4)).reshape(B, nc, G, CMP_BLOCK * dh)
    return jax.nn.gelu(flat @ w1) @ w2


def _nsa_attention(q_raw, q_rot, k_cmp, v_cmp, k_slc, v_slc, k_win, v_win, gates):
    B, T = q_raw.shape[0], q_raw.shape[1]
    G, R, dh = C_GROUPS, C_REP, HEAD_DIM
    scale = dh ** -0.5
    nc = k_cmp.shape[1]
    ns = T // SLC_BLOCK
    n_sel = min(SLC_TOPN, ns)
    cmp_start = jnp.arange(nc) * CMP_STRIDE
    cmp_last = cmp_start + CMP_BLOCK - 1
    slc_start = jnp.arange(ns) * SLC_BLOCK
    overlap = ((cmp_start[:, None] < slc_start[None, :] + SLC_BLOCK) & (slc_start[None, :] <= cmp_last[:, None])).astype(jnp.float32)
    blk = jnp.arange(ns)
    k_sb = jnp.transpose(k_slc.reshape(B, ns, SLC_BLOCK, G, dh), (0, 3, 1, 2, 4))
    v_sb = jnp.transpose(v_slc.reshape(B, ns, SLC_BLOCK, G, dh), (0, 3, 1, 2, 4))
    wpad = ((0, 0), (WINDOW, 0), (0, 0), (0, 0))
    k_wp = jnp.pad(k_win, wpad)
    v_wp = jnp.pad(v_win, wpad)
    win_off = jnp.arange(WINDOW + Q_BLOCK) - WINDOW
    bi = jnp.arange(B)[:, None, None, None]
    gi = jnp.arange(G)[None, None, :, None]

    def block(c):
        q0 = c * Q_BLOCK
        tq = q0 + jnp.arange(Q_BLOCK)
        qn = _qslice(q_raw, q0).reshape(B, Q_BLOCK, G, R, dh)
        qr = _qslice(q_rot, q0).reshape(B, Q_BLOCK, G, R, dh)
        s_c = jnp.einsum('bqgrd,bngd->bqgrn', qn, k_cmp) * scale
        p_c = _masked_softmax(s_c, (cmp_last[None, :] <= tq[:, None])[None, :, None, None, :])
        o_c = jnp.einsum('bqgrn,bngd->bqgrd', p_c.astype(v_cmp.dtype), v_cmp)
        imp = jnp.einsum('bqgrn,nj->bqgj', p_c, overlap)
        jq = tq // SLC_BLOCK
        adm = blk[None, :] <= jq[:, None]
        forced = (blk[None, :] == 0) | (blk[None, :] == jq[:, None]) | (blk[None, :] == jq[:, None] - 1)
        score = jnp.where(adm[None, :, None, :], imp, -jnp.inf)
        score = jnp.where((adm & forced)[None, :, None, :], FORCE_SCORE, score)
        _, sel = lax.top_k(score, n_sel)
        k_sel = k_sb[bi, gi, sel]
        v_sel = v_sb[bi, gi, sel].reshape(B, Q_BLOCK, G, n_sel * SLC_BLOCK, dh)
        s_s = jnp.einsum('bqgrd,bqgjsd->bqgrjs', qr, k_sel).reshape(B, Q_BLOCK, G, R, n_sel * SLC_BLOCK) * scale
        kpos = (sel[..., None] * SLC_BLOCK + jnp.arange(SLC_BLOCK)).reshape(B, Q_BLOCK, G, n_sel * SLC_BLOCK)
        p_s = _masked_softmax(s_s, (kpos <= tq[None, :, None, None])[:, :, :, None, :])
        o_s = jnp.einsum('bqgrm,bqgmd->bqgrd', p_s.astype(v_slc.dtype), v_sel)
        k_wc = lax.dynamic_slice_in_dim(k_wp, q0, WINDOW + Q_BLOCK, axis=1)
        v_wc = lax.dynamic_slice_in_dim(v_wp, q0, WINDOW + Q_BLOCK, axis=1)
        kpos_w = q0 + win_off
        diff = tq[:, None] - kpos_w[None, :]
        mask_w = (diff >= 0) & (diff < WINDOW) & (kpos_w[None, :] >= 0)
        s_w = jnp.einsum('bqgrd,bkgd->bqgrk', qr, k_wc) * scale
        p_w = _masked_softmax(s_w, mask_w[None, :, None, None, :])
        o_w = jnp.einsum('bqgrk,bkgd->bqgrd', p_w.astype(v_win.dtype), v_wc)
        g = _qslice(gates, q0).reshape(B, Q_BLOCK, G, R, 3)
        o = g[..., 0:1] * o_c + g[..., 1:2] * o_s + g[..., 2:3] * o_w
        return o.reshape(B, Q_BLOCK, C_HEADS, dh)

    return _sweep(block, T)


def setup_inputs(seed: int = 0) -> dict:
    key = jax.random.key(seed)
    k = jax.random.split(key, 22)
    L, D = DEPTH, D_MODEL

    def nrm(kk, shape, fan_in):
        return jax.random.normal(kk, shape, jnp.float32) * (fan_in ** -0.5)

    def gain(kk, shape):
        return 1.0 + 0.05 * jax.random.normal(kk, shape, jnp.float32)

    cmp_in = CMP_BLOCK * HEAD_DIM
    return {
        'x': jax.random.normal(k[0], (BATCH, SEQ, D), jnp.float32),
        'ffn1_norm': gain(k[1], (L, D)),
        'ffn1_w_gate': nrm(k[2], (L, D, D_FF), D),
        'ffn1_w_up': nrm(k[3], (L, D, D_FF), D),
        'ffn1_w_down': nrm(k[4], (L, D_FF, D), D_FF),
        'mix_norm': gain(k[5], (L, D)),
        'w_in': nrm(k[6], (L, D, IN_COLS), D),
        'cmp_pos_k': 0.1 * jax.random.normal(k[7], (L, CMP_BLOCK, HEAD_DIM), jnp.float32),
        'cmp_w1_k': nrm(k[8], (L, cmp_in, CMP_HIDDEN), cmp_in),
        'cmp_w2_k': nrm(k[9], (L, CMP_HIDDEN, HEAD_DIM), CMP_HIDDEN),
        'cmp_pos_v': 0.1 * jax.random.normal(k[10], (L, CMP_BLOCK, HEAD_DIM), jnp.float32),
        'cmp_w1_v': nrm(k[11], (L, cmp_in, CMP_HIDDEN), cmp_in),
        'cmp_w2_v': nrm(k[12], (L, CMP_HIDDEN, HEAD_DIM), CMP_HIDDEN),
        'w_branch_a': nrm(k[13], (L, A_WIDTH, D), A_WIDTH),
        'w_branch_b': nrm(k[14], (L, B_WIDTH, D), B_WIDTH),
        'w_branch_c': nrm(k[15], (L, C_WIDTH, D), C_WIDTH),
        'w_out': nrm(k[16], (L, D, D), D),
        'ffn2_norm': gain(k[17], (L, D)),
        'ffn2_w_gate': nrm(k[18], (L, D, D_FF), D),
        'ffn2_w_up': nrm(k[19], (L, D, D_FF), D),
        'ffn2_w_down': nrm(k[20], (L, D_FF, D), D_FF),
        'final_norm': gain(k[21], (D,)),
    }


def reference(x, ffn1_norm, ffn1_w_gate, ffn1_w_up, ffn1_w_down, mix_norm, w_in,
              cmp_pos_k, cmp_w1_k, cmp_w2_k, cmp_pos_v, cmp_w1_v, cmp_w2_v,
              w_branch_a, w_branch_b, w_branch_c, w_out,
              ffn2_norm, ffn2_w_gate, ffn2_w_up, ffn2_w_down, final_norm):
    B, T, D = x.shape
    pos = jnp.arange(T)
    cuts = np.cumsum(IN_SIZES)[:-1].tolist()
    for l in range(DEPTH):
        x = x + 0.5 * _swiglu(_rmsnorm(x, ffn1_norm[l]), ffn1_w_gate[l], ffn1_w_up[l], ffn1_w_down[l])
        h = _rmsnorm(x, mix_norm[l])
        a_q, a_kv, i_q, i_k, i_w, b_qkv, c_q, c_kv, c_g, m_g = jnp.split(h @ w_in[l], cuts, axis=-1)
        a_q = _rope(a_q.reshape(B, T, A_HEADS, HEAD_DIM), pos)
        a_k = _rope(a_kv[..., None, :HEAD_DIM], pos)[:, :, 0]
        a_v = a_kv[..., HEAD_DIM:]
        i_q = _rope(i_q.reshape(B, T, IDX_HEADS, IDX_DIM), pos)
        i_k = _rope(i_k[:, :, None, :], pos)[:, :, 0]
        y_a = _dsa_attention(a_q, a_k, a_v, i_q, i_k, i_w) @ w_branch_a[l]
        b_qkv = b_qkv.reshape(B, T, 3, B_HEADS, HEAD_DIM)
        y_b = _moba_attention(_rope(b_qkv[:, :, 0], pos), _rope(b_qkv[:, :, 1], pos), b_qkv[:, :, 2]) @ w_branch_b[l]
        c_q = c_q.reshape(B, T, C_HEADS, HEAD_DIM)
        c_kv = c_kv.reshape(B, T, 6, C_GROUPS, HEAD_DIM)
        k_cmp = _compress(c_kv[:, :, 0], cmp_pos_k[l], cmp_w1_k[l], cmp_w2_k[l])
        v_cmp = _compress(c_kv[:, :, 1], cmp_pos_v[l], cmp_w1_v[l], cmp_w2_v[l])
        c_gates = jax.nn.sigmoid(c_g.reshape(B, T, C_HEADS, 3))
        y_c = _nsa_attention(c_q, _rope(c_q, pos), k_cmp, v_cmp,
                             _rope(c_kv[:, :, 2], pos), c_kv[:, :, 3],
                             _rope(c_kv[:, :, 4], pos), c_kv[:, :, 5], c_gates) @ w_branch_c[l]
        g = jax.nn.sigmoid(m_g.reshape(B, T, N_BRANCH, D))
        x = x + (g[:, :, 0] * y_a + g[:, :, 1] * y_b + g[:, :, 2] * y_c) @ w_out[l]
        x = x + 0.5 * _swiglu(_rmsnorm(x, ffn2_norm[l]), ffn2_w_gate[l], ffn2_w_up[l], ffn2_w_down[l])
    return _rmsnorm(x, final_norm)
```

```python
import functools

import numpy as np
import jax
import jax.numpy as jnp
from jax import lax
from jax.experimental import pallas as pl
from jax.experimental.pallas import tpu as pltpu

HEAD_DIM = 64
ROPE_THETA = 10000.0
NORM_EPS = 1e-6
A_HEADS = 4
IDX_HEADS = 4
IDX_DIM = 64
DSA_TOPK_MAX = 256
B_HEADS = 4
MOBA_BLOCK = 256
MOBA_TOPK = 3
C_HEADS = 8
C_GROUPS = 2
C_REP = C_HEADS // C_GROUPS
CMP_BLOCK = 32
CMP_STRIDE = 16
CMP_HIDDEN = 128
SLC_BLOCK = 64
SLC_TOPN = 16
WINDOW = 512
FORCE_SCORE = 1e30

LANES = 128
VMEM_LIMIT = 56 * 1024 * 1024

TQ = 128
KC = 512
FFN_TM = 512
PROJ_TM = 512
MERGE_TM = 512

NEG_BIG = -1e30
F32 = jnp.float32
BF16 = jnp.bfloat16
HIGHEST = lax.Precision.HIGHEST

_O_AQ = 0
_O_AK = 256
_O_AV = 320
_O_IQ = 384
_O_IK = 640
_O_IW = 704
_O_BQ = 708
_O_BK = 964
_O_BV = 1220
_O_CQ = 1476
_O_CKV = 1988
_O_CG = 2756
_O_MG = 2780
_N_IN = 5852

P_AQ = 0
P_AK = 256
P_IK = 320
P_IQ = 384
P_BQ = 640
P_BK = 896
P_CKS = 1152
P_CKW = 1280
P_CQ = 1408
P_ROPE_END = 1920
P_BV = 1920
P_KCMP = 2176
P_VS = 2432
P_VW = 2560
P_AV = 2688
P_MISC = 2752
P_COLS = 2816
MISC_CG = 0
MISC_IW = 24


def _in_perm():
    perm = -np.ones((P_COLS,), np.int64)

    def put(dst, src, n):
        perm[dst:dst + n] = np.arange(src, src + n)

    put(P_AQ, _O_AQ, 256)
    put(P_AK, _O_AK, 64)
    put(P_IK, _O_IK, 64)
    put(P_IQ, _O_IQ, 256)
    put(P_BQ, _O_BQ, 256)
    put(P_BK, _O_BK, 256)
    ckv = lambda s, g: _O_CKV + (s * C_GROUPS + g) * HEAD_DIM
    for g in range(C_GROUPS):
        put(P_CKS + 64 * g, ckv(2, g), 64)
        put(P_CKW + 64 * g, ckv(4, g), 64)
        put(P_KCMP + 64 * g, ckv(0, g), 64)
        put(P_KCMP + 128 + 64 * g, ckv(1, g), 64)
        put(P_VS + 64 * g, ckv(3, g), 64)
        put(P_VW + 64 * g, ckv(5, g), 64)
    put(P_CQ, _O_CQ, 512)
    put(P_BV, _O_BV, 256)
    put(P_AV, _O_AV, 64)
    put(P_MISC + MISC_CG, _O_CG, 24)
    put(P_MISC + MISC_IW, _O_IW, 4)
    return perm


def _cparams(sem):
    return pltpu.CompilerParams(dimension_semantics=sem, vmem_limit_bytes=VMEM_LIMIT)


def _rms(x, g):
    y = x * lax.rsqrt(jnp.mean(x * x, axis=-1, keepdims=True) + NORM_EPS)
    return y * g


def _dot_nt(a, b, precision=None):
    return lax.dot_general(a, b, (((1,), (1,)), ((), ())), precision=precision,
                           preferred_element_type=F32)


def _ffn_body(x_ref, g_ref, wg_ref, wu_ref, wd_ref, fg_ref, o_ref, h_scr, acc_scr, *, final_norm):
    f = pl.program_id(1)

    @pl.when(f == 0)
    def _():
        h_scr[...] = _rms(x_ref[...], g_ref[...]).astype(BF16)
        acc_scr[...] = jnp.zeros_like(acc_scr)

    h = h_scr[...]
    a = jnp.dot(h, wg_ref[...], preferred_element_type=F32)
    u = jnp.dot(h, wu_ref[...], preferred_element_type=F32)
    act = (a * jax.nn.sigmoid(a) * u).astype(BF16)
    acc_scr[...] += jnp.dot(act, wd_ref[...], preferred_element_type=F32)

    @pl.when(f == pl.num_programs(1) - 1)
    def _():
        y = x_ref[...] + 0.5 * acc_scr[...]
        if final_norm:
            y = _rms(y, fg_ref[...])
        o_ref[...] = y


def _ffn(x2, g, wg, wu, wd, fg, final_norm):
    m, d = x2.shape
    dff = wg.shape[1]
    tf = dff // 2 if (dff // 2) % LANES == 0 else dff
    tm = FFN_TM
    return pl.pallas_call(
        functools.partial(_ffn_body, final_norm=final_norm),
        grid=(m // tm, dff // tf),
        in_specs=[
            pl.BlockSpec((tm, d), lambda i, f: (i, 0)),
            pl.BlockSpec((1, d), lambda i, f: (0, 0)),
            pl.BlockSpec((d, tf), lambda i, f: (0, f)),
            pl.BlockSpec((d, tf), lambda i, f: (0, f)),
            pl.BlockSpec((tf, d), lambda i, f: (f, 0)),
            pl.BlockSpec((1, d), lambda i, f: (0, 0)),
        ],
        out_specs=pl.BlockSpec((tm, d), lambda i, f: (i, 0)),
        out_shape=jax.ShapeDtypeStruct((m, d), F32),
        scratch_shapes=[pltpu.VMEM((tm, d), BF16), pltpu.VMEM((tm, d), F32)],
        compiler_params=_cparams(("parallel", "arbitrary")),
        name="ffn",
    )(x2, g, wg, wu, wd, fg)


def _lane_iota(shape):
    return lax.broadcasted_iota(jnp.int32, shape, len(shape) - 1)


def _split_hi_lo(x):
    hi = x.astype(BF16).astype(F32)
    return hi, x - hi


def _inproj_body(x_ref, g_ref, w_ref, cos_ref, sa_ref, sb_ref,
                 aq_ref, akv_ref, iq_ref, ik_ref, misc_ref, bq_ref, bqp_ref, bkv_ref, km_ref,
                 cqn_ref, cqr_ref, cslc_ref, cwin_ref, ccmp_ref):
    tm = x_ref.shape[0]
    h = _rms(x_ref[...], g_ref[...]).astype(BF16)
    cos, sa, sb = cos_ref[...], sa_ref[...], sb_ref[...]
    lane = _lane_iota((tm, LANES))
    low = lane < HEAD_DIM

    raw = []
    rot = []
    for j in range(P_COLS // 256):
        z = jnp.dot(h, w_ref[:, 256 * j:256 * (j + 1)], preferred_element_type=F32)
        for half in range(2):
            p = z[:, LANES * half:LANES * (half + 1)]
            raw.append(p)
            if LANES * len(raw) <= P_ROPE_END:
                rot.append(p * cos + pltpu.roll(p, 32, axis=1) * sa + pltpu.roll(p, 96, axis=1) * sb)

    def pick(pieces, col):
        p = pieces[col // LANES]
        return pltpu.roll(p, HEAD_DIM, axis=1) if col % LANES else p

    def join(lo, hi=None):
        if hi is None:
            return jnp.where(low, lo, 0.0)
        return jnp.where(low, lo, pltpu.roll(hi, HEAD_DIM, axis=1))

    scale = HEAD_DIM ** -0.5

    for hh in range(A_HEADS):
        aq_ref[:, LANES * hh:LANES * (hh + 1)] = join(pick(rot, P_AQ + 64 * hh) * scale).astype(BF16)
        qhi, qlo = _split_hi_lo(pick(rot, P_IQ + 64 * hh))
        iq_ref[:, 256 * hh:256 * hh + LANES] = join(qhi, qlo).astype(BF16)
        iq_ref[:, 256 * hh + LANES:256 * (hh + 1)] = join(qhi).astype(BF16)
    akv_ref[...] = join(pick(rot, P_AK), pick(raw, P_AV)).astype(BF16)
    khi, klo = _split_hi_lo(pick(rot, P_IK))
    ik_ref[:, :LANES] = join(khi, khi).astype(BF16)
    ik_ref[:, LANES:] = join(klo).astype(BF16)
    misc_ref[...] = pick(raw, P_MISC)

    bq_ref[:, :LANES] = rot[P_BQ // LANES]
    bq_ref[:, LANES:] = rot[P_BQ // LANES + 1]
    for hh in range(B_HEADS):
        bqp_ref[:, LANES * hh:LANES * (hh + 1)] = join(pick(rot, P_BQ + 64 * hh) * scale).astype(BF16)
        bkv_ref[:, LANES * hh:LANES * (hh + 1)] = join(pick(rot, P_BK + 64 * hh),
                                                      pick(raw, P_BV + 64 * hh)).astype(BF16)
    nblk = tm // MOBA_BLOCK
    km_ref[...] = jnp.zeros_like(km_ref)
    for half in range(2):
        kp = rot[P_BK // LANES + half]
        for b in range(nblk):
            km_ref[0, b:b + 1, LANES * half:LANES * (half + 1)] = jnp.mean(
                kp[MOBA_BLOCK * b:MOBA_BLOCK * (b + 1)], axis=0, keepdims=True)

    for hh in range(C_HEADS):
        nhi, nlo = _split_hi_lo(pick(raw, P_CQ + 64 * hh) * scale)
        cqn_ref[:, 256 * hh:256 * hh + LANES] = join(nhi, nlo).astype(BF16)
        cqn_ref[:, 256 * hh + LANES:256 * (hh + 1)] = join(nhi).astype(BF16)
        cqr_ref[:, LANES * hh:LANES * (hh + 1)] = join(pick(rot, P_CQ + 64 * hh) * scale).astype(BF16)
    for g in range(C_GROUPS):
        cslc_ref[:, LANES * g:LANES * (g + 1)] = join(pick(rot, P_CKS + 64 * g),
                                                     pick(raw, P_VS + 64 * g)).astype(BF16)
        cwin_ref[:, LANES * g:LANES * (g + 1)] = join(pick(rot, P_CKW + 64 * g),
                                                     pick(raw, P_VW + 64 * g)).astype(BF16)
    ccmp_ref[:, :LANES] = raw[P_KCMP // LANES]
    ccmp_ref[:, LANES:] = raw[P_KCMP // LANES + 1]


def _inproj(x2, g, w, cos, sa, sb, seq):
    m, d = x2.shape
    tm = PROJ_TM
    nt = seq // tm
    row = lambda c: pl.BlockSpec((tm, c), lambda i: (i, 0))
    tab = pl.BlockSpec((tm, LANES), lambda i: (i % nt, 0))
    outs = [
        ("aq", 4 * LANES, BF16), ("akv", LANES, BF16), ("iq", 4 * 256, BF16), ("ik", 256, BF16),
        ("misc", LANES, F32), ("bq", 256, F32), ("bqp", 4 * LANES, BF16), ("bkv", 4 * LANES, BF16),
        ("km", None, F32),
        ("cqn", 8 * 256, BF16), ("cqr", 8 * LANES, BF16), ("cslc", 2 * LANES, BF16),
        ("cwin", 2 * LANES, BF16), ("ccmp", 256, F32),
    ]
    out_specs, out_shape = [], []
    for name, c, dt in outs:
        if name == "km":
            out_specs.append(pl.BlockSpec((1, 8, 256), lambda i: (i, 0, 0)))
            out_shape.append(jax.ShapeDtypeStruct((m // tm, 8, 256), dt))
        else:
            out_specs.append(row(c))
            out_shape.append(jax.ShapeDtypeStruct((m, c), dt))
    res = pl.pallas_call(
        _inproj_body,
        grid=(m // tm,),
        in_specs=[row(d), pl.BlockSpec((1, d), lambda i: (0, 0)),
                  pl.BlockSpec((d, P_COLS), lambda i: (0, 0)), tab, tab, tab],
        out_specs=out_specs,
        out_shape=out_shape,
        compiler_params=_cparams(("parallel",)),
        name="inproj",
    )(x2, g, w, cos, sa, sb)
    return dict(zip([o[0] for o in outs], res))


def _compress_body(x_ref, w1_ref, w2_ref, pos_ref, ok_ref, ov_ref, *, n_valid):
    x = x_ref[0, 0]
    w1 = w1_ref[0]
    pre = jnp.dot(x, w1, precision=HIGHEST, preferred_element_type=F32)
    pp = jnp.dot(pos_ref[0], w1, precision=HIGHEST, preferred_element_type=F32)
    posb = pp[0:1, :CMP_HIDDEN] + pp[1:2, CMP_HIDDEN:]
    ncp = x.shape[0]
    upper = pre[:, :CMP_HIDDEN]
    lower_next = pltpu.roll(pre[:, CMP_HIDDEN:], ncp - 1, axis=0)
    hid = jax.nn.gelu(upper + lower_next + posb)
    out = jnp.dot(hid, w2_ref[0], precision=HIGHEST, preferred_element_type=F32)
    rows = lax.broadcasted_iota(jnp.int32, out.shape, 0)
    out = jnp.where(rows < n_valid, out, 0.0)
    low = _lane_iota(out.shape) < HEAD_DIM
    hi, lo = _split_hi_lo(out)
    ok_ref[0, 0, :, :LANES] = jnp.where(low, hi, pltpu.roll(hi, HEAD_DIM, axis=1)).astype(BF16)
    ok_ref[0, 0, :, LANES:] = lo.astype(BF16)
    ov_ref[0, 0] = pltpu.roll(out, HEAD_DIM, axis=1).astype(BF16)


def _compress(xc, w1, w2, pos, n_valid):
    b, four, ncp, _ = xc.shape
    return pl.pallas_call(
        functools.partial(_compress_body, n_valid=n_valid),
        grid=(b, four),
        in_specs=[
            pl.BlockSpec((1, 1, ncp, 1024), lambda i, j: (i, j, 0, 0)),
            pl.BlockSpec((1, 1024, 256), lambda i, j: (j // 2, 0, 0)),
            pl.BlockSpec((1, CMP_HIDDEN, LANES), lambda i, j: (j // 2, 0, 0)),
            pl.BlockSpec((1, 8, 1024), lambda i, j: (j // 2, 0, 0)),
        ],
        out_specs=[pl.BlockSpec((1, 1, ncp, 256), lambda i, j: (i, j, 0, 0)),
                   pl.BlockSpec((1, 1, ncp, LANES), lambda i, j: (i, j, 0, 0))],
        out_shape=[jax.ShapeDtypeStruct((b, four, ncp, 256), BF16),
                   jax.ShapeDtypeStruct((b, four, ncp, LANES), BF16)],
        compiler_params=_cparams(("parallel", "parallel")),
        name="compress",
    )(xc, w1, w2, pos)


def _online_update(carry, s, mask, kv, heads):
    m_old, l_old, acc = carry
    n = s.shape[-1]
    s3 = s.reshape(heads, TQ, n)
    sm = jnp.where(mask[None], s3, NEG_BIG)
    m_new = jnp.maximum(m_old, jnp.max(sm, axis=-1, keepdims=True))
    p = jnp.where(mask[None], jnp.exp(s3 - m_new), 0.0)
    alpha = jnp.exp(m_old - m_new)
    l_new = alpha * l_old + jnp.sum(p, axis=-1, keepdims=True)
    pv = jnp.dot(p.reshape(heads * TQ, n).astype(BF16), kv, preferred_element_type=F32)
    acc = alpha.reshape(heads * TQ, 1) * acc + pv
    return m_new, l_new, acc


def _online_init(heads):
    return (jnp.full((heads, TQ, 1), NEG_BIG, F32), jnp.zeros((heads, TQ, 1), F32),
            jnp.zeros((heads * TQ, LANES), F32))


def _online_finish(carry, heads):
    _, l, acc = carry
    return acc / l.reshape(heads * TQ, 1)


def _store_heads(o_ref, o, heads, col0=0):
    low = _lane_iota((TQ, LANES)) < HEAD_DIM
    for pair in range(heads // 2):
        even = o[TQ * (2 * pair):TQ * (2 * pair + 1)]
        odd = o[TQ * (2 * pair + 1):TQ * (2 * pair + 2)]
        piece = jnp.where(low, pltpu.roll(even, HEAD_DIM, axis=1), odd)
        o_ref[0, :, col0 + LANES * pair:col0 + LANES * (pair + 1)] = piece.astype(o_ref.dtype)


def _topn_mask(score, n_pick):
    idx = _lane_iota(score.shape)
    width = score.shape[-1]
    sel = jnp.zeros(score.shape, F32)
    for _ in range(n_pick):
        m = jnp.max(score, axis=-1, keepdims=True)
        first = jnp.min(jnp.where(score == m, idx, width), axis=-1, keepdims=True)
        hit = idx == first
        sel = jnp.where(hit, 1.0, sel)
        score = jnp.where(hit, -jnp.inf, score)
    return sel


_NEG_INF_KEY = -2139095041
_INT_MIN = -2147483648


def _dsa_body(iq_ref, ik_ref, misc_ref, aq_ref, akv_ref, o_ref, key_scr, *, topk, seq):
    c = pl.program_id(1)
    q0 = c * TQ
    nchunk = (q0 + TQ + KC - 1) // KC
    row = q0 + lax.broadcasted_iota(jnp.int32, (TQ, 1), 0)
    col = _lane_iota((1, KC))
    idx_scale = (IDX_HEADS * IDX_DIM) ** -0.5

    iq = jnp.concatenate([iq_ref[0, :, 256 * h:256 * (h + 1)] for h in range(IDX_HEADS)], axis=0)
    misc = misc_ref[0]
    iw = [misc[:, MISC_IW + h:MISC_IW + h + 1] for h in range(IDX_HEADS)]

    def score_body(j, _):
        ks = pl.multiple_of(j * KC, KC)
        lg = jnp.maximum(_dot_nt(iq, ik_ref[0, pl.ds(ks, KC), :]), 0.0)
        sc = iw[0] * lg[0:TQ]
        for h in range(1, IDX_HEADS):
            sc = sc + iw[h] * lg[TQ * h:TQ * (h + 1)]
        sc = sc * idx_scale
        sc = jnp.where(sc == 0.0, 0.0, sc)
        sc = jnp.where(ks + col <= row, sc, -jnp.inf)
        bits = lax.bitcast_convert_type(sc, jnp.int32)
        key_scr[j] = jnp.where(bits < 0, bits ^ jnp.int32(0x7FFFFFFF), bits)
        return 0

    lax.fori_loop(0, nchunk, score_body, 0)

    def count(pred):
        def body(j, acc):
            hit = jnp.where(pred(key_scr[j], j), 1.0, 0.0)
            for q in range(KC // LANES):
                acc = acc + hit[:, LANES * q:LANES * (q + 1)]
            return acc
        acc = lax.fori_loop(0, nchunk, body, jnp.zeros((TQ, LANES), F32))
        return jnp.sum(acc, axis=-1, keepdims=True)

    def bit_body(i, t):
        cand = t + lax.shift_left(jnp.int32(1), 31 - i)
        cnt = count(lambda k, j: k >= cand)
        return jnp.where(cnt >= topk, cand, t)

    thr = lax.fori_loop(0, 32, bit_body, jnp.full((TQ, 1), _INT_MIN, jnp.int32))

    n_gt = count(lambda k, j: k > thr)
    n_ge = count(lambda k, j: k >= thr)
    need = topk - n_gt
    finite = thr > _NEG_INF_KEY
    any_tie = jnp.max(jnp.where(finite, n_ge, 0.0)) > topk

    def tie_cut():
        def pos_body(i, cut):
            cand = cut + lax.shift_left(jnp.int32(1), (seq.bit_length() - 1) - i)
            cnt = count(lambda k, j: (k == thr) & (j * KC + col < cand))
            return jnp.where(cnt < need, cand, cut)
        return lax.fori_loop(0, seq.bit_length(), pos_body, jnp.zeros((TQ, 1), jnp.int32))

    cut = lax.cond(any_tie, tie_cut, lambda: jnp.full((TQ, 1), seq, jnp.int32))
    cut = jnp.where(finite, cut, -1)

    q = jnp.concatenate([aq_ref[0, :, LANES * h:LANES * (h + 1)] for h in range(A_HEADS)], axis=0)

    def att_body(j, carry):
        ks = pl.multiple_of(j * KC, KC)
        kv = akv_ref[0, pl.ds(ks, KC), :]
        k = key_scr[j]
        mask = (k > thr) | ((k == thr) & (ks + col <= cut))
        return _online_update(carry, _dot_nt(q, kv), mask, kv, A_HEADS)

    carry = lax.fori_loop(0, nchunk, att_body, _online_init(A_HEADS))
    _store_heads(o_ref, _online_finish(carry, A_HEADS), A_HEADS)


def _dsa(p, batch, seq):
    topk = min(DSA_TOPK_MAX, seq // 4)
    r3 = lambda a: a.reshape(batch, seq, a.shape[-1])
    qblk = lambda c: pl.BlockSpec((1, TQ, c), lambda b, i: (b, i, 0))
    full = lambda c: pl.BlockSpec((1, seq, c), lambda b, i: (b, 0, 0))
    return pl.pallas_call(
        functools.partial(_dsa_body, topk=topk, seq=seq),
        grid=(batch, seq // TQ),
        in_specs=[qblk(4 * 256), full(256), qblk(LANES), qblk(4 * LANES), full(LANES)],
        out_specs=qblk(A_HEADS * HEAD_DIM),
        out_shape=jax.ShapeDtypeStruct((batch, seq, A_HEADS * HEAD_DIM), BF16),
        scratch_shapes=[pltpu.VMEM((seq // KC, TQ, KC), jnp.int32)],
        compiler_params=_cparams(("parallel", "arbitrary")),
        name="dsa",
    )(r3(p["iq"]), r3(p["ik"]), r3(p["misc"]), r3(p["aq"]), r3(p["akv"]))


def _moba_body(bq_ref, bqp_ref, bkv_ref, km_ref, o_ref, *, nb):
    c = pl.program_id(1)
    q0 = c * TQ
    own = q0 // MOBA_BLOCK
    row = q0 + lax.broadcasted_iota(jnp.int32, (TQ, 1), 0)
    blk = _lane_iota((1, nb))
    past = blk < own
    col = _lane_iota((1, MOBA_BLOCK))
    brow = lax.broadcasted_iota(jnp.int32, (nb, MOBA_BLOCK), 0)
    outs = []
    for h in range(B_HEADS):
        qf = bq_ref[0, :, HEAD_DIM * h:HEAD_DIM * (h + 1)]
        km = km_ref[0, :, HEAD_DIM * h:HEAD_DIM * (h + 1)]
        gate = jnp.where(past, _dot_nt(qf, km, precision=HIGHEST), -jnp.inf)
        sel = jnp.where(past, _topn_mask(gate, MOBA_TOPK), 0.0).astype(BF16)
        q = bqp_ref[0, :, LANES * h:LANES * (h + 1)]

        ks = pl.multiple_of(own * MOBA_BLOCK, MOBA_BLOCK)
        kv = bkv_ref[0, pl.ds(ks, MOBA_BLOCK), LANES * h:LANES * (h + 1)]
        carry = _online_update(_online_init(1), _dot_nt(q, kv), ks + col <= row, kv, 1)

        def body(j, carry):
            ks = pl.multiple_of(j * MOBA_BLOCK, MOBA_BLOCK)
            kv = bkv_ref[0, pl.ds(ks, MOBA_BLOCK), LANES * h:LANES * (h + 1)]
            expand = jnp.where(brow == j, 1.0, 0.0).astype(BF16)
            mask = jnp.dot(sel, expand, preferred_element_type=F32) > 0.5
            return _online_update(carry, _dot_nt(q, kv), mask, kv, 1)

        carry = lax.fori_loop(0, own, body, carry)
        outs.append(_online_finish(carry, 1))
    _store_heads(o_ref, jnp.concatenate(outs, axis=0), B_HEADS)


def _moba(p, batch, seq):
    nb = seq // MOBA_BLOCK
    r3 = lambda a: a.reshape(batch, seq, a.shape[-1])
    km = p["km"][:, :PROJ_TM // MOBA_BLOCK].reshape(batch, nb, 256)
    qblk = lambda c: pl.BlockSpec((1, TQ, c), lambda b, i: (b, i, 0))
    return pl.pallas_call(
        functools.partial(_moba_body, nb=nb),
        grid=(batch, seq // TQ),
        in_specs=[qblk(256), qblk(4 * LANES),
                  pl.BlockSpec((1, seq, 4 * LANES), lambda b, i: (b, 0, 0)),
                  pl.BlockSpec((1, nb, 256), lambda b, i: (b, 0, 0))],
        out_specs=qblk(B_HEADS * HEAD_DIM),
        out_shape=jax.ShapeDtypeStruct((batch, seq, B_HEADS * HEAD_DIM), BF16),
        compiler_params=_cparams(("parallel", "arbitrary")),
        name="moba",
    )(r3(p["bq"]), r3(p["bqp"]), r3(p["bkv"]), km)


def _nsa_body(cqn_ref, cqr_ref, misc_ref, kc_ref, vc_ref, cslc_ref, cwin_ref, ov_ref, o_ref,
              *, ncp, ns, n_sel):
    c = pl.program_id(1)
    q0 = c * TQ
    nchunk = (q0 + TQ + KC - 1) // KC
    row = q0 + lax.broadcasted_iota(jnp.int32, (TQ, 1), 0)
    col = _lane_iota((1, KC))
    gates = jax.nn.sigmoid(misc_ref[0])
    cmp_vis = _lane_iota((1, ncp)) * CMP_STRIDE + (CMP_BLOCK - 1) <= row
    blk = _lane_iota((1, ns))
    jq = row // SLC_BLOCK
    adm = blk <= jq
    forced = adm & ((blk == 0) | (blk == jq) | (blk == jq - 1))
    brow = lax.broadcasted_iota(jnp.int32, (ns, KC), 0)
    bcol = lax.broadcasted_iota(jnp.int32, (ns, KC), 1) // SLC_BLOCK
    wstart = pl.multiple_of(jnp.maximum(q0 - WINDOW, 0), TQ)
    wlen = WINDOW + TQ
    wdiff = row - (wstart + _lane_iota((1, wlen)))
    wmask = (wdiff >= 0) & (wdiff < WINDOW)

    for g in range(C_GROUPS):
        qn = jnp.concatenate([cqn_ref[0, :, 256 * (C_REP * g + r):256 * (C_REP * g + r + 1)]
                              for r in range(C_REP)], axis=0)
        s3 = _dot_nt(qn, kc_ref[0, g]).reshape(C_REP, TQ, ncp)
        sm = jnp.where(cmp_vis[None], s3, -jnp.inf)
        m = jnp.max(sm, axis=-1, keepdims=True)
        m = jnp.where(m > -jnp.inf, m, 0.0)
        pc = jnp.where(cmp_vis[None], jnp.exp(s3 - m), 0.0)
        den = jnp.sum(pc, axis=-1, keepdims=True)
        pc = pc / jnp.where(den > 0, den, 1.0)
        o_c = jnp.dot(pc.reshape(C_REP * TQ, ncp).astype(BF16), vc_ref[0, g], preferred_element_type=F32)

        psum = pc[0]
        for r in range(1, C_REP):
            psum = psum + pc[r]
        imp = jnp.dot(psum, ov_ref[...], precision=HIGHEST, preferred_element_type=F32)
        score = jnp.where(adm, imp, -jnp.inf)
        score = jnp.where(forced, FORCE_SCORE, score)
        sel = _topn_mask(score, n_sel).astype(BF16)

        qr = jnp.concatenate([cqr_ref[0, :, LANES * (C_REP * g + r):LANES * (C_REP * g + r + 1)]
                              for r in range(C_REP)], axis=0)

        def slc_body(j, carry):
            ks = pl.multiple_of(j * KC, KC)
            kv = cslc_ref[0, pl.ds(ks, KC), LANES * g:LANES * (g + 1)]
            expand = jnp.where(brow == bcol + j * (KC // SLC_BLOCK), 1.0, 0.0).astype(BF16)
            picked = jnp.dot(sel, expand, preferred_element_type=F32) > 0.5
            mask = picked & (ks + col <= row)
            return _online_update(carry, _dot_nt(qr, kv), mask, kv, C_REP)

        o_s = _online_finish(lax.fori_loop(0, nchunk, slc_body, _online_init(C_REP)), C_REP)

        kvw = cwin_ref[0, pl.ds(wstart, wlen), LANES * g:LANES * (g + 1)]
        o_w = _online_finish(_online_update(_online_init(C_REP), _dot_nt(qr, kvw), wmask, kvw, C_REP), C_REP)

        outs = []
        for r in range(C_REP):
            hh = C_REP * g + r
            gcol = lambda j: gates[:, MISC_CG + 3 * hh + j:MISC_CG + 3 * hh + j + 1]
            rows = slice(TQ * r, TQ * (r + 1))
            outs.append(gcol(0) * o_c[rows] + gcol(1) * o_s[rows] + gcol(2) * o_w[rows])
        _store_heads(o_ref, jnp.concatenate(outs, axis=0), C_REP, col0=C_REP * HEAD_DIM * g)


def _nsa(p, kcmp, vcmp, overlap, batch, seq):
    ncp = seq // CMP_STRIDE
    ns = seq // SLC_BLOCK
    n_sel = min(SLC_TOPN, ns)
    r3 = lambda a: a.reshape(batch, seq, a.shape[-1])
    qblk = lambda c: pl.BlockSpec((1, TQ, c), lambda b, i: (b, i, 0))
    full = lambda c: pl.BlockSpec((1, seq, c), lambda b, i: (b, 0, 0))
    return pl.pallas_call(
        functools.partial(_nsa_body, ncp=ncp, ns=ns, n_sel=n_sel),
        grid=(batch, seq // TQ),
        in_specs=[qblk(8 * 256), qblk(8 * LANES), qblk(LANES),
                  pl.BlockSpec((1, C_GROUPS, ncp, 256), lambda b, i: (b, 0, 0, 0)),
                  pl.BlockSpec((1, C_GROUPS, ncp, LANES), lambda b, i: (b, 0, 0, 0)),
                  full(2 * LANES), full(2 * LANES),
                  pl.BlockSpec((ncp, ns), lambda b, i: (0, 0))],
        out_specs=qblk(C_HEADS * HEAD_DIM),
        out_shape=jax.ShapeDtypeStruct((batch, seq, C_HEADS * HEAD_DIM), BF16),
        compiler_params=_cparams(("parallel", "arbitrary")),
        name="nsa",
    )(r3(p["cqn"]), r3(p["cqr"]), r3(p["misc"]), kcmp, vcmp, r3(p["cslc"]), r3(p["cwin"]), overlap)


def _merge_body(x_ref, g_ref, oa_ref, ob_ref, oc_ref, wm_ref, wa_ref, wb_ref, wc_ref, wo_ref, o_ref):
    x = x_ref[...]
    d = x.shape[1]
    h = _rms(x, g_ref[...]).astype(BF16)
    merged = None
    for i, (o_r, w_r) in enumerate(((oa_ref, wa_ref), (ob_ref, wb_ref), (oc_ref, wc_ref))):
        gate = jax.nn.sigmoid(jnp.dot(h, wm_ref[:, d * i:d * (i + 1)], preferred_element_type=F32))
        y = gate * jnp.dot(o_r[...], w_r[...], preferred_element_type=F32)
        merged = y if merged is None else merged + y
    o_ref[...] = x + jnp.dot(merged.astype(BF16), wo_ref[...], preferred_element_type=F32)


def _merge(x2, g, oa, ob, oc, wm, wa, wb, wc, wo):
    m, d = x2.shape
    tm = MERGE_TM
    row = lambda c: pl.BlockSpec((tm, c), lambda i: (i, 0))
    const = lambda a: pl.BlockSpec(a.shape, lambda i: (0, 0))
    return pl.pallas_call(
        _merge_body,
        grid=(m // tm,),
        in_specs=[row(d), const(g), row(oa.shape[1]), row(ob.shape[1]), row(oc.shape[1]),
                  const(wm), const(wa), const(wb), const(wc), const(wo)],
        out_specs=row(d),
        out_shape=jax.ShapeDtypeStruct((m, d), F32),
        compiler_params=_cparams(("parallel",)),
        name="merge",
    )(x2, g, oa, ob, oc, wm, wa, wb, wc, wo)


def _rope_tables(seq):
    half = HEAD_DIM // 2
    inv = ROPE_THETA ** (-jnp.arange(half, dtype=F32) / half)
    ang = jnp.arange(seq, dtype=F32)[:, None] * inv[None, :]
    lane = np.arange(LANES)
    ang = ang[:, lane % half]
    second = jnp.asarray((lane % HEAD_DIM) >= half)[None, :]
    sin = jnp.sin(ang)
    return jnp.cos(ang), jnp.where(second, sin, 0.0), jnp.where(second, 0.0, -sin)


def _overlap(seq):
    ncp = seq // CMP_STRIDE
    nc = (seq - CMP_BLOCK) // CMP_STRIDE + 1
    ns = seq // SLC_BLOCK
    cs = np.arange(ncp) * CMP_STRIDE
    ss = np.arange(ns) * SLC_BLOCK
    ov = (cs[:, None] < ss[None, :] + SLC_BLOCK) & (ss[None, :] <= cs[:, None] + CMP_BLOCK - 1)
    ov &= (np.arange(ncp) < nc)[:, None]
    return jnp.asarray(ov.astype(np.float32))


def kernel(x, ffn1_norm, ffn1_w_gate, ffn1_w_up, ffn1_w_down, mix_norm, w_in, cmp_pos_k, cmp_w1_k, cmp_w2_k, cmp_pos_v, cmp_w1_v, cmp_w2_v, w_branch_a, w_branch_b, w_branch_c, w_out, ffn2_norm, ffn2_w_gate, ffn2_w_up, ffn2_w_down, final_norm):
    batch, seq, d = x.shape
    depth = w_in.shape[0]
    assert seq % max(KC, PROJ_TM) == 0 and seq >= WINDOW + TQ and w_in.shape[2] == _N_IN
    nc = (seq - CMP_BLOCK) // CMP_STRIDE + 1
    ncp = seq // CMP_STRIDE

    perm = _in_perm()
    w_perm = jnp.where(jnp.asarray(perm >= 0)[None, None, :],
                       jnp.take(w_in, jnp.asarray(np.maximum(perm, 0)), axis=2), 0.0).astype(BF16)
    w_mg = w_in[:, :, _O_MG:].astype(BF16)
    half_rows = CMP_BLOCK * HEAD_DIM // 2
    w1 = jnp.stack([cmp_w1_k, cmp_w1_v], axis=1)
    w1 = jnp.concatenate([w1[:, :, :half_rows], w1[:, :, half_rows:]], axis=-1)
    w2 = jnp.pad(jnp.stack([cmp_w2_k, cmp_w2_v], axis=1), ((0, 0), (0, 0), (0, 0), (0, LANES - HEAD_DIM)))
    pos = jnp.stack([cmp_pos_k, cmp_pos_v], axis=1).reshape(depth, 2, 2, half_rows)
    pos = jnp.pad(pos, ((0, 0), (0, 0), (0, 6), (0, 0)))
    cos, sa, sb = _rope_tables(seq)
    overlap = _overlap(seq)
    bf = lambda a: a.astype(BF16)
    row = lambda a: a.reshape(1, d)

    x2 = x.reshape(batch * seq, d)
    for l in range(depth):
        x2 = _ffn(x2, row(ffn1_norm[l]), bf(ffn1_w_gate[l]), bf(ffn1_w_up[l]), bf(ffn1_w_down[l]),
                  row(final_norm), False)
        p = _inproj(x2, row(mix_norm[l]), w_perm[l], cos, sa, sb, seq)
        xc = p["ccmp"].reshape(batch, seq, 4, HEAD_DIM).transpose(0, 2, 1, 3).reshape(batch, 4, ncp, 1024)
        kc, vc = _compress(xc, w1[l], w2[l], pos[l], nc)
        o_a = _dsa(p, batch, seq)
        o_b = _moba(p, batch, seq)
        o_c = _nsa(p, kc[:, :C_GROUPS], vc[:, C_GROUPS:], overlap, batch, seq)
        flat = lambda a: a.reshape(batch * seq, a.shape[-1])
        x2 = _merge(x2, row(mix_norm[l]), flat(o_a), flat(o_b), flat(o_c), w_mg[l],
                    bf(w_branch_a[l]), bf(w_branch_b[l]), bf(w_branch_c[l]), bf(w_out[l]))
        x2 = _ffn(x2, row(ffn2_norm[l]), bf(ffn2_w_gate[l]), bf(ffn2_w_up[l]), bf(ffn2_w_down[l]),
                  row(final_norm), l == depth - 1)
    return x2.reshape(batch, seq, d)
```

```python
import functools
import math

import numpy as np
import jax
import jax.numpy as jnp
from jax import lax
from jax.experimental import pallas as pl
from jax.experimental.pallas import tpu as pltpu

HEAD_DIM = 64
ROPE_THETA = 10000.0
NORM_EPS = 1e-6
A_HEADS = 4
IDX_HEADS = 4
IDX_DIM = 64
DSA_TOPK_MAX = 256
B_HEADS = 4
MOBA_BLOCK = 256
MOBA_TOPK = 3
C_HEADS = 8
C_GROUPS = 2
C_REP = C_HEADS // C_GROUPS
CMP_BLOCK = 32
CMP_STRIDE = 16
CMP_HIDDEN = 128
SLC_BLOCK = 64
SLC_TOPN = 16
WINDOW = 512
FORCE_SCORE = 1e30

LANES = 128
VMEM_LIMIT = 56 * 1024 * 1024

TQ = 128
KC = 512
FFN_TM = 512
PROJ_TM = 512
MERGE_TM = 512

NEG_BIG = -1e30
QSCALE = HEAD_DIM ** -0.5 * math.log2(math.e)
F32 = jnp.float32
BF16 = jnp.bfloat16
HIGHEST = lax.Precision.HIGHEST

_O_AQ = 0
_O_AK = 256
_O_AV = 320
_O_IQ = 384
_O_IK = 640
_O_IW = 704
_O_BQ = 708
_O_BK = 964
_O_BV = 1220
_O_CQ = 1476
_O_CKV = 1988
_O_CG = 2756
_O_MG = 2780
_N_IN = 5852

P_AQ = 0
P_AK = 256
P_IK = 320
P_IQ = 384
P_BQ = 640
P_BK = 896
P_CKS = 1152
P_CKW = 1280
P_CQ = 1408
P_ROPE_END = 1920
P_BV = 1920
P_KCMP = 2176
P_VS = 2432
P_VW = 2560
P_AV = 2688
P_MISC = 2752
P_COLS = 2816
MISC_CG = 0
MISC_IW = 24


def _in_perm():
    perm = -np.ones((P_COLS,), np.int64)

    def put(dst, src, n):
        perm[dst:dst + n] = np.arange(src, src + n)

    put(P_AQ, _O_AQ, 256)
    put(P_AK, _O_AK, 64)
    put(P_IK, _O_IK, 64)
    put(P_IQ, _O_IQ, 256)
    put(P_BQ, _O_BQ, 256)
    put(P_BK, _O_BK, 256)
    ckv = lambda s, g: _O_CKV + (s * C_GROUPS + g) * HEAD_DIM
    for g in range(C_GROUPS):
        put(P_CKS + 64 * g, ckv(2, g), 64)
        put(P_CKW + 64 * g, ckv(4, g), 64)
        put(P_KCMP + 64 * g, ckv(0, g), 64)
        put(P_KCMP + 128 + 64 * g, ckv(1, g), 64)
        put(P_VS + 64 * g, ckv(3, g), 64)
        put(P_VW + 64 * g, ckv(5, g), 64)
    put(P_CQ, _O_CQ, 512)
    put(P_BV, _O_BV, 256)
    put(P_AV, _O_AV, 64)
    put(P_MISC + MISC_CG, _O_CG, 24)
    put(P_MISC + MISC_IW, _O_IW, 4)
    return perm


def _cparams(sem):
    return pltpu.CompilerParams(dimension_semantics=sem, vmem_limit_bytes=VMEM_LIMIT)


def _rms(x, g):
    y = x * lax.rsqrt(jnp.mean(x * x, axis=-1, keepdims=True) + NORM_EPS)
    return y * g


def _dot_nt(a, b, precision=None):
    return lax.dot_general(a, b, (((1,), (1,)), ((), ())), precision=precision,
                           preferred_element_type=F32)


def _ffn_body(x_ref, g_ref, wg_ref, wu_ref, wd_ref, fg_ref, o_ref, h_scr, acc_scr, *, final_norm):
    f = pl.program_id(1)

    @pl.when(f == 0)
    def _():
        h_scr[...] = _rms(x_ref[...], g_ref[...]).astype(BF16)
        acc_scr[...] = jnp.zeros_like(acc_scr)

    h = h_scr[...]
    a = jnp.dot(h, wg_ref[...], preferred_element_type=F32)
    u = jnp.dot(h, wu_ref[...], preferred_element_type=F32)
    act = (a * jax.nn.sigmoid(a) * u).astype(BF16)
    acc_scr[...] += jnp.dot(act, wd_ref[...], preferred_element_type=F32)

    @pl.when(f == pl.num_programs(1) - 1)
    def _():
        y = x_ref[...] + 0.5 * acc_scr[...]
        if final_norm:
            y = _rms(y, fg_ref[...])
        o_ref[...] = y


def _ffn(x2, g, wg, wu, wd, fg, final_norm):
    m, d = x2.shape
    dff = wg.shape[1]
    tf = dff // 2 if (dff // 2) % LANES == 0 else dff
    tm = FFN_TM
    return pl.pallas_call(
        functools.partial(_ffn_body, final_norm=final_norm),
        grid=(m // tm, dff // tf),
        in_specs=[
            pl.BlockSpec((tm, d), lambda i, f: (i, 0)),
            pl.BlockSpec((1, d), lambda i, f: (0, 0)),
            pl.BlockSpec((d, tf), lambda i, f: (0, f)),
            pl.BlockSpec((d, tf), lambda i, f: (0, f)),
            pl.BlockSpec((tf, d), lambda i, f: (f, 0)),
            pl.BlockSpec((1, d), lambda i, f: (0, 0)),
        ],
        out_specs=pl.BlockSpec((tm, d), lambda i, f: (i, 0)),
        out_shape=jax.ShapeDtypeStruct((m, d), F32),
        scratch_shapes=[pltpu.VMEM((tm, d), BF16), pltpu.VMEM((tm, d), F32)],
        compiler_params=_cparams(("parallel", "arbitrary")),
        name="ffn",
    )(x2, g, wg, wu, wd, fg)


def _lane_iota(shape):
    return lax.broadcasted_iota(jnp.int32, shape, len(shape) - 1)


def _split_hi_lo(x):
    hi = x.astype(BF16).astype(F32)
    return hi, x - hi


def _inproj_body(x_ref, g_ref, w_ref, cos_ref, sa_ref, sb_ref,
                 aq_ref, akv_ref, iq_ref, ik_ref, misc_ref, bq_ref, bqp_ref, bkv_ref, km_ref,
                 cqn_ref, cqr_ref, cslc_ref, cwin_ref, ccmp_ref):
    tm = x_ref.shape[0]
    h = _rms(x_ref[...], g_ref[...]).astype(BF16)
    cos, sa, sb = cos_ref[...], sa_ref[...], sb_ref[...]
    lane = _lane_iota((tm, LANES))
    low = lane < HEAD_DIM

    raw = []
    rot = []
    for j in range(P_COLS // 256):
        z = jnp.dot(h, w_ref[:, 256 * j:256 * (j + 1)], preferred_element_type=F32)
        for half in range(2):
            p = z[:, LANES * half:LANES * (half + 1)]
            raw.append(p)
            if LANES * len(raw) <= P_ROPE_END:
                rot.append(p * cos + pltpu.roll(p, 32, axis=1) * sa + pltpu.roll(p, 96, axis=1) * sb)

    def pick(pieces, col):
        p = pieces[col // LANES]
        return pltpu.roll(p, HEAD_DIM, axis=1) if col % LANES else p

    def join(lo, hi=None):
        if hi is None:
            return jnp.where(low, lo, 0.0)
        return jnp.where(low, lo, pltpu.roll(hi, HEAD_DIM, axis=1))

    scale = QSCALE

    for hh in range(A_HEADS):
        aq_ref[:, LANES * hh:LANES * (hh + 1)] = join(pick(rot, P_AQ + 64 * hh) * scale).astype(BF16)
        qhi, qlo = _split_hi_lo(pick(rot, P_IQ + 64 * hh))
        iq_ref[:, 256 * hh:256 * hh + LANES] = join(qhi, qlo).astype(BF16)
        iq_ref[:, 256 * hh + LANES:256 * (hh + 1)] = join(qhi).astype(BF16)
    akv_ref[...] = join(pick(rot, P_AK), pick(raw, P_AV)).astype(BF16)
    khi, klo = _split_hi_lo(pick(rot, P_IK))
    ik_ref[:, :LANES] = join(khi, khi).astype(BF16)
    ik_ref[:, LANES:] = join(klo).astype(BF16)
    misc_ref[...] = pick(raw, P_MISC)

    bq_ref[:, :LANES] = rot[P_BQ // LANES]
    bq_ref[:, LANES:] = rot[P_BQ // LANES + 1]
    for hh in range(B_HEADS):
        bqp_ref[:, LANES * hh:LANES * (hh + 1)] = join(pick(rot, P_BQ + 64 * hh) * scale).astype(BF16)
        bkv_ref[:, LANES * hh:LANES * (hh + 1)] = join(pick(rot, P_BK + 64 * hh),
                                                      pick(raw, P_BV + 64 * hh)).astype(BF16)
    nblk = tm // MOBA_BLOCK
    km_ref[...] = jnp.zeros_like(km_ref)
    for half in range(2):
        kp = rot[P_BK // LANES + half]
        for b in range(nblk):
            km_ref[0, b:b + 1, LANES * half:LANES * (half + 1)] = jnp.mean(
                kp[MOBA_BLOCK * b:MOBA_BLOCK * (b + 1)], axis=0, keepdims=True)

    for hh in range(C_HEADS):
        nhi, nlo = _split_hi_lo(pick(raw, P_CQ + 64 * hh) * scale)
        cqn_ref[:, 256 * hh:256 * hh + LANES] = join(nhi, nlo).astype(BF16)
        cqn_ref[:, 256 * hh + LANES:256 * (hh + 1)] = join(nhi).astype(BF16)
        cqr_ref[:, LANES * hh:LANES * (hh + 1)] = join(pick(rot, P_CQ + 64 * hh) * scale).astype(BF16)
    for g in range(C_GROUPS):
        cslc_ref[:, LANES * g:LANES * (g + 1)] = join(pick(rot, P_CKS + 64 * g),
                                                     pick(raw, P_VS + 64 * g)).astype(BF16)
        cwin_ref[:, LANES * g:LANES * (g + 1)] = join(pick(rot, P_CKW + 64 * g),
                                                     pick(raw, P_VW + 64 * g)).astype(BF16)
    ccmp_ref[:, :LANES] = raw[P_KCMP // LANES]
    ccmp_ref[:, LANES:] = raw[P_KCMP // LANES + 1]


def _inproj(x2, g, w, cos, sa, sb, seq):
    m, d = x2.shape
    tm = PROJ_TM
    nt = seq // tm
    row = lambda c: pl.BlockSpec((tm, c), lambda i: (i, 0))
    tab = pl.BlockSpec((tm, LANES), lambda i: (i % nt, 0))
    outs = [
        ("aq", 4 * LANES, BF16), ("akv", LANES, BF16), ("iq", 4 * 256, BF16), ("ik", 256, BF16),
        ("misc", LANES, F32), ("bq", 256, F32), ("bqp", 4 * LANES, BF16), ("bkv", 4 * LANES, BF16),
        ("km", None, F32),
        ("cqn", 8 * 256, BF16), ("cqr", 8 * LANES, BF16), ("cslc", 2 * LANES, BF16),
        ("cwin", 2 * LANES, BF16), ("ccmp", 256, F32),
    ]
    out_specs, out_shape = [], []
    for name, c, dt in outs:
        if name == "km":
            out_specs.append(pl.BlockSpec((1, 8, 256), lambda i: (i, 0, 0)))
            out_shape.append(jax.ShapeDtypeStruct((m // tm, 8, 256), dt))
        else:
            out_specs.append(row(c))
            out_shape.append(jax.ShapeDtypeStruct((m, c), dt))
    res = pl.pallas_call(
        _inproj_body,
        grid=(m // tm,),
        in_specs=[row(d), pl.BlockSpec((1, d), lambda i: (0, 0)),
                  pl.BlockSpec((d, P_COLS), lambda i: (0, 0)), tab, tab, tab],
        out_specs=out_specs,
        out_shape=out_shape,
        compiler_params=_cparams(("parallel",)),
        name="inproj",
    )(x2, g, w, cos, sa, sb)
    return dict(zip([o[0] for o in outs], res))


def _compress_body(x_ref, w1_ref, w2_ref, pos_ref, ok_ref, ov_ref, *, n_valid):
    x = x_ref[0, 0]
    w1 = w1_ref[0]
    pre = jnp.dot(x, w1, precision=HIGHEST, preferred_element_type=F32)
    pp = jnp.dot(pos_ref[0], w1, precision=HIGHEST, preferred_element_type=F32)
    posb = pp[0:1, :CMP_HIDDEN] + pp[1:2, CMP_HIDDEN:]
    ncp = x.shape[0]
    upper = pre[:, :CMP_HIDDEN]
    lower_next = pltpu.roll(pre[:, CMP_HIDDEN:], ncp - 1, axis=0)
    hid = jax.nn.gelu(upper + lower_next + posb)
    out = jnp.dot(hid, w2_ref[0], precision=HIGHEST, preferred_element_type=F32)
    rows = lax.broadcasted_iota(jnp.int32, out.shape, 0)
    out = jnp.where(rows < n_valid, out, 0.0)
    low = _lane_iota(out.shape) < HEAD_DIM
    hi, lo = _split_hi_lo(out)
    ok_ref[0, 0, :, :LANES] = jnp.where(low, hi, pltpu.roll(hi, HEAD_DIM, axis=1)).astype(BF16)
    ok_ref[0, 0, :, LANES:] = lo.astype(BF16)
    ov_ref[0, 0] = pltpu.roll(out, HEAD_DIM, axis=1).astype(BF16)


def _compress(xc, w1, w2, pos, n_valid):
    b, four, ncp, _ = xc.shape
    return pl.pallas_call(
        functools.partial(_compress_body, n_valid=n_valid),
        grid=(b, four),
        in_specs=[
            pl.BlockSpec((1, 1, ncp, 1024), lambda i, j: (i, j, 0, 0)),
            pl.BlockSpec((1, 1024, 256), lambda i, j: (j // 2, 0, 0)),
            pl.BlockSpec((1, CMP_HIDDEN, LANES), lambda i, j: (j // 2, 0, 0)),
            pl.BlockSpec((1, 8, 1024), lambda i, j: (j // 2, 0, 0)),
        ],
        out_specs=[pl.BlockSpec((1, 1, ncp, 256), lambda i, j: (i, j, 0, 0)),
                   pl.BlockSpec((1, 1, ncp, LANES), lambda i, j: (i, j, 0, 0))],
        out_shape=[jax.ShapeDtypeStruct((b, four, ncp, 256), BF16),
                   jax.ShapeDtypeStruct((b, four, ncp, LANES), BF16)],
        compiler_params=_cparams(("parallel", "parallel")),
        name="compress",
    )(xc, w1, w2, pos)


def _softmax_step(carry, s3, pv):
    m_old, l_old, acc = carry
    heads = s3.shape[0]
    m_new = jnp.maximum(m_old, jnp.max(s3, axis=-1, keepdims=True))
    p = jnp.exp2(s3 - m_new)
    alpha = jnp.exp2(m_old - m_new)
    l_new = alpha * l_old + jnp.sum(p, axis=-1, keepdims=True)
    acc = alpha.reshape(heads * TQ, 1) * acc + pv(p.astype(BF16))
    return m_new, l_new, acc


def _online_init(heads):
    return (jnp.full((heads, TQ, 1), NEG_BIG, F32), jnp.zeros((heads, TQ, 1), F32),
            jnp.zeros((heads * TQ, LANES), F32))


def _online_finish(carry, heads):
    _, l, acc = carry
    return acc / l.reshape(heads * TQ, 1)


def _pv_shared(kv):
    return lambda p: jnp.dot(p.reshape(p.shape[0] * TQ, p.shape[2]), kv, preferred_element_type=F32)


def _store_heads(o_ref, o, heads, col0=0):
    low = _lane_iota((TQ, LANES)) < HEAD_DIM
    for pair in range(heads // 2):
        even = o[TQ * (2 * pair):TQ * (2 * pair + 1)]
        odd = o[TQ * (2 * pair + 1):TQ * (2 * pair + 2)]
        piece = jnp.where(low, pltpu.roll(even, HEAD_DIM, axis=1), odd)
        o_ref[0, :, col0 + LANES * pair:col0 + LANES * (pair + 1)] = piece.astype(o_ref.dtype)


def _topn_mask(score, n_pick):
    idx = _lane_iota(score.shape)
    width = score.shape[-1]
    sel = jnp.zeros(score.shape, F32)
    for _ in range(n_pick):
        m = jnp.max(score, axis=-1, keepdims=True)
        first = jnp.min(jnp.where(score == m, idx, width), axis=-1, keepdims=True)
        hit = idx == first
        sel = jnp.where(hit, 1.0, sel)
        score = jnp.where(hit, -jnp.inf, score)
    return sel


_NEG_INF_KEY = -2139095041
_INT_MIN = -2147483648


def _dsa_body(iq_ref, ik_ref, misc_ref, aq_ref, akv_ref, o_ref, key_scr, *, topk, seq):
    c = pl.program_id(1)
    q0 = c * TQ
    nchunk = (q0 + TQ + KC - 1) // KC
    row = q0 + lax.broadcasted_iota(jnp.int32, (TQ, 1), 0)
    col = _lane_iota((1, KC))
    lane = _lane_iota((1, LANES))
    idx_scale = (IDX_HEADS * IDX_DIM) ** -0.5

    iq = jnp.concatenate([iq_ref[0, :, 256 * h:256 * (h + 1)] for h in range(IDX_HEADS)], axis=0)
    misc = misc_ref[0]
    iw = [misc[:, MISC_IW + h:MISC_IW + h + 1] for h in range(IDX_HEADS)]

    def score_body(j, _):
        ks = pl.multiple_of(j * KC, KC)
        lg = jnp.maximum(_dot_nt(iq, ik_ref[0, pl.ds(ks, KC), :]), 0.0)
        sc = iw[0] * lg[0:TQ]
        for h in range(1, IDX_HEADS):
            sc = sc + iw[h] * lg[TQ * h:TQ * (h + 1)]
        sc = sc * idx_scale
        sc = jnp.where(sc == 0.0, 0.0, sc)
        sc = jnp.where(ks + col <= row, sc, -jnp.inf)
        bits = lax.bitcast_convert_type(sc, jnp.int32)
        key_scr[j] = jnp.where(bits < 0, bits ^ jnp.int32(0x7FFFFFFF), bits)
        return 0

    lax.fori_loop(0, nchunk, score_body, 0)

    def count(pred):
        def body(j, acc):
            for q in range(KC // LANES):
                k = key_scr[j, :, LANES * q:LANES * (q + 1)]
                acc = acc + jnp.where(pred(k, j * KC + LANES * q), 1.0, 0.0)
            return acc
        acc = lax.fori_loop(0, nchunk, body, jnp.zeros((TQ, LANES), F32))
        return jnp.sum(acc, axis=-1, keepdims=True)

    def bit_body(i, t):
        cand = t + lax.shift_left(jnp.int32(1), 31 - i)
        cnt = count(lambda k, p0: k >= cand)
        return jnp.where(cnt >= topk, cand, t)

    thr = lax.fori_loop(0, 32, bit_body, jnp.full((TQ, 1), _INT_MIN, jnp.int32))

    n_gt = count(lambda k, p0: k > thr)
    n_ge = count(lambda k, p0: k >= thr)
    need = topk - n_gt
    finite = thr > _NEG_INF_KEY
    any_tie = jnp.max(jnp.where(finite, n_ge, 0.0)) > topk

    def tie_cut():
        def pos_body(i, cut):
            cand = cut + lax.shift_left(jnp.int32(1), (seq.bit_length() - 1) - i)
            cnt = count(lambda k, p0: (k == thr) & (p0 + lane < cand))
            return jnp.where(cnt < need, cand, cut)
        return lax.fori_loop(0, seq.bit_length(), pos_body, jnp.zeros((TQ, 1), jnp.int32))

    cut = lax.cond(any_tie, tie_cut, lambda: jnp.full((TQ, 1), seq, jnp.int32))
    cut = jnp.where(finite, cut, -1)

    q = jnp.concatenate([aq_ref[0, :, LANES * h:LANES * (h + 1)] for h in range(A_HEADS)], axis=0)

    def att_body(j, carry):
        ks = pl.multiple_of(j * KC, KC)
        kv = akv_ref[0, pl.ds(ks, KC), :]
        k = key_scr[j]
        taken = (k > thr) | ((k == thr) & (ks + col <= cut))
        bias = jnp.where(taken, 0.0, NEG_BIG)
        s3 = _dot_nt(q, kv).reshape(A_HEADS, TQ, KC) + bias[None]
        return _softmax_step(carry, s3, _pv_shared(kv))

    carry = lax.fori_loop(0, nchunk, att_body, _online_init(A_HEADS))
    _store_heads(o_ref, _online_finish(carry, A_HEADS), A_HEADS)


def _dsa(p, batch, seq):
    topk = min(DSA_TOPK_MAX, seq // 4)
    r3 = lambda a: a.reshape(batch, seq, a.shape[-1])
    qblk = lambda c: pl.BlockSpec((1, TQ, c), lambda b, i: (b, i, 0))
    full = lambda c: pl.BlockSpec((1, seq, c), lambda b, i: (b, 0, 0))
    return pl.pallas_call(
        functools.partial(_dsa_body, topk=topk, seq=seq),
        grid=(batch, seq // TQ),
        in_specs=[qblk(4 * 256), full(256), qblk(LANES), qblk(4 * LANES), full(LANES)],
        out_specs=qblk(A_HEADS * HEAD_DIM),
        out_shape=jax.ShapeDtypeStruct((batch, seq, A_HEADS * HEAD_DIM), BF16),
        scratch_shapes=[pltpu.VMEM((seq // KC, TQ, KC), jnp.int32)],
        compiler_params=_cparams(("parallel", "arbitrary")),
        name="dsa",
    )(r3(p["iq"]), r3(p["ik"]), r3(p["misc"]), r3(p["aq"]), r3(p["akv"]))


def _pv_per_head(kvs):
    return lambda p: jnp.concatenate(
        [jnp.dot(p[h], kvs[h], preferred_element_type=F32) for h in range(len(kvs))], axis=0)


def _moba_body(bq_ref, bqp_ref, bkv_ref, km_ref, hot_ref, o_ref):
    c = pl.program_id(1)
    q0 = c * TQ
    own = q0 // MOBA_BLOCK
    row = q0 + lax.broadcasted_iota(jnp.int32, (TQ, 1), 0)
    past = _lane_iota((1, LANES)) < own
    col = _lane_iota((1, MOBA_BLOCK))
    ks_own = pl.multiple_of(own * MOBA_BLOCK, MOBA_BLOCK)
    heads = range(B_HEADS)
    kv_at = lambda ks, h: bkv_ref[0, pl.ds(ks, MOBA_BLOCK), LANES * h:LANES * (h + 1)]

    q_aug, s_own = [], []
    for h in heads:
        qf = bq_ref[0, :, HEAD_DIM * h:HEAD_DIM * (h + 1)]
        km = km_ref[0, :, HEAD_DIM * h:HEAD_DIM * (h + 1)]
        gate = jnp.where(past, _dot_nt(qf, km, precision=HIGHEST), -jnp.inf)
        picked = past & (_topn_mask(gate, MOBA_TOPK) > 0.0)
        q = bqp_ref[0, :, LANES * h:LANES * (h + 1)]
        q_aug.append(jnp.concatenate([q, jnp.where(picked, 0.0, NEG_BIG).astype(BF16)], axis=1))
        s_own.append(jnp.where(ks_own + col <= row, _dot_nt(q, kv_at(ks_own, h)), NEG_BIG))

    stack = lambda xs: jnp.concatenate(xs, axis=0).reshape(B_HEADS, TQ, MOBA_BLOCK)
    carry = _softmax_step(_online_init(B_HEADS), stack(s_own),
                          _pv_per_head([kv_at(ks_own, h) for h in heads]))

    def body(j, carry):
        ks = pl.multiple_of(j * MOBA_BLOCK, MOBA_BLOCK)
        hot = hot_ref[pl.ds(ks, MOBA_BLOCK), :]
        kvs = [kv_at(ks, h) for h in heads]
        s3 = stack([_dot_nt(q_aug[h], jnp.concatenate([kvs[h], hot], axis=1)) for h in heads])
        return _softmax_step(carry, s3, _pv_per_head(kvs))

    carry = lax.fori_loop(0, own, body, carry)
    _store_heads(o_ref, _online_finish(carry, B_HEADS), B_HEADS)


def _moba(p, hot, batch, seq):
    nb = seq // MOBA_BLOCK
    r3 = lambda a: a.reshape(batch, seq, a.shape[-1])
    km = p["km"][:, :PROJ_TM // MOBA_BLOCK].reshape(batch, nb, 256)
    km = jnp.pad(km, ((0, 0), (0, LANES - nb), (0, 0)))
    qblk = lambda c: pl.BlockSpec((1, TQ, c), lambda b, i: (b, i, 0))
    return pl.pallas_call(
        _moba_body,
        grid=(batch, seq // TQ),
        in_specs=[qblk(256), qblk(4 * LANES),
                  pl.BlockSpec((1, seq, 4 * LANES), lambda b, i: (b, 0, 0)),
                  pl.BlockSpec((1, LANES, 256), lambda b, i: (b, 0, 0)),
                  pl.BlockSpec((seq, LANES), lambda b, i: (0, 0))],
        out_specs=qblk(B_HEADS * HEAD_DIM),
        out_shape=jax.ShapeDtypeStruct((batch, seq, B_HEADS * HEAD_DIM), BF16),
        compiler_params=_cparams(("parallel", "arbitrary")),
        name="moba",
    )(r3(p["bq"]), r3(p["bqp"]), r3(p["bkv"]), km, hot)


def _nsa_body(cqn_ref, cqr_ref, misc_ref, kc_ref, vc_ref, cslc_ref, cwin_ref, ov_ref, hot_ref, o_ref,
              *, ncp, n_sel):
    c = pl.program_id(1)
    q0 = c * TQ
    last = (q0 + TQ - 1) // KC
    ks_last = pl.multiple_of(last * KC, KC)
    row = q0 + lax.broadcasted_iota(jnp.int32, (TQ, 1), 0)
    col = _lane_iota((1, KC))
    gates = jax.nn.sigmoid(misc_ref[0])
    cmp_vis = _lane_iota((1, ncp)) * CMP_STRIDE + (CMP_BLOCK - 1) <= row
    blk = _lane_iota((1, LANES))
    jq = row // SLC_BLOCK
    adm = blk <= jq
    forced = adm & ((blk == 0) | (blk == jq) | (blk == jq - 1))
    wstart = pl.multiple_of(jnp.maximum(q0 - WINDOW, 0), TQ)
    wlen = WINDOW + TQ
    wdiff = row - (wstart + _lane_iota((1, wlen)))
    wmask = (wdiff >= 0) & (wdiff < WINDOW)

    for g in range(C_GROUPS):
        qn = jnp.concatenate([cqn_ref[0, :, 256 * (C_REP * g + r):256 * (C_REP * g + r + 1)]
                              for r in range(C_REP)], axis=0)
        s3 = _dot_nt(qn, kc_ref[0, g]).reshape(C_REP, TQ, ncp)
        m = jnp.max(jnp.where(cmp_vis[None], s3, -jnp.inf), axis=-1, keepdims=True)
        m = jnp.where(m > -jnp.inf, m, 0.0)
        pc = jnp.where(cmp_vis[None], jnp.exp2(s3 - m), 0.0)
        den = jnp.sum(pc, axis=-1, keepdims=True)
        pc = pc / jnp.where(den > 0, den, 1.0)
        o_c = jnp.dot(pc.reshape(C_REP * TQ, ncp).astype(BF16), vc_ref[0, g], preferred_element_type=F32)

        psum = pc[0]
        for r in range(1, C_REP):
            psum = psum + pc[r]
        imp = jnp.dot(psum, ov_ref[...], precision=HIGHEST, preferred_element_type=F32)
        score = jnp.where(forced, FORCE_SCORE, jnp.where(adm, imp, -jnp.inf))
        picked = adm & (_topn_mask(score, n_sel) > 0.0)
        bias = jnp.where(picked, 0.0, NEG_BIG).astype(BF16)

        qr = jnp.concatenate([cqr_ref[0, :, LANES * (C_REP * g + r):LANES * (C_REP * g + r + 1)]
                              for r in range(C_REP)], axis=0)
        qa = jnp.concatenate([qr, jnp.concatenate([bias] * C_REP, axis=0)], axis=1)

        def slc_scores(ks):
            kv = cslc_ref[0, pl.ds(ks, KC), LANES * g:LANES * (g + 1)]
            kh = jnp.concatenate([kv, hot_ref[pl.ds(ks, KC), :]], axis=1)
            return _dot_nt(qa, kh).reshape(C_REP, TQ, KC), kv

        def slc_body(j, carry):
            s3, kv = slc_scores(pl.multiple_of(j * KC, KC))
            return _softmax_step(carry, s3, _pv_shared(kv))

        carry = lax.fori_loop(0, last, slc_body, _online_init(C_REP))
        s3, kv = slc_scores(ks_last)
        s3 = jnp.where((ks_last + col <= row)[None], s3, NEG_BIG)
        o_s = _online_finish(_softmax_step(carry, s3, _pv_shared(kv)), C_REP)

        kvw = cwin_ref[0, pl.ds(wstart, wlen), LANES * g:LANES * (g + 1)]
        s3 = jnp.where(wmask[None], _dot_nt(qr, kvw).reshape(C_REP, TQ, wlen), NEG_BIG)
        o_w = _online_finish(_softmax_step(_online_init(C_REP), s3, _pv_shared(kvw)), C_REP)

        outs = []
        for r in range(C_REP):
            hh = C_REP * g + r
            gcol = lambda j: gates[:, MISC_CG + 3 * hh + j:MISC_CG + 3 * hh + j + 1]
            rows = slice(TQ * r, TQ * (r + 1))
            outs.append(gcol(0) * o_c[rows] + gcol(1) * o_s[rows] + gcol(2) * o_w[rows])
        _store_heads(o_ref, jnp.concatenate(outs, axis=0), C_REP, col0=C_REP * HEAD_DIM * g)


def _nsa(p, kcmp, vcmp, overlap, hot, batch, seq):
    ncp = seq // CMP_STRIDE
    n_sel = min(SLC_TOPN, seq // SLC_BLOCK)
    r3 = lambda a: a.reshape(batch, seq, a.shape[-1])
    qblk = lambda c: pl.BlockSpec((1, TQ, c), lambda b, i: (b, i, 0))
    full = lambda c: pl.BlockSpec((1, seq, c), lambda b, i: (b, 0, 0))
    return pl.pallas_call(
        functools.partial(_nsa_body, ncp=ncp, n_sel=n_sel),
        grid=(batch, seq // TQ),
        in_specs=[qblk(8 * 256), qblk(8 * LANES), qblk(LANES),
                  pl.BlockSpec((1, C_GROUPS, ncp, 256), lambda b, i: (b, 0, 0, 0)),
                  pl.BlockSpec((1, C_GROUPS, ncp, LANES), lambda b, i: (b, 0, 0, 0)),
                  full(2 * LANES), full(2 * LANES),
                  pl.BlockSpec((ncp, LANES), lambda b, i: (0, 0)),
                  pl.BlockSpec((seq, LANES), lambda b, i: (0, 0))],
        out_specs=qblk(C_HEADS * HEAD_DIM),
        out_shape=jax.ShapeDtypeStruct((batch, seq, C_HEADS * HEAD_DIM), BF16),
        compiler_params=_cparams(("parallel", "arbitrary")),
        name="nsa",
    )(r3(p["cqn"]), r3(p["cqr"]), r3(p["misc"]), kcmp, vcmp, r3(p["cslc"]), r3(p["cwin"]), overlap, hot)


def _merge_body(x_ref, g_ref, oa_ref, ob_ref, oc_ref, wm_ref, wa_ref, wb_ref, wc_ref, wo_ref, o_ref):
    x = x_ref[...]
    d = x.shape[1]
    h = _rms(x, g_ref[...]).astype(BF16)
    merged = None
    for i, (o_r, w_r) in enumerate(((oa_ref, wa_ref), (ob_ref, wb_ref), (oc_ref, wc_ref))):
        gate = jax.nn.sigmoid(jnp.dot(h, wm_ref[:, d * i:d * (i + 1)], preferred_element_type=F32))
        y = gate * jnp.dot(o_r[...], w_r[...], preferred_element_type=F32)
        merged = y if merged is None else merged + y
    o_ref[...] = x + jnp.dot(merged.astype(BF16), wo_ref[...], preferred_element_type=F32)


def _merge(x2, g, oa, ob, oc, wm, wa, wb, wc, wo):
    m, d = x2.shape
    tm = MERGE_TM
    row = lambda c: pl.BlockSpec((tm, c), lambda i: (i, 0))
    const = lambda a: pl.BlockSpec(a.shape, lambda i: (0, 0))
    return pl.pallas_call(
        _merge_body,
        grid=(m // tm,),
        in_specs=[row(d), const(g), row(oa.shape[1]), row(ob.shape[1]), row(oc.shape[1]),
                  const(wm), const(wa), const(wb), const(wc), const(wo)],
        out_specs=row(d),
        out_shape=jax.ShapeDtypeStruct((m, d), F32),
        compiler_params=_cparams(("parallel",)),
        name="merge",
    )(x2, g, oa, ob, oc, wm, wa, wb, wc, wo)


def _rope_tables(seq):
    half = HEAD_DIM // 2
    inv = ROPE_THETA ** (-jnp.arange(half, dtype=F32) / half)
    ang = jnp.arange(seq, dtype=F32)[:, None] * inv[None, :]
    lane = np.arange(LANES)
    ang = ang[:, lane % half]
    second = jnp.asarray((lane % HEAD_DIM) >= half)[None, :]
    sin = jnp.sin(ang)
    return jnp.cos(ang), jnp.where(second, sin, 0.0), jnp.where(second, 0.0, -sin)


def _overlap(seq):
    ncp = seq // CMP_STRIDE
    nc = (seq - CMP_BLOCK) // CMP_STRIDE + 1
    ns = seq // SLC_BLOCK
    cs = np.arange(ncp) * CMP_STRIDE
    ss = np.arange(ns) * SLC_BLOCK
    ov = (cs[:, None] < ss[None, :] + SLC_BLOCK) & (ss[None, :] <= cs[:, None] + CMP_BLOCK - 1)
    ov &= (np.arange(ncp) < nc)[:, None]
    out = np.zeros((ncp, LANES), np.float32)
    out[:, :ns] = ov
    return jnp.asarray(out)


def _block_onehot(seq, block):
    return jnp.asarray(np.arange(seq)[:, None] // block == np.arange(LANES)[None, :], BF16)


def kernel(x, ffn1_norm, ffn1_w_gate, ffn1_w_up, ffn1_w_down, mix_norm, w_in, cmp_pos_k, cmp_w1_k, cmp_w2_k, cmp_pos_v, cmp_w1_v, cmp_w2_v, w_branch_a, w_branch_b, w_branch_c, w_out, ffn2_norm, ffn2_w_gate, ffn2_w_up, ffn2_w_down, final_norm):
    batch, seq, d = x.shape
    depth = w_in.shape[0]
    assert seq % max(KC, PROJ_TM) == 0 and WINDOW + TQ <= seq <= SLC_BLOCK * LANES and w_in.shape[2] == _N_IN
    nc = (seq - CMP_BLOCK) // CMP_STRIDE + 1
    ncp = seq // CMP_STRIDE

    perm = _in_perm()
    w_perm = jnp.where(jnp.asarray(perm >= 0)[None, None, :],
                       jnp.take(w_in, jnp.asarray(np.maximum(perm, 0)), axis=2), 0.0).astype(BF16)
    w_mg = w_in[:, :, _O_MG:].astype(BF16)
    half_rows = CMP_BLOCK * HEAD_DIM // 2
    w1 = jnp.stack([cmp_w1_k, cmp_w1_v], axis=1)
    w1 = jnp.concatenate([w1[:, :, :half_rows], w1[:, :, half_rows:]], axis=-1)
    w2 = jnp.pad(jnp.stack([cmp_w2_k, cmp_w2_v], axis=1), ((0, 0), (0, 0), (0, 0), (0, LANES - HEAD_DIM)))
    pos = jnp.stack([cmp_pos_k, cmp_pos_v], axis=1).reshape(depth, 2, 2, half_rows)
    pos = jnp.pad(pos, ((0, 0), (0, 0), (0, 6), (0, 0)))
    cos, sa, sb = _rope_tables(seq)
    overlap = _overlap(seq)
    hot_b = _block_onehot(seq, MOBA_BLOCK)
    hot_s = _block_onehot(seq, SLC_BLOCK)
    bf = lambda a: a.astype(BF16)
    row = lambda a: a.reshape(1, d)

    x2 = x.reshape(batch * seq, d)
    for l in range(depth):
        x2 = _ffn(x2, row(ffn1_norm[l]), bf(ffn1_w_gate[l]), bf(ffn1_w_up[l]), bf(ffn1_w_down[l]),
                  row(final_norm), False)
        p = _inproj(x2, row(mix_norm[l]), w_perm[l], cos, sa, sb, seq)
        xc = p["ccmp"].reshape(batch, seq, 4, HEAD_DIM).transpose(0, 2, 1, 3).reshape(batch, 4, ncp, 1024)
        kc, vc = _compress(xc, w1[l], w2[l], pos[l], nc)
        o_a = _dsa(p, batch, seq)
        o_b = _moba(p, hot_b, batch, seq)
        o_c = _nsa(p, kc[:, :C_GROUPS], vc[:, C_GROUPS:], overlap, hot_s, batch, seq)
        flat = lambda a: a.reshape(batch * seq, a.shape[-1])
        x2 = _merge(x2, row(mix_norm[l]), flat(o_a), flat(o_b), flat(o_c), w_mg[l],
                    bf(w_branch_a[l]), bf(w_branch_b[l]), bf(w_branch_c[l]), bf(w_out[l]))
        x2 = _ffn(x2, row(ffn2_norm[l]), bf(ffn2_w_gate[l]), bf(ffn2_w_up[l]), bf(ffn2_w_down[l]),
                  row(final_norm), l == depth - 1)
    return x2.reshape(batch, seq, d)
```

```python
import functools
import math

import numpy as np
import jax
import jax.numpy as jnp
from jax import lax
from jax.experimental import pallas as pl
from jax.experimental.pallas import tpu as pltpu

HEAD_DIM = 64
ROPE_THETA = 10000.0
NORM_EPS = 1e-6
A_HEADS = 4
IDX_HEADS = 4
IDX_DIM = 64
DSA_TOPK_MAX = 256
B_HEADS = 4
MOBA_BLOCK = 256
MOBA_TOPK = 3
C_HEADS = 8
C_GROUPS = 2
C_REP = C_HEADS // C_GROUPS
CMP_BLOCK = 32
CMP_STRIDE = 16
CMP_HIDDEN = 128
SLC_BLOCK = 64
SLC_TOPN = 16
WINDOW = 512
FORCE_SCORE = 1e30

LANES = 128
VMEM_LIMIT = 56 * 1024 * 1024

TQ = 128
KC = 512
FFN_TM = 512
PROJ_TM = 512
MERGE_TM = 512

NEG_BIG = -1e30
QSCALE = HEAD_DIM ** -0.5 * math.log2(math.e)
F32 = jnp.float32
BF16 = jnp.bfloat16
HIGHEST = lax.Precision.HIGHEST

_O_AQ = 0
_O_AK = 256
_O_AV = 320
_O_IQ = 384
_O_IK = 640
_O_IW = 704
_O_BQ = 708
_O_BK = 964
_O_BV = 1220
_O_CQ = 1476
_O_CKV = 1988
_O_CG = 2756
_O_MG = 2780
_N_IN = 5852

P_AQ = 0
P_AK = 256
P_IK = 320
P_IQ = 384
P_BQ = 640
P_BK = 896
P_CKS = 1152
P_CKW = 1280
P_CQ = 1408
P_ROPE_END = 1920
P_BV = 1920
P_KCMP = 2176
P_VS = 2432
P_VW = 2560
P_AV = 2688
P_MISC = 2752
P_COLS = 2816
MISC_CG = 0
MISC_IW = 24


def _in_perm():
    perm = -np.ones((P_COLS,), np.int64)

    def put(dst, src, n):
        perm[dst:dst + n] = np.arange(src, src + n)

    put(P_AQ, _O_AQ, 256)
    put(P_AK, _O_AK, 64)
    put(P_IK, _O_IK, 64)
    put(P_IQ, _O_IQ, 256)
    put(P_BQ, _O_BQ, 256)
    put(P_BK, _O_BK, 256)
    ckv = lambda s, g: _O_CKV + (s * C_GROUPS + g) * HEAD_DIM
    for g in range(C_GROUPS):
        put(P_CKS + 64 * g, ckv(2, g), 64)
        put(P_CKW + 64 * g, ckv(4, g), 64)
        put(P_KCMP + 64 * g, ckv(0, g), 64)
        put(P_KCMP + 128 + 64 * g, ckv(1, g), 64)
        put(P_VS + 64 * g, ckv(3, g), 64)
        put(P_VW + 64 * g, ckv(5, g), 64)
    put(P_CQ, _O_CQ, 512)
    put(P_BV, _O_BV, 256)
    put(P_AV, _O_AV, 64)
    put(P_MISC + MISC_CG, _O_CG, 24)
    put(P_MISC + MISC_IW, _O_IW, 4)
    return perm


def _cparams(sem):
    return pltpu.CompilerParams(dimension_semantics=sem, vmem_limit_bytes=VMEM_LIMIT)


def _rms(x, g):
    y = x * lax.rsqrt(jnp.mean(x * x, axis=-1, keepdims=True) + NORM_EPS)
    return y * g


def _dot_nt(a, b, precision=None):
    return lax.dot_general(a, b, (((1,), (1,)), ((), ())), precision=precision,
                           preferred_element_type=F32)


def _ffn_body(x_ref, g_ref, wg_ref, wu_ref, wd_ref, fg_ref, o_ref, h_scr, acc_scr, *, final_norm):
    f = pl.program_id(1)

    @pl.when(f == 0)
    def _():
        h_scr[...] = _rms(x_ref[...], g_ref[...]).astype(BF16)
        acc_scr[...] = jnp.zeros_like(acc_scr)

    h = h_scr[...]
    a = jnp.dot(h, wg_ref[...], preferred_element_type=F32)
    u = jnp.dot(h, wu_ref[...], preferred_element_type=F32)
    act = (a * jax.nn.sigmoid(a) * u).astype(BF16)
    acc_scr[...] += jnp.dot(act, wd_ref[...], preferred_element_type=F32)

    @pl.when(f == pl.num_programs(1) - 1)
    def _():
        y = x_ref[...] + 0.5 * acc_scr[...]
        if final_norm:
            y = _rms(y, fg_ref[...])
        o_ref[...] = y


def _ffn(x2, g, wg, wu, wd, fg, final_norm):
    m, d = x2.shape
    dff = wg.shape[1]
    tf = dff // 2 if (dff // 2) % LANES == 0 else dff
    tm = FFN_TM
    return pl.pallas_call(
        functools.partial(_ffn_body, final_norm=final_norm),
        grid=(m // tm, dff // tf),
        in_specs=[
            pl.BlockSpec((tm, d), lambda i, f: (i, 0)),
            pl.BlockSpec((1, d), lambda i, f: (0, 0)),
            pl.BlockSpec((d, tf), lambda i, f: (0, f)),
            pl.BlockSpec((d, tf), lambda i, f: (0, f)),
            pl.BlockSpec((tf, d), lambda i, f: (f, 0)),
            pl.BlockSpec((1, d), lambda i, f: (0, 0)),
        ],
        out_specs=pl.BlockSpec((tm, d), lambda i, f: (i, 0)),
        out_shape=jax.ShapeDtypeStruct((m, d), F32),
        scratch_shapes=[pltpu.VMEM((tm, d), BF16), pltpu.VMEM((tm, d), F32)],
        compiler_params=_cparams(("parallel", "arbitrary")),
        name="ffn",
    )(x2, g, wg, wu, wd, fg)


def _lane_iota(shape):
    return lax.broadcasted_iota(jnp.int32, shape, len(shape) - 1)


def _split_hi_lo(x):
    hi = x.astype(BF16).astype(F32)
    return hi, x - hi


def _inproj_body(x_ref, g_ref, w_ref, cos_ref, sa_ref, sb_ref,
                 aq_ref, akv_ref, iq_ref, ik_ref, misc_ref, bq_ref, bqp_ref, bkv_ref, km_ref,
                 cqn_ref, cqr_ref, cslc_ref, cwin_ref, ccmp_ref):
    tm = x_ref.shape[0]
    h = _rms(x_ref[...], g_ref[...]).astype(BF16)
    cos, sa, sb = cos_ref[...], sa_ref[...], sb_ref[...]
    lane = _lane_iota((tm, LANES))
    low = lane < HEAD_DIM

    raw = []
    rot = []
    for j in range(P_COLS // 256):
        z = jnp.dot(h, w_ref[:, 256 * j:256 * (j + 1)], preferred_element_type=F32)
        for half in range(2):
            p = z[:, LANES * half:LANES * (half + 1)]
            raw.append(p)
            if LANES * len(raw) <= P_ROPE_END:
                rot.append(p * cos + pltpu.roll(p, 32, axis=1) * sa + pltpu.roll(p, 96, axis=1) * sb)

    def pick(pieces, col):
        p = pieces[col // LANES]
        return pltpu.roll(p, HEAD_DIM, axis=1) if col % LANES else p

    def join(lo, hi=None):
        if hi is None:
            return jnp.where(low, lo, 0.0)
        return jnp.where(low, lo, pltpu.roll(hi, HEAD_DIM, axis=1))

    scale = QSCALE

    for hh in range(A_HEADS):
        aq_ref[:, LANES * hh:LANES * (hh + 1)] = join(pick(rot, P_AQ + 64 * hh) * scale).astype(BF16)
        qhi, qlo = _split_hi_lo(pick(rot, P_IQ + 64 * hh))
        iq_ref[:, 256 * hh:256 * hh + LANES] = join(qhi, qlo).astype(BF16)
        iq_ref[:, 256 * hh + LANES:256 * (hh + 1)] = join(qhi).astype(BF16)
    akv_ref[...] = join(pick(rot, P_AK), pick(raw, P_AV)).astype(BF16)
    khi, klo = _split_hi_lo(pick(rot, P_IK))
    ik_ref[:, :LANES] = join(khi, khi).astype(BF16)
    ik_ref[:, LANES:] = join(klo).astype(BF16)
    misc_ref[...] = pick(raw, P_MISC)

    bq_ref[:, :LANES] = rot[P_BQ // LANES]
    bq_ref[:, LANES:] = rot[P_BQ // LANES + 1]
    for hh in range(B_HEADS):
        bqp_ref[:, LANES * hh:LANES * (hh + 1)] = join(pick(rot, P_BQ + 64 * hh) * scale).astype(BF16)
        bkv_ref[:, LANES * hh:LANES * (hh + 1)] = join(pick(rot, P_BK + 64 * hh),
                                                      pick(raw, P_BV + 64 * hh)).astype(BF16)
    nblk = tm // MOBA_BLOCK
    km_ref[...] = jnp.zeros_like(km_ref)
    for half in range(2):
        kp = rot[P_BK // LANES + half]
        for b in range(nblk):
            km_ref[0, b:b + 1, LANES * half:LANES * (half + 1)] = jnp.mean(
                kp[MOBA_BLOCK * b:MOBA_BLOCK * (b + 1)], axis=0, keepdims=True)

    for hh in range(C_HEADS):
        nhi, nlo = _split_hi_lo(pick(raw, P_CQ + 64 * hh) * scale)
        cqn_ref[:, 256 * hh:256 * hh + LANES] = join(nhi, nlo).astype(BF16)
        cqn_ref[:, 256 * hh + LANES:256 * (hh + 1)] = join(nhi).astype(BF16)
        cqr_ref[:, LANES * hh:LANES * (hh + 1)] = join(pick(rot, P_CQ + 64 * hh) * scale).astype(BF16)
    for g in range(C_GROUPS):
        cslc_ref[:, LANES * g:LANES * (g + 1)] = join(pick(rot, P_CKS + 64 * g),
                                                     pick(raw, P_VS + 64 * g)).astype(BF16)
        cwin_ref[:, LANES * g:LANES * (g + 1)] = join(pick(rot, P_CKW + 64 * g),
                                                     pick(raw, P_VW + 64 * g)).astype(BF16)
    ccmp_ref[:, :LANES] = raw[P_KCMP // LANES]
    ccmp_ref[:, LANES:] = raw[P_KCMP // LANES + 1]


def _inproj(x2, g, w, cos, sa, sb, seq):
    m, d = x2.shape
    tm = PROJ_TM
    nt = seq // tm
    row = lambda c: pl.BlockSpec((tm, c), lambda i: (i, 0))
    tab = pl.BlockSpec((tm, LANES), lambda i: (i % nt, 0))
    outs = [
        ("aq", 4 * LANES, BF16), ("akv", LANES, BF16), ("iq", 4 * 256, BF16), ("ik", 256, BF16),
        ("misc", LANES, F32), ("bq", 256, F32), ("bqp", 4 * LANES, BF16), ("bkv", 4 * LANES, BF16),
        ("km", None, F32),
        ("cqn", 8 * 256, BF16), ("cqr", 8 * LANES, BF16), ("cslc", 2 * LANES, BF16),
        ("cwin", 2 * LANES, BF16), ("ccmp", 256, F32),
    ]
    out_specs, out_shape = [], []
    for name, c, dt in outs:
        if name == "km":
            out_specs.append(pl.BlockSpec((1, 8, 256), lambda i: (i, 0, 0)))
            out_shape.append(jax.ShapeDtypeStruct((m // tm, 8, 256), dt))
        else:
            out_specs.append(row(c))
            out_shape.append(jax.ShapeDtypeStruct((m, c), dt))
    res = pl.pallas_call(
        _inproj_body,
        grid=(m // tm,),
        in_specs=[row(d), pl.BlockSpec((1, d), lambda i: (0, 0)),
                  pl.BlockSpec((d, P_COLS), lambda i: (0, 0)), tab, tab, tab],
        out_specs=out_specs,
        out_shape=out_shape,
        compiler_params=_cparams(("parallel",)),
        name="inproj",
    )(x2, g, w, cos, sa, sb)
    return dict(zip([o[0] for o in outs], res))


def _compress_body(x_ref, w1_ref, w2_ref, pos_ref, ok_ref, ov_ref, *, n_valid):
    x = x_ref[0, 0]
    w1 = w1_ref[0]
    pre = jnp.dot(x, w1, precision=HIGHEST, preferred_element_type=F32)
    pp = jnp.dot(pos_ref[0], w1, precision=HIGHEST, preferred_element_type=F32)
    posb = pp[0:1, :CMP_HIDDEN] + pp[1:2, CMP_HIDDEN:]
    ncp = x.shape[0]
    upper = pre[:, :CMP_HIDDEN]
    lower_next = pltpu.roll(pre[:, CMP_HIDDEN:], ncp - 1, axis=0)
    hid = jax.nn.gelu(upper + lower_next + posb)
    out = jnp.dot(hid, w2_ref[0], precision=HIGHEST, preferred_element_type=F32)
    rows = lax.broadcasted_iota(jnp.int32, out.shape, 0)
    out = jnp.where(rows < n_valid, out, 0.0)
    low = _lane_iota(out.shape) < HEAD_DIM
    hi, lo = _split_hi_lo(out)
    ok_ref[0, 0, :, :LANES] = jnp.where(low, hi, pltpu.roll(hi, HEAD_DIM, axis=1)).astype(BF16)
    ok_ref[0, 0, :, LANES:] = lo.astype(BF16)
    ov_ref[0, 0] = pltpu.roll(out, HEAD_DIM, axis=1).astype(BF16)


def _compress(xc, w1, w2, pos, n_valid):
    b, four, ncp, _ = xc.shape
    return pl.pallas_call(
        functools.partial(_compress_body, n_valid=n_valid),
        grid=(b, four),
        in_specs=[
            pl.BlockSpec((1, 1, ncp, 1024), lambda i, j: (i, j, 0, 0)),
            pl.BlockSpec((1, 1024, 256), lambda i, j: (j // 2, 0, 0)),
            pl.BlockSpec((1, CMP_HIDDEN, LANES), lambda i, j: (j // 2, 0, 0)),
            pl.BlockSpec((1, 8, 1024), lambda i, j: (j // 2, 0, 0)),
        ],
        out_specs=[pl.BlockSpec((1, 1, ncp, 256), lambda i, j: (i, j, 0, 0)),
                   pl.BlockSpec((1, 1, ncp, LANES), lambda i, j: (i, j, 0, 0))],
        out_shape=[jax.ShapeDtypeStruct((b, four, ncp, 256), BF16),
                   jax.ShapeDtypeStruct((b, four, ncp, LANES), BF16)],
        compiler_params=_cparams(("parallel", "parallel")),
        name="compress",
    )(xc, w1, w2, pos)


def _softmax_step(carry, s3, pv):
    m_old, l_old, acc = carry
    heads = s3.shape[0]
    m_new = jnp.maximum(m_old, jnp.max(s3, axis=-1, keepdims=True))
    p = jnp.exp2(s3 - m_new)
    alpha = jnp.exp2(m_old - m_new)
    l_new = alpha * l_old + jnp.sum(p, axis=-1, keepdims=True)
    acc = alpha.reshape(heads * TQ, 1) * acc + pv(p.astype(BF16))
    return m_new, l_new, acc


def _online_init(heads):
    return (jnp.full((heads, TQ, 1), NEG_BIG, F32), jnp.zeros((heads, TQ, 1), F32),
            jnp.zeros((heads * TQ, LANES), F32))


def _online_finish(carry, heads):
    _, l, acc = carry
    return acc / l.reshape(heads * TQ, 1)


def _pv_shared(kv):
    return lambda p: jnp.dot(p.reshape(p.shape[0] * TQ, p.shape[2]), kv, preferred_element_type=F32)


def _store_heads(o_ref, o, heads, col0=0):
    low = _lane_iota((TQ, LANES)) < HEAD_DIM
    for pair in range(heads // 2):
        even = o[TQ * (2 * pair):TQ * (2 * pair + 1)]
        odd = o[TQ * (2 * pair + 1):TQ * (2 * pair + 2)]
        piece = jnp.where(low, pltpu.roll(even, HEAD_DIM, axis=1), odd)
        o_ref[0, :, col0 + LANES * pair:col0 + LANES * (pair + 1)] = piece.astype(o_ref.dtype)


def _topn_mask(score, n_pick, axis=-1):
    axis = axis % score.ndim
    idx = lax.broadcasted_iota(jnp.int32, score.shape, axis)
    width = score.shape[axis]
    sel = jnp.zeros(score.shape, F32)
    for _ in range(n_pick):
        m = jnp.max(score, axis=axis, keepdims=True)
        first = jnp.min(jnp.where(score == m, idx, width), axis=axis, keepdims=True)
        hit = idx == first
        sel = jnp.where(hit, 1.0, sel)
        score = jnp.where(hit, -jnp.inf, score)
    return sel


_NEG_INF_KEY = -2139095041
_INT_MIN = -2147483648


def _key_of(x):
    bits = lax.bitcast_convert_type(x, jnp.int32)
    return jnp.where(bits < 0, bits ^ jnp.int32(0x7FFFFFFF), bits)


def _dsa_body(iq_ref, ik_ref, misc_ref, aq_ref, akv_ref, o_ref, key_scr, *, topk, seq):
    c = pl.program_id(1)
    q0 = c * TQ
    nchunk = (q0 + TQ + KC - 1) // KC
    row = q0 + lax.broadcasted_iota(jnp.int32, (TQ, 1), 0)
    col = _lane_iota((1, KC))
    lane = _lane_iota((1, LANES))
    idx_scale = (IDX_HEADS * IDX_DIM) ** -0.5

    iq = jnp.concatenate([iq_ref[0, :, 256 * h:256 * (h + 1)] for h in range(IDX_HEADS)], axis=0)
    misc = misc_ref[0]
    iw = [misc[:, MISC_IW + h:MISC_IW + h + 1] for h in range(IDX_HEADS)]

    def score_body(j, _):
        ks = pl.multiple_of(j * KC, KC)
        lg = jnp.maximum(_dot_nt(iq, ik_ref[0, pl.ds(ks, KC), :]), 0.0)
        sc = iw[0] * lg[0:TQ]
        for h in range(1, IDX_HEADS):
            sc = sc + iw[h] * lg[TQ * h:TQ * (h + 1)]
        sc = sc * idx_scale
        sc = jnp.where(sc == 0.0, 0.0, sc)
        key_scr[j] = _key_of(jnp.where(ks + col <= row, sc, -jnp.inf))
        return 0

    lax.fori_loop(0, nchunk, score_body, 0)

    def count(pred):
        def body(j, acc):
            for g in range(KC // LANES):
                k = key_scr[j, :, LANES * g:LANES * (g + 1)]
                acc = acc + jnp.where(pred(k, j * KC + LANES * g), 1.0, 0.0)
            return acc
        acc = lax.fori_loop(0, nchunk, body, jnp.zeros((TQ, LANES), F32))
        return jnp.sum(acc, axis=-1, keepdims=True)

    def bit_body(i, t):
        cand = t + lax.shift_left(jnp.int32(1), 31 - i)
        cnt = count(lambda k, p0: k >= cand)
        return jnp.where(cnt >= topk, cand, t)

    thr = lax.fori_loop(0, 32, bit_body, jnp.full((TQ, 1), _INT_MIN, jnp.int32))

    n_gt = count(lambda k, p0: k > thr)
    n_ge = count(lambda k, p0: k >= thr)
    need = topk - n_gt
    finite = thr > _NEG_INF_KEY
    any_tie = jnp.max(jnp.where(finite, n_ge, 0.0)) > topk

    def tie_cut():
        def pos_body(i, cut):
            cand = cut + lax.shift_left(jnp.int32(1), (seq.bit_length() - 1) - i)
            cnt = count(lambda k, p0: (k == thr) & (p0 + lane < cand))
            return jnp.where(cnt < need, cand, cut)
        return lax.fori_loop(0, seq.bit_length(), pos_body, jnp.zeros((TQ, 1), jnp.int32))

    cut = lax.cond(any_tie, tie_cut, lambda: jnp.full((TQ, 1), seq, jnp.int32))
    cut = jnp.where(finite, cut, -1)

    q = jnp.concatenate([aq_ref[0, :, LANES * h:LANES * (h + 1)] for h in range(A_HEADS)], axis=0)

    def att_body(j, carry):
        kv = akv_ref[0, pl.ds(pl.multiple_of(j * KC, KC), KC), :]
        k = key_scr[j]
        taken = (k > thr) | ((k == thr) & (j * KC + col <= cut))
        s3 = _dot_nt(q, kv).reshape(A_HEADS, TQ, KC) + jnp.where(taken, 0.0, NEG_BIG)[None]
        return _softmax_step(carry, s3, _pv_shared(kv))

    carry = lax.fori_loop(0, nchunk, att_body, _online_init(A_HEADS))
    _store_heads(o_ref, _online_finish(carry, A_HEADS), A_HEADS)


def _dsa(p, batch, seq):
    topk = min(DSA_TOPK_MAX, seq // 4)
    r3 = lambda a: a.reshape(batch, seq, a.shape[-1])
    qblk = lambda c: pl.BlockSpec((1, TQ, c), lambda b, i: (b, i, 0))
    full = lambda c: pl.BlockSpec((1, seq, c), lambda b, i: (b, 0, 0))
    return pl.pallas_call(
        functools.partial(_dsa_body, topk=topk, seq=seq),
        grid=(batch, seq // TQ),
        in_specs=[qblk(4 * 256), full(256), qblk(LANES), qblk(4 * LANES), full(LANES)],
        out_specs=qblk(A_HEADS * HEAD_DIM),
        out_shape=jax.ShapeDtypeStruct((batch, seq, A_HEADS * HEAD_DIM), BF16),
        scratch_shapes=[pltpu.VMEM((seq // KC, TQ, KC), jnp.int32)],
        compiler_params=_cparams(("parallel", "arbitrary")),
        name="dsa",
    )(r3(p["iq"]), r3(p["ik"]), r3(p["misc"]), r3(p["aq"]), r3(p["akv"]))


def _pv_per_head(kvs):
    return lambda p: jnp.concatenate(
        [jnp.dot(p[h], kvs[h], preferred_element_type=F32) for h in range(len(kvs))], axis=0)


def _moba_body(bq_ref, bqp_ref, bkv_ref, km_ref, hot_ref, o_ref):
    c = pl.program_id(1)
    q0 = c * TQ
    own = q0 // MOBA_BLOCK
    row = q0 + lax.broadcasted_iota(jnp.int32, (TQ, 1), 0)
    past = _lane_iota((1, LANES)) < own
    col = _lane_iota((1, MOBA_BLOCK))
    ks_own = pl.multiple_of(own * MOBA_BLOCK, MOBA_BLOCK)
    heads = range(B_HEADS)
    kv_at = lambda ks, h: bkv_ref[0, pl.ds(ks, MOBA_BLOCK), LANES * h:LANES * (h + 1)]

    q_aug, s_own = [], []
    for h in heads:
        qf = bq_ref[0, :, HEAD_DIM * h:HEAD_DIM * (h + 1)]
        km = km_ref[0, :, HEAD_DIM * h:HEAD_DIM * (h + 1)]
        gate = jnp.where(past, _dot_nt(qf, km, precision=HIGHEST), -jnp.inf)
        picked = past & (_topn_mask(gate, MOBA_TOPK) > 0.0)
        q = bqp_ref[0, :, LANES * h:LANES * (h + 1)]
        q_aug.append(jnp.concatenate([q, jnp.where(picked, 0.0, NEG_BIG).astype(BF16)], axis=1))
        s_own.append(jnp.where(ks_own + col <= row, _dot_nt(q, kv_at(ks_own, h)), NEG_BIG))

    stack = lambda xs: jnp.concatenate(xs, axis=0).reshape(B_HEADS, TQ, MOBA_BLOCK)
    carry = _softmax_step(_online_init(B_HEADS), stack(s_own),
                          _pv_per_head([kv_at(ks_own, h) for h in heads]))

    def scores(j):
        ks = pl.multiple_of(j * MOBA_BLOCK, MOBA_BLOCK)
        hot = hot_ref[pl.ds(ks, MOBA_BLOCK), :]
        return stack([_dot_nt(q_aug[h], jnp.concatenate([kv_at(ks, h), hot], axis=1)) for h in heads])

    def body(j, carry):
        state, s_cur = carry
        s_next = scores(jnp.minimum(j + 1, own - 1))
        ks = pl.multiple_of(j * MOBA_BLOCK, MOBA_BLOCK)
        return _softmax_step(state, s_cur, _pv_per_head([kv_at(ks, h) for h in heads])), s_next

    carry, _ = lax.fori_loop(0, own, body, (carry, scores(0)))
    _store_heads(o_ref, _online_finish(carry, B_HEADS), B_HEADS)


def _moba(p, hot, batch, seq):
    nb = seq // MOBA_BLOCK
    r3 = lambda a: a.reshape(batch, seq, a.shape[-1])
    km = p["km"][:, :PROJ_TM // MOBA_BLOCK].reshape(batch, nb, 256)
    km = jnp.pad(km, ((0, 0), (0, LANES - nb), (0, 0)))
    qblk = lambda c: pl.BlockSpec((1, TQ, c), lambda b, i: (b, i, 0))
    return pl.pallas_call(
        _moba_body,
        grid=(batch, seq // TQ),
        in_specs=[qblk(256), qblk(4 * LANES),
                  pl.BlockSpec((1, seq, 4 * LANES), lambda b, i: (b, 0, 0)),
                  pl.BlockSpec((1, LANES, 256), lambda b, i: (b, 0, 0)),
                  pl.BlockSpec((seq, LANES), lambda b, i: (0, 0))],
        out_specs=qblk(B_HEADS * HEAD_DIM),
        out_shape=jax.ShapeDtypeStruct((batch, seq, B_HEADS * HEAD_DIM), BF16),
        compiler_params=_cparams(("parallel", "arbitrary")),
        name="moba",
    )(r3(p["bq"]), r3(p["bqp"]), r3(p["bkv"]), km, hot)


def _nsa_body(cqn_ref, cqr_ref, misc_ref, kc_ref, vc_ref, cslc_ref, cwin_ref, ov_ref, hot_ref, o_ref,
              *, ncp, n_sel):
    c = pl.program_id(1)
    q0 = c * TQ
    last = (q0 + TQ - 1) // KC
    ks_last = pl.multiple_of(last * KC, KC)
    row = q0 + lax.broadcasted_iota(jnp.int32, (TQ, 1), 0)
    col = _lane_iota((1, KC))
    gates = jax.nn.sigmoid(misc_ref[0])
    cmp_vis = _lane_iota((1, ncp)) * CMP_STRIDE + (CMP_BLOCK - 1) <= row
    wstart = pl.multiple_of(jnp.maximum(q0 - WINDOW, 0), TQ)
    wlen = WINDOW + TQ
    wdiff = row - (wstart + _lane_iota((1, wlen)))
    wmask = (wdiff >= 0) & (wdiff < WINDOW)

    o_cmp, imp_t = [], []
    for g in range(C_GROUPS):
        qn = jnp.concatenate([cqn_ref[0, :, 256 * (C_REP * g + r):256 * (C_REP * g + r + 1)]
                              for r in range(C_REP)], axis=0)
        s3 = _dot_nt(qn, kc_ref[0, g]).reshape(C_REP, TQ, ncp)
        m = jnp.max(jnp.where(cmp_vis[None], s3, -jnp.inf), axis=-1, keepdims=True)
        m = jnp.where(m > -jnp.inf, m, 0.0)
        pc = jnp.where(cmp_vis[None], jnp.exp2(s3 - m), 0.0)
        den = jnp.sum(pc, axis=-1, keepdims=True)
        pc = pc / jnp.where(den > 0, den, 1.0)
        o_cmp.append(jnp.dot(pc.reshape(C_REP * TQ, ncp).astype(BF16), vc_ref[0, g],
                             preferred_element_type=F32))
        psum = pc[0]
        for r in range(1, C_REP):
            psum = psum + pc[r]
        imp = jnp.dot(psum, ov_ref[...], precision=HIGHEST, preferred_element_type=F32)
        imp_t.append(imp.T)

    blk = lax.broadcasted_iota(jnp.int32, (LANES, 1), 0)
    jq = (q0 + _lane_iota((1, C_GROUPS * TQ)) % TQ) // SLC_BLOCK
    adm = blk <= jq
    forced = adm & ((blk == 0) | (blk == jq) | (blk == jq - 1))
    score = jnp.where(forced, FORCE_SCORE, jnp.where(adm, jnp.concatenate(imp_t, axis=1), -jnp.inf))
    picked = adm & (_topn_mask(score, n_sel, axis=0) > 0.0)
    bias_t = jnp.where(picked, 0.0, NEG_BIG)

    for g in range(C_GROUPS):
        o_c = o_cmp[g]
        bias = bias_t[:, TQ * g:TQ * (g + 1)].T.astype(BF16)

        qr = jnp.concatenate([cqr_ref[0, :, LANES * (C_REP * g + r):LANES * (C_REP * g + r + 1)]
                              for r in range(C_REP)], axis=0)
        qa = jnp.concatenate([qr, jnp.concatenate([bias] * C_REP, axis=0)], axis=1)
        kv_at = lambda j: cslc_ref[0, pl.ds(pl.multiple_of(j * KC, KC), KC), LANES * g:LANES * (g + 1)]
        slc_scores = lambda j: _dot_nt(qa, jnp.concatenate(
            [kv_at(j), hot_ref[pl.ds(pl.multiple_of(j * KC, KC), KC), :]], axis=1))

        def slc_body(j, carry):
            return _softmax_step(carry, slc_scores(j).reshape(C_REP, TQ, KC), _pv_shared(kv_at(j)))

        carry = lax.fori_loop(0, last, slc_body, _online_init(C_REP))
        s3 = jnp.where((ks_last + col <= row)[None], slc_scores(last).reshape(C_REP, TQ, KC), NEG_BIG)
        o_s = _online_finish(_softmax_step(carry, s3, _pv_shared(kv_at(last))), C_REP)

        kvw = cwin_ref[0, pl.ds(wstart, wlen), LANES * g:LANES * (g + 1)]
        s3 = jnp.where(wmask[None], _dot_nt(qr, kvw).reshape(C_REP, TQ, wlen), NEG_BIG)
        o_w = _online_finish(_softmax_step(_online_init(C_REP), s3, _pv_shared(kvw)), C_REP)

        outs = []
        for r in range(C_REP):
            hh = C_REP * g + r
            gcol = lambda j: gates[:, MISC_CG + 3 * hh + j:MISC_CG + 3 * hh + j + 1]
            rows = slice(TQ * r, TQ * (r + 1))
            outs.append(gcol(0) * o_c[rows] + gcol(1) * o_s[rows] + gcol(2) * o_w[rows])
        _store_heads(o_ref, jnp.concatenate(outs, axis=0), C_REP, col0=C_REP * HEAD_DIM * g)


def _nsa(p, kcmp, vcmp, overlap, hot, batch, seq):
    ncp = seq // CMP_STRIDE
    n_sel = min(SLC_TOPN, seq // SLC_BLOCK)
    r3 = lambda a: a.reshape(batch, seq, a.shape[-1])
    qblk = lambda c: pl.BlockSpec((1, TQ, c), lambda b, i: (b, i, 0))
    full = lambda c: pl.BlockSpec((1, seq, c), lambda b, i: (b, 0, 0))
    return pl.pallas_call(
        functools.partial(_nsa_body, ncp=ncp, n_sel=n_sel),
        grid=(batch, seq // TQ),
        in_specs=[qblk(8 * 256), qblk(8 * LANES), qblk(LANES),
                  pl.BlockSpec((1, C_GROUPS, ncp, 256), lambda b, i: (b, 0, 0, 0)),
                  pl.BlockSpec((1, C_GROUPS, ncp, LANES), lambda b, i: (b, 0, 0, 0)),
                  full(2 * LANES), full(2 * LANES),
                  pl.BlockSpec((ncp, LANES), lambda b, i: (0, 0)),
                  pl.BlockSpec((seq, LANES), lambda b, i: (0, 0))],
        out_specs=qblk(C_HEADS * HEAD_DIM),
        out_shape=jax.ShapeDtypeStruct((batch, seq, C_HEADS * HEAD_DIM), BF16),
        compiler_params=_cparams(("parallel", "arbitrary")),
        name="nsa",
    )(r3(p["cqn"]), r3(p["cqr"]), r3(p["misc"]), kcmp, vcmp, r3(p["cslc"]), r3(p["cwin"]), overlap, hot)


def _merge_body(x_ref, g_ref, oa_ref, ob_ref, oc_ref, wm_ref, wa_ref, wb_ref, wc_ref, wo_ref, o_ref):
    x = x_ref[...]
    d = x.shape[1]
    h = _rms(x, g_ref[...]).astype(BF16)
    merged = None
    for i, (o_r, w_r) in enumerate(((oa_ref, wa_ref), (ob_ref, wb_ref), (oc_ref, wc_ref))):
        gate = jax.nn.sigmoid(jnp.dot(h, wm_ref[:, d * i:d * (i + 1)], preferred_element_type=F32))
        y = gate * jnp.dot(o_r[...], w_r[...], preferred_element_type=F32)
        merged = y if merged is None else merged + y
    o_ref[...] = x + jnp.dot(merged.astype(BF16), wo_ref[...], preferred_element_type=F32)


def _merge(x2, g, oa, ob, oc, wm, wa, wb, wc, wo):
    m, d = x2.shape
    tm = MERGE_TM
    row = lambda c: pl.BlockSpec((tm, c), lambda i: (i, 0))
    const = lambda a: pl.BlockSpec(a.shape, lambda i: (0, 0))
    return pl.pallas_call(
        _merge_body,
        grid=(m // tm,),
        in_specs=[row(d), const(g), row(oa.shape[1]), row(ob.shape[1]), row(oc.shape[1]),
                  const(wm), const(wa), const(wb), const(wc), const(wo)],
        out_specs=row(d),
        out_shape=jax.ShapeDtypeStruct((m, d), F32),
        compiler_params=_cparams(("parallel",)),
        name="merge",
    )(x2, g, oa, ob, oc, wm, wa, wb, wc, wo)


def _rope_tables(seq):
    half = HEAD_DIM // 2
    inv = ROPE_THETA ** (-jnp.arange(half, dtype=F32) / half)
    ang = jnp.arange(seq, dtype=F32)[:, None] * inv[None, :]
    lane = np.arange(LANES)
    ang = ang[:, lane % half]
    second = jnp.asarray((lane % HEAD_DIM) >= half)[None, :]
    sin = jnp.sin(ang)
    return jnp.cos(ang), jnp.where(second, sin, 0.0), jnp.where(second, 0.0, -sin)


def _overlap(seq):
    ncp = seq // CMP_STRIDE
    nc = (seq - CMP_BLOCK) // CMP_STRIDE + 1
    ns = seq // SLC_BLOCK
    cs = np.arange(ncp) * CMP_STRIDE
    ss = np.arange(ns) * SLC_BLOCK
    ov = (cs[:, None] < ss[None, :] + SLC_BLOCK) & (ss[None, :] <= cs[:, None] + CMP_BLOCK - 1)
    ov &= (np.arange(ncp) < nc)[:, None]
    out = np.zeros((ncp, LANES), np.float32)
    out[:, :ns] = ov
    return jnp.asarray(out)


def _block_onehot(seq, block):
    return jnp.asarray(np.arange(seq)[:, None] // block == np.arange(LANES)[None, :], BF16)


def kernel(x, ffn1_norm, ffn1_w_gate, ffn1_w_up, ffn1_w_down, mix_norm, w_in, cmp_pos_k, cmp_w1_k, cmp_w2_k, cmp_pos_v, cmp_w1_v, cmp_w2_v, w_branch_a, w_branch_b, w_branch_c, w_out, ffn2_norm, ffn2_w_gate, ffn2_w_up, ffn2_w_down, final_norm):
    batch, seq, d = x.shape
    depth = w_in.shape[0]
    assert seq % max(KC, PROJ_TM) == 0 and WINDOW + TQ <= seq <= SLC_BLOCK * LANES and w_in.shape[2] == _N_IN
    nc = (seq - CMP_BLOCK) // CMP_STRIDE + 1
    ncp = seq // CMP_STRIDE

    perm = _in_perm()
    w_perm = jnp.where(jnp.asarray(perm >= 0)[None, None, :],
                       jnp.take(w_in, jnp.asarray(np.maximum(perm, 0)), axis=2), 0.0).astype(BF16)
    w_mg = w_in[:, :, _O_MG:].astype(BF16)
    half_rows = CMP_BLOCK * HEAD_DIM // 2
    w1 = jnp.stack([cmp_w1_k, cmp_w1_v], axis=1)
    w1 = jnp.concatenate([w1[:, :, :half_rows], w1[:, :, half_rows:]], axis=-1)
    w2 = jnp.pad(jnp.stack([cmp_w2_k, cmp_w2_v], axis=1), ((0, 0), (0, 0), (0, 0), (0, LANES - HEAD_DIM)))
    pos = jnp.stack([cmp_pos_k, cmp_pos_v], axis=1).reshape(depth, 2, 2, half_rows)
    pos = jnp.pad(pos, ((0, 0), (0, 0), (0, 6), (0, 0)))
    cos, sa, sb = _rope_tables(seq)
    overlap = _overlap(seq)
    hot_b = _block_onehot(seq, MOBA_BLOCK)
    hot_s = _block_onehot(seq, SLC_BLOCK)
    bf = lambda a: a.astype(BF16)
    row = lambda a: a.reshape(1, d)

    x2 = x.reshape(batch * seq, d)
    for l in range(depth):
        x2 = _ffn(x2, row(ffn1_norm[l]), bf(ffn1_w_gate[l]), bf(ffn1_w_up[l]), bf(ffn1_w_down[l]),
                  row(final_norm), False)
        p = _inproj(x2, row(mix_norm[l]), w_perm[l], cos, sa, sb, seq)
        xc = p["ccmp"].reshape(batch, seq, 4, HEAD_DIM).transpose(0, 2, 1, 3).reshape(batch, 4, ncp, 1024)
        kc, vc = _compress(xc, w1[l], w2[l], pos[l], nc)
        o_a = _dsa(p, batch, seq)
        o_b = _moba(p, hot_b, batch, seq)
        o_c = _nsa(p, kc[:, :C_GROUPS], vc[:, C_GROUPS:], overlap, hot_s, batch, seq)
        flat = lambda a: a.reshape(batch * seq, a.shape[-1])
        x2 = _merge(x2, row(mix_norm[l]), flat(o_a), flat(o_b), flat(o_c), w_mg[l],
                    bf(w_branch_a[l]), bf(w_branch_b[l]), bf(w_branch_c[l]), bf(w_out[l]))
        x2 = _ffn(x2, row(ffn2_norm[l]), bf(ffn2_w_gate[l]), bf(ffn2_w_up[l]), bf(ffn2_w_down[l]),
                  row(final_norm), l == depth - 1)
    return x2.reshape(batch, seq, d)
```

```python
import functools
import math

import numpy as np
import jax
import jax.numpy as jnp
from jax import lax
from jax.experimental import pallas as pl
from jax.experimental.pallas import tpu as pltpu

HEAD_DIM = 64
ROPE_THETA = 10000.0
NORM_EPS = 1e-6
A_HEADS = 4
IDX_HEADS = 4
IDX_DIM = 64
DSA_TOPK_MAX = 256
B_HEADS = 4
MOBA_BLOCK = 256
MOBA_TOPK = 3
C_HEADS = 8
C_GROUPS = 2
C_REP = C_HEADS // C_GROUPS
CMP_BLOCK = 32
CMP_STRIDE = 16
CMP_HIDDEN = 128
SLC_BLOCK = 64
SLC_TOPN = 16
WINDOW = 512
FORCE_SCORE = 1e30

LANES = 128
VMEM_LIMIT = 56 * 1024 * 1024

DSA_TQ = 128
MOBA_TQ = 256
NSA_TQ = 256
KC = 512
COUNT_ROWS = 128
FFN_TM = 512
PROJ_TM = 512
MERGE_TM = 512

NEG_BIG = -1e30
QSCALE = HEAD_DIM ** -0.5 * math.log2(math.e)
F32 = jnp.float32
BF16 = jnp.bfloat16
HIGHEST = lax.Precision.HIGHEST

_O_AQ = 0
_O_AK = 256
_O_AV = 320
_O_IQ = 384
_O_IK = 640
_O_IW = 704
_O_BQ = 708
_O_BK = 964
_O_BV = 1220
_O_CQ = 1476
_O_CKV = 1988
_O_CG = 2756
_O_MG = 2780
_N_IN = 5852

P_AQ = 0
P_AK = 256
P_IK = 320
P_IQ = 384
P_BQ = 640
P_BK = 896
P_CKS = 1152
P_CKW = 1280
P_CQ = 1408
P_ROPE_END = 1920
P_BV = 1920
P_KCMP = 2176
P_VS = 2432
P_VW = 2560
P_AV = 2688
P_MISC = 2752
P_COLS = 2816
MISC_CG = 0
MISC_IW = 24


def _in_perm():
    perm = -np.ones((P_COLS,), np.int64)

    def put(dst, src, n):
        perm[dst:dst + n] = np.arange(src, src + n)

    put(P_AQ, _O_AQ, 256)
    put(P_AK, _O_AK, 64)
    put(P_IK, _O_IK, 64)
    put(P_IQ, _O_IQ, 256)
    put(P_BQ, _O_BQ, 256)
    put(P_BK, _O_BK, 256)
    ckv = lambda s, g: _O_CKV + (s * C_GROUPS + g) * HEAD_DIM
    for g in range(C_GROUPS):
        put(P_CKS + 64 * g, ckv(2, g), 64)
        put(P_CKW + 64 * g, ckv(4, g), 64)
        put(P_KCMP + 64 * g, ckv(0, g), 64)
        put(P_KCMP + 128 + 64 * g, ckv(1, g), 64)
        put(P_VS + 64 * g, ckv(3, g), 64)
        put(P_VW + 64 * g, ckv(5, g), 64)
    put(P_CQ, _O_CQ, 512)
    put(P_BV, _O_BV, 256)
    put(P_AV, _O_AV, 64)
    put(P_MISC + MISC_CG, _O_CG, 24)
    put(P_MISC + MISC_IW, _O_IW, 4)
    return perm


def _cparams(sem):
    return pltpu.CompilerParams(dimension_semantics=sem, vmem_limit_bytes=VMEM_LIMIT)


def _rms(x, g):
    y = x * lax.rsqrt(jnp.mean(x * x, axis=-1, keepdims=True) + NORM_EPS)
    return y * g


def _dot_nt(a, b, precision=None):
    return lax.dot_general(a, b, (((1,), (1,)), ((), ())), precision=precision,
                           preferred_element_type=F32)


def _ffn_body(x_ref, g_ref, wg_ref, wu_ref, wd_ref, fg_ref, o_ref, h_scr, acc_scr, *, final_norm):
    f = pl.program_id(1)

    @pl.when(f == 0)
    def _():
        h_scr[...] = _rms(x_ref[...], g_ref[...]).astype(BF16)
        acc_scr[...] = jnp.zeros_like(acc_scr)

    h = h_scr[...]
    a = jnp.dot(h, wg_ref[...], preferred_element_type=F32)
    u = jnp.dot(h, wu_ref[...], preferred_element_type=F32)
    act = (a * jax.nn.sigmoid(a) * u).astype(BF16)
    acc_scr[...] += jnp.dot(act, wd_ref[...], preferred_element_type=F32)

    @pl.when(f == pl.num_programs(1) - 1)
    def _():
        y = x_ref[...] + 0.5 * acc_scr[...]
        if final_norm:
            y = _rms(y, fg_ref[...])
        o_ref[...] = y


def _ffn(x2, g, wg, wu, wd, fg, final_norm):
    m, d = x2.shape
    dff = wg.shape[1]
    tf = dff // 2 if (dff // 2) % LANES == 0 else dff
    tm = FFN_TM
    return pl.pallas_call(
        functools.partial(_ffn_body, final_norm=final_norm),
        grid=(m // tm, dff // tf),
        in_specs=[
            pl.BlockSpec((tm, d), lambda i, f: (i, 0)),
            pl.BlockSpec((1, d), lambda i, f: (0, 0)),
            pl.BlockSpec((d, tf), lambda i, f: (0, f)),
            pl.BlockSpec((d, tf), lambda i, f: (0, f)),
            pl.BlockSpec((tf, d), lambda i, f: (f, 0)),
            pl.BlockSpec((1, d), lambda i, f: (0, 0)),
        ],
        out_specs=pl.BlockSpec((tm, d), lambda i, f: (i, 0)),
        out_shape=jax.ShapeDtypeStruct((m, d), F32),
        scratch_shapes=[pltpu.VMEM((tm, d), BF16), pltpu.VMEM((tm, d), F32)],
        compiler_params=_cparams(("parallel", "arbitrary")),
        name="ffn",
    )(x2, g, wg, wu, wd, fg)


def _lane_iota(shape):
    return lax.broadcasted_iota(jnp.int32, shape, len(shape) - 1)


def _split_hi_lo(x):
    hi = x.astype(BF16).astype(F32)
    return hi, x - hi


def _inproj_body(x_ref, g_ref, w_ref, cos_ref, sa_ref, sb_ref,
                 aq_ref, akv_ref, iq_ref, ik_ref, misc_ref, bq_ref, bqp_ref, bkv_ref, km_ref,
                 cqn_ref, cqr_ref, cslc_ref, cwin_ref, ccmp_ref):
    tm = x_ref.shape[0]
    h = _rms(x_ref[...], g_ref[...]).astype(BF16)
    cos, sa, sb = cos_ref[...], sa_ref[...], sb_ref[...]
    lane = _lane_iota((tm, LANES))
    low = lane < HEAD_DIM

    raw = []
    rot = []
    for j in range(P_COLS // 256):
        z = jnp.dot(h, w_ref[:, 256 * j:256 * (j + 1)], preferred_element_type=F32)
        for half in range(2):
            p = z[:, LANES * half:LANES * (half + 1)]
            raw.append(p)
            if LANES * len(raw) <= P_ROPE_END:
                rot.append(p * cos + pltpu.roll(p, 32, axis=1) * sa + pltpu.roll(p, 96, axis=1) * sb)

    def pick(pieces, col):
        p = pieces[col // LANES]
        return pltpu.roll(p, HEAD_DIM, axis=1) if col % LANES else p

    def join(lo, hi=None):
        if hi is None:
            return jnp.where(low, lo, 0.0)
        return jnp.where(low, lo, pltpu.roll(hi, HEAD_DIM, axis=1))

    scale = QSCALE

    for hh in range(A_HEADS):
        aq_ref[:, LANES * hh:LANES * (hh + 1)] = join(pick(rot, P_AQ + 64 * hh) * scale).astype(BF16)
        qhi, qlo = _split_hi_lo(pick(rot, P_IQ + 64 * hh))
        iq_ref[:, 256 * hh:256 * hh + LANES] = join(qhi, qlo).astype(BF16)
        iq_ref[:, 256 * hh + LANES:256 * (hh + 1)] = join(qhi).astype(BF16)
    akv_ref[...] = join(pick(rot, P_AK), pick(raw, P_AV)).astype(BF16)
    khi, klo = _split_hi_lo(pick(rot, P_IK))
    ik_ref[:, :LANES] = join(khi, khi).astype(BF16)
    ik_ref[:, LANES:] = join(klo).astype(BF16)
    misc_ref[...] = pick(raw, P_MISC)

    bq_ref[:, :LANES] = rot[P_BQ // LANES]
    bq_ref[:, LANES:] = rot[P_BQ // LANES + 1]
    for hh in range(B_HEADS):
        bqp_ref[:, LANES * hh:LANES * (hh + 1)] = join(pick(rot, P_BQ + 64 * hh) * scale).astype(BF16)
        bkv_ref[:, LANES * hh:LANES * (hh + 1)] = join(pick(rot, P_BK + 64 * hh),
                                                      pick(raw, P_BV + 64 * hh)).astype(BF16)
    nblk = tm // MOBA_BLOCK
    km_ref[...] = jnp.zeros_like(km_ref)
    for half in range(2):
        kp = rot[P_BK // LANES + half]
        for b in range(nblk):
            km_ref[0, b:b + 1, LANES * half:LANES * (half + 1)] = jnp.mean(
                kp[MOBA_BLOCK * b:MOBA_BLOCK * (b + 1)], axis=0, keepdims=True)

    for hh in range(C_HEADS):
        nhi, nlo = _split_hi_lo(pick(raw, P_CQ + 64 * hh) * scale)
        cqn_ref[:, 256 * hh:256 * hh + LANES] = join(nhi, nlo).astype(BF16)
        cqn_ref[:, 256 * hh + LANES:256 * (hh + 1)] = join(nhi).astype(BF16)
        cqr_ref[:, LANES * hh:LANES * (hh + 1)] = join(pick(rot, P_CQ + 64 * hh) * scale).astype(BF16)
    for g in range(C_GROUPS):
        cslc_ref[:, LANES * g:LANES * (g + 1)] = join(pick(rot, P_CKS + 64 * g),
                                                     pick(raw, P_VS + 64 * g)).astype(BF16)
        cwin_ref[:, LANES * g:LANES * (g + 1)] = join(pick(rot, P_CKW + 64 * g),
                                                     pick(raw, P_VW + 64 * g)).astype(BF16)
    ccmp_ref[:, :LANES] = raw[P_KCMP // LANES]
    ccmp_ref[:, LANES:] = raw[P_KCMP // LANES + 1]


def _inproj(x2, g, w, cos, sa, sb, seq):
    m, d = x2.shape
    tm = PROJ_TM
    nt = seq // tm
    row = lambda c: pl.BlockSpec((tm, c), lambda i: (i, 0))
    tab = pl.BlockSpec((tm, LANES), lambda i: (i % nt, 0))
    outs = [
        ("aq", 4 * LANES, BF16), ("akv", LANES, BF16), ("iq", 4 * 256, BF16), ("ik", 256, BF16),
        ("misc", LANES, F32), ("bq", 256, F32), ("bqp", 4 * LANES, BF16), ("bkv", 4 * LANES, BF16),
        ("km", None, F32),
        ("cqn", 8 * 256, BF16), ("cqr", 8 * LANES, BF16), ("cslc", 2 * LANES, BF16),
        ("cwin", 2 * LANES, BF16), ("ccmp", 256, F32),
    ]
    out_specs, out_shape = [], []
    for name, c, dt in outs:
        if name == "km":
            out_specs.append(pl.BlockSpec((1, 8, 256), lambda i: (i, 0, 0)))
            out_shape.append(jax.ShapeDtypeStruct((m // tm, 8, 256), dt))
        else:
            out_specs.append(row(c))
            out_shape.append(jax.ShapeDtypeStruct((m, c), dt))
    res = pl.pallas_call(
        _inproj_body,
        grid=(m // tm,),
        in_specs=[row(d), pl.BlockSpec((1, d), lambda i: (0, 0)),
                  pl.BlockSpec((d, P_COLS), lambda i: (0, 0)), tab, tab, tab],
        out_specs=out_specs,
        out_shape=out_shape,
        compiler_params=_cparams(("parallel",)),
        name="inproj",
    )(x2, g, w, cos, sa, sb)
    return dict(zip([o[0] for o in outs], res))


def _compress_body(x_ref, w1_ref, w2_ref, pos_ref, ok_ref, ov_ref, *, n_valid):
    x = x_ref[0, 0]
    w1 = w1_ref[0]
    pre = jnp.dot(x, w1, precision=HIGHEST, preferred_element_type=F32)
    pp = jnp.dot(pos_ref[0], w1, precision=HIGHEST, preferred_element_type=F32)
    posb = pp[0:1, :CMP_HIDDEN] + pp[1:2, CMP_HIDDEN:]
    ncp = x.shape[0]
    upper = pre[:, :CMP_HIDDEN]
    lower_next = pltpu.roll(pre[:, CMP_HIDDEN:], ncp - 1, axis=0)
    hid = jax.nn.gelu(upper + lower_next + posb)
    out = jnp.dot(hid, w2_ref[0], precision=HIGHEST, preferred_element_type=F32)
    rows = lax.broadcasted_iota(jnp.int32, out.shape, 0)
    out = jnp.where(rows < n_valid, out, 0.0)
    low = _lane_iota(out.shape) < HEAD_DIM
    hi, lo = _split_hi_lo(out)
    ok_ref[0, 0, :, :LANES] = jnp.where(low, hi, pltpu.roll(hi, HEAD_DIM, axis=1)).astype(BF16)
    ok_ref[0, 0, :, LANES:] = lo.astype(BF16)
    ov_ref[0, 0] = pltpu.roll(out, HEAD_DIM, axis=1).astype(BF16)


def _compress(xc, w1, w2, pos, n_valid):
    b, four, ncp, _ = xc.shape
    return pl.pallas_call(
        functools.partial(_compress_body, n_valid=n_valid),
        grid=(b, four),
        in_specs=[
            pl.BlockSpec((1, 1, ncp, 1024), lambda i, j: (i, j, 0, 0)),
            pl.BlockSpec((1, 1024, 256), lambda i, j: (j // 2, 0, 0)),
            pl.BlockSpec((1, CMP_HIDDEN, LANES), lambda i, j: (j // 2, 0, 0)),
            pl.BlockSpec((1, 8, 1024), lambda i, j: (j // 2, 0, 0)),
        ],
        out_specs=[pl.BlockSpec((1, 1, ncp, 256), lambda i, j: (i, j, 0, 0)),
                   pl.BlockSpec((1, 1, ncp, LANES), lambda i, j: (i, j, 0, 0))],
        out_shape=[jax.ShapeDtypeStruct((b, four, ncp, 256), BF16),
                   jax.ShapeDtypeStruct((b, four, ncp, LANES), BF16)],
        compiler_params=_cparams(("parallel", "parallel")),
        name="compress",
    )(xc, w1, w2, pos)


def _softmax_step(carry, s3, pv):
    m_old, l_old, acc = carry
    heads, tq = s3.shape[0], s3.shape[1]
    m_new = jnp.maximum(m_old, jnp.max(s3, axis=-1, keepdims=True))
    p = jnp.exp2(s3 - m_new)
    alpha = jnp.exp2(m_old - m_new)
    l_new = alpha * l_old + jnp.sum(p, axis=-1, keepdims=True)
    acc = alpha.reshape(heads * tq, 1) * acc + pv(p.astype(BF16))
    return m_new, l_new, acc


def _online_init(heads, tq):
    return (jnp.full((heads, tq, 1), NEG_BIG, F32), jnp.zeros((heads, tq, 1), F32),
            jnp.zeros((heads * tq, LANES), F32))


def _online_finish(carry):
    _, l, acc = carry
    return acc / l.reshape(acc.shape[0], 1)


def _pv_shared(kv):
    return lambda p: jnp.dot(p.reshape(p.shape[0] * p.shape[1], p.shape[2]), kv, preferred_element_type=F32)


def _store_heads(o_ref, o, heads, col0=0):
    tq = o.shape[0] // heads
    low = _lane_iota((tq, LANES)) < HEAD_DIM
    for pair in range(heads // 2):
        even = o[tq * (2 * pair):tq * (2 * pair + 1)]
        odd = o[tq * (2 * pair + 1):tq * (2 * pair + 2)]
        piece = jnp.where(low, pltpu.roll(even, HEAD_DIM, axis=1), odd)
        o_ref[0, :, col0 + LANES * pair:col0 + LANES * (pair + 1)] = piece.astype(o_ref.dtype)


def _topn_mask(score, n_pick, axis=-1):
    axis = axis % score.ndim
    idx = lax.broadcasted_iota(jnp.int32, score.shape, axis)
    width = score.shape[axis]
    sel = jnp.zeros(score.shape, F32)
    for _ in range(n_pick):
        m = jnp.max(score, axis=axis, keepdims=True)
        first = jnp.min(jnp.where(score == m, idx, width), axis=axis, keepdims=True)
        hit = idx == first
        sel = jnp.where(hit, 1.0, sel)
        score = jnp.where(hit, -jnp.inf, score)
    return sel


_NEG_INF_KEY = -2139095041
_INT_MIN = -2147483648


def _key_of(x):
    bits = lax.bitcast_convert_type(x, jnp.int32)
    return jnp.where(bits < 0, bits ^ jnp.int32(0x7FFFFFFF), bits)


def _dsa_body(iq_ref, ik_ref, misc_ref, aq_ref, akv_ref, o_ref, key_scr, *, topk, seq):
    TQ = DSA_TQ
    c = pl.program_id(1)
    q0 = c * TQ
    nchunk = (q0 + TQ + KC - 1) // KC
    row = q0 + lax.broadcasted_iota(jnp.int32, (TQ, 1), 0)
    col = _lane_iota((1, KC))
    lane = _lane_iota((1, LANES))
    idx_scale = (IDX_HEADS * IDX_DIM) ** -0.5

    iq = jnp.concatenate([iq_ref[0, :, 256 * h:256 * (h + 1)] for h in range(IDX_HEADS)], axis=0)
    misc = misc_ref[0]
    iw = [misc[:, MISC_IW + h:MISC_IW + h + 1] for h in range(IDX_HEADS)]

    def score_body(j, _):
        ks = pl.multiple_of(j * KC, KC)
        lg = jnp.maximum(_dot_nt(iq, ik_ref[0, pl.ds(ks, KC), :]), 0.0)
        sc = iw[0] * lg[0:TQ]
        for h in range(1, IDX_HEADS):
            sc = sc + iw[h] * lg[TQ * h:TQ * (h + 1)]
        sc = sc * idx_scale
        sc = jnp.where(sc == 0.0, 0.0, sc)
        key_scr[j] = _key_of(jnp.where(ks + col <= row, sc, -jnp.inf))
        return 0

    lax.fori_loop(0, nchunk, score_body, 0)

    strip = lambda x, rows: jnp.broadcast_to(x[rows], (COUNT_ROWS, LANES))

    def count(pred):
        outs = []
        for r in range(TQ // COUNT_ROWS):
            rows = slice(COUNT_ROWS * r, COUNT_ROWS * (r + 1))
            hit = pred(rows)

            def body(j, acc, rows=rows, hit=hit):
                for g in range(KC // LANES):
                    k = key_scr[j, rows, LANES * g:LANES * (g + 1)]
                    acc = acc + jnp.where(hit(k, j * KC + LANES * g), 1.0, 0.0)
                return acc

            acc = lax.fori_loop(0, nchunk, body, jnp.zeros((COUNT_ROWS, LANES), F32))
            outs.append(jnp.sum(acc, axis=-1, keepdims=True))
        return jnp.concatenate(outs, axis=0)

    def bit_body(i, t):
        cand = t + lax.shift_left(jnp.int32(1), 31 - i)
        cnt = count(lambda rows: (lambda k, p0, c=strip(cand, rows): k >= c))
        return jnp.where(cnt >= topk, cand, t)

    thr = lax.fori_loop(0, 32, bit_body, jnp.full((TQ, 1), _INT_MIN, jnp.int32))

    n_gt = count(lambda rows: (lambda k, p0, t=strip(thr, rows): k > t))
    n_ge = count(lambda rows: (lambda k, p0, t=strip(thr, rows): k >= t))
    need = topk - n_gt
    finite = thr > _NEG_INF_KEY
    any_tie = jnp.max(jnp.where(finite, n_ge, 0.0)) > topk

    def tie_cut():
        def pos_body(i, cut):
            cand = cut + lax.shift_left(jnp.int32(1), (seq.bit_length() - 1) - i)
            cnt = count(lambda rows: (lambda k, p0, t=strip(thr, rows), c=strip(cand, rows):
                                      (k == t) & (p0 + lane < c)))
            return jnp.where(cnt < need, cand, cut)
        return lax.fori_loop(0, seq.bit_length(), pos_body, jnp.zeros((TQ, 1), jnp.int32))

    cut = lax.cond(any_tie, tie_cut, lambda: jnp.full((TQ, 1), seq, jnp.int32))
    cut = jnp.where(finite, cut, -1)

    q = jnp.concatenate([aq_ref[0, :, LANES * h:LANES * (h + 1)] for h in range(A_HEADS)], axis=0)

    def att_body(j, carry):
        kv = akv_ref[0, pl.ds(pl.multiple_of(j * KC, KC), KC), :]
        k = key_scr[j]
        taken = (k > thr) | ((k == thr) & (j * KC + col <= cut))
        s3 = _dot_nt(q, kv).reshape(A_HEADS, TQ, KC) + jnp.where(taken, 0.0, NEG_BIG)[None]
        return _softmax_step(carry, s3, _pv_shared(kv))

    carry = lax.fori_loop(0, nchunk, att_body, _online_init(A_HEADS, TQ))
    _store_heads(o_ref, _online_finish(carry), A_HEADS)


def _dsa(p, batch, seq):
    topk = min(DSA_TOPK_MAX, seq // 4)
    r3 = lambda a: a.reshape(batch, seq, a.shape[-1])
    qblk = lambda c: pl.BlockSpec((1, DSA_TQ, c), lambda b, i: (b, i, 0))
    full = lambda c: pl.BlockSpec((1, seq, c), lambda b, i: (b, 0, 0))
    return pl.pallas_call(
        functools.partial(_dsa_body, topk=topk, seq=seq),
        grid=(batch, seq // DSA_TQ),
        in_specs=[qblk(4 * 256), full(256), qblk(LANES), qblk(4 * LANES), full(LANES)],
        out_specs=qblk(A_HEADS * HEAD_DIM),
        out_shape=jax.ShapeDtypeStruct((batch, seq, A_HEADS * HEAD_DIM), BF16),
        scratch_shapes=[pltpu.VMEM((seq // KC, DSA_TQ, KC), jnp.int32)],
        compiler_params=_cparams(("parallel", "arbitrary")),
        name="dsa",
    )(r3(p["iq"]), r3(p["ik"]), r3(p["misc"]), r3(p["aq"]), r3(p["akv"]))


def _pv_per_head(kvs):
    return lambda p: jnp.concatenate(
        [jnp.dot(p[h], kvs[h], preferred_element_type=F32) for h in range(len(kvs))], axis=0)


def _moba_body(bq_ref, bqp_ref, bkv_ref, km_ref, hot_ref, o_ref, *, nbp):
    TQ = MOBA_TQ
    c = pl.program_id(1)
    q0 = c * TQ
    own = q0 // MOBA_BLOCK
    row = q0 + lax.broadcasted_iota(jnp.int32, (TQ, 1), 0)
    col = _lane_iota((1, MOBA_BLOCK))
    ks_own = pl.multiple_of(own * MOBA_BLOCK, MOBA_BLOCK)
    heads = range(B_HEADS)
    kv_at = lambda ks, h: bkv_ref[0, pl.ds(ks, MOBA_BLOCK), LANES * h:LANES * (h + 1)]

    blk = lax.broadcasted_iota(jnp.int32, (nbp, 1), 0)
    gate = jnp.concatenate(
        [_dot_nt(km_ref[0, :nbp, HEAD_DIM * h:HEAD_DIM * (h + 1)], bq_ref[0, :, HEAD_DIM * h:HEAD_DIM * (h + 1)],
                 precision=HIGHEST) for h in heads], axis=1)
    picked = (blk < own) & (_topn_mask(jnp.where(blk < own, gate, -jnp.inf), MOBA_TOPK, axis=0) > 0.0)
    bias_t = jnp.where(picked, 0.0, NEG_BIG)
    fill = jnp.zeros((LANES - nbp, TQ), F32)

    q_aug, s_own = [], []
    for h in heads:
        bias = jnp.concatenate([bias_t[:, TQ * h:TQ * (h + 1)], fill], axis=0).T
        q = bqp_ref[0, :, LANES * h:LANES * (h + 1)]
        q_aug.append(jnp.concatenate([q, bias.astype(BF16)], axis=1))
        s_own.append(jnp.where(ks_own + col <= row, _dot_nt(q, kv_at(ks_own, h)), NEG_BIG))

    stack = lambda xs: jnp.concatenate(xs, axis=0).reshape(B_HEADS, TQ, MOBA_BLOCK)
    carry = _softmax_step(_online_init(B_HEADS, TQ), stack(s_own),
                          _pv_per_head([kv_at(ks_own, h) for h in heads]))

    def scores(j):
        ks = pl.multiple_of(j * MOBA_BLOCK, MOBA_BLOCK)
        hot = hot_ref[pl.ds(ks, MOBA_BLOCK), :]
        return stack([_dot_nt(q_aug[h], jnp.concatenate([kv_at(ks, h), hot], axis=1)) for h in heads])

    def body(j, carry):
        state, s_cur = carry
        s_next = scores(jnp.minimum(j + 1, own - 1))
        ks = pl.multiple_of(j * MOBA_BLOCK, MOBA_BLOCK)
        return _softmax_step(state, s_cur, _pv_per_head([kv_at(ks, h) for h in heads])), s_next

    carry, _ = lax.fori_loop(0, own, body, (carry, scores(0)))
    _store_heads(o_ref, _online_finish(carry), B_HEADS)


def _moba(p, hot, batch, seq):
    nb = seq // MOBA_BLOCK
    r3 = lambda a: a.reshape(batch, seq, a.shape[-1])
    km = p["km"][:, :PROJ_TM // MOBA_BLOCK].reshape(batch, nb, 256)
    km = jnp.pad(km, ((0, 0), (0, LANES - nb), (0, 0)))
    qblk = lambda c: pl.BlockSpec((1, MOBA_TQ, c), lambda b, i: (b, i, 0))
    return pl.pallas_call(
        functools.partial(_moba_body, nbp=min(LANES, -(-nb // 8) * 8)),
        grid=(batch, seq // MOBA_TQ),
        in_specs=[qblk(256), qblk(4 * LANES),
                  pl.BlockSpec((1, seq, 4 * LANES), lambda b, i: (b, 0, 0)),
                  pl.BlockSpec((1, LANES, 256), lambda b, i: (b, 0, 0)),
                  pl.BlockSpec((seq, LANES), lambda b, i: (0, 0))],
        out_specs=qblk(B_HEADS * HEAD_DIM),
        out_shape=jax.ShapeDtypeStruct((batch, seq, B_HEADS * HEAD_DIM), BF16),
        compiler_params=_cparams(("parallel", "arbitrary")),
        name="moba",
    )(r3(p["bq"]), r3(p["bqp"]), r3(p["bkv"]), km, hot)


def _nsa_body(cqn_ref, cqr_ref, misc_ref, kc_ref, vc_ref, cslc_ref, cwin_ref, ov_ref, hot_ref, o_ref,
              *, ncp, n_sel):
    TQ = NSA_TQ
    c = pl.program_id(1)
    q0 = c * TQ
    last = (q0 + TQ - 1) // KC
    ks_last = pl.multiple_of(last * KC, KC)
    row = q0 + lax.broadcasted_iota(jnp.int32, (TQ, 1), 0)
    col = _lane_iota((1, KC))
    gates = jax.nn.sigmoid(misc_ref[0])
    cmp_vis = _lane_iota((1, ncp)) * CMP_STRIDE + (CMP_BLOCK - 1) <= row
    wstart = pl.multiple_of(jnp.maximum(q0 - WINDOW, 0), TQ)
    wlen = WINDOW + TQ
    wdiff = row - (wstart + _lane_iota((1, wlen)))
    wmask = (wdiff >= 0) & (wdiff < WINDOW)

    o_cmp, imp_t = [], []
    for g in range(C_GROUPS):
        qn = jnp.concatenate([cqn_ref[0, :, 256 * (C_REP * g + r):256 * (C_REP * g + r + 1)]
                              for r in range(C_REP)], axis=0)
        s3 = _dot_nt(qn, kc_ref[0, g]).reshape(C_REP, TQ, ncp)
        m = jnp.max(jnp.where(cmp_vis[None], s3, -jnp.inf), axis=-1, keepdims=True)
        m = jnp.where(m > -jnp.inf, m, 0.0)
        pc = jnp.where(cmp_vis[None], jnp.exp2(s3 - m), 0.0)
        den = jnp.sum(pc, axis=-1, keepdims=True)
        pc = pc / jnp.where(den > 0, den, 1.0)
        o_cmp.append(jnp.dot(pc.reshape(C_REP * TQ, ncp).astype(BF16), vc_ref[0, g],
                             preferred_element_type=F32))
        psum = pc[0]
        for r in range(1, C_REP):
            psum = psum + pc[r]
        imp = jnp.dot(psum, ov_ref[...], precision=HIGHEST, preferred_element_type=F32)
        imp_t.append(imp.T)

    blk = lax.broadcasted_iota(jnp.int32, (LANES, 1), 0)
    jq = (q0 + _lane_iota((1, C_GROUPS * TQ)) % TQ) // SLC_BLOCK
    adm = blk <= jq
    forced = adm & ((blk == 0) | (blk == jq) | (blk == jq - 1))
    score = jnp.where(forced, FORCE_SCORE, jnp.where(adm, jnp.concatenate(imp_t, axis=1), -jnp.inf))
    picked = adm & (_topn_mask(score, n_sel, axis=0) > 0.0)
    bias_t = jnp.where(picked, 0.0, NEG_BIG)

    for g in range(C_GROUPS):
        o_c = o_cmp[g]
        bias = bias_t[:, TQ * g:TQ * (g + 1)].T.astype(BF16)

        qr = jnp.concatenate([cqr_ref[0, :, LANES * (C_REP * g + r):LANES * (C_REP * g + r + 1)]
                              for r in range(C_REP)], axis=0)
        qa = jnp.concatenate([qr, jnp.concatenate([bias] * C_REP, axis=0)], axis=1)
        kv_at = lambda j: cslc_ref[0, pl.ds(pl.multiple_of(j * KC, KC), KC), LANES * g:LANES * (g + 1)]
        slc_scores = lambda j: _dot_nt(qa, jnp.concatenate(
            [kv_at(j), hot_ref[pl.ds(pl.multiple_of(j * KC, KC), KC), :]], axis=1))

        def slc_body(j, carry):
            return _softmax_step(carry, slc_scores(j).reshape(C_REP, TQ, KC), _pv_shared(kv_at(j)))

        carry = lax.fori_loop(0, last, slc_body, _online_init(C_REP, TQ))
        s3 = jnp.where((ks_last + col <= row)[None], slc_scores(last).reshape(C_REP, TQ, KC), NEG_BIG)
        o_s = _online_finish(_softmax_step(carry, s3, _pv_shared(kv_at(last))))

        kvw = cwin_ref[0, pl.ds(wstart, wlen), LANES * g:LANES * (g + 1)]
        s3 = jnp.where(wmask[None], _dot_nt(qr, kvw).reshape(C_REP, TQ, wlen), NEG_BIG)
        o_w = _online_finish(_softmax_step(_online_init(C_REP, TQ), s3, _pv_shared(kvw)))

        outs = []
        for r in range(C_REP):
            hh = C_REP * g + r
            gcol = lambda j: gates[:, MISC_CG + 3 * hh + j:MISC_CG + 3 * hh + j + 1]
            rows = slice(TQ * r, TQ * (r + 1))
            outs.append(gcol(0) * o_c[rows] + gcol(1) * o_s[rows] + gcol(2) * o_w[rows])
        _store_heads(o_ref, jnp.concatenate(outs, axis=0), C_REP, col0=C_REP * HEAD_DIM * g)


def _nsa(p, kcmp, vcmp, overlap, hot, batch, seq):
    ncp = seq // CMP_STRIDE
    n_sel = min(SLC_TOPN, seq // SLC_BLOCK)
    r3 = lambda a: a.reshape(batch, seq, a.shape[-1])
    qblk = lambda c: pl.BlockSpec((1, NSA_TQ, c), lambda b, i: (b, i, 0))
    full = lambda c: pl.BlockSpec((1, seq, c), lambda b, i: (b, 0, 0))
    return pl.pallas_call(
        functools.partial(_nsa_body, ncp=ncp, n_sel=n_sel),
        grid=(batch, seq // NSA_TQ),
        in_specs=[qblk(8 * 256), qblk(8 * LANES), qblk(LANES),
                  pl.BlockSpec((1, C_GROUPS, ncp, 256), lambda b, i: (b, 0, 0, 0)),
                  pl.BlockSpec((1, C_GROUPS, ncp, LANES), lambda b, i: (b, 0, 0, 0)),
                  full(2 * LANES), full(2 * LANES),
                  pl.BlockSpec((ncp, LANES), lambda b, i: (0, 0)),
                  pl.BlockSpec((seq, LANES), lambda b, i: (0, 0))],
        out_specs=qblk(C_HEADS * HEAD_DIM),
        out_shape=jax.ShapeDtypeStruct((batch, seq, C_HEADS * HEAD_DIM), BF16),
        compiler_params=_cparams(("parallel", "arbitrary")),
        name="nsa",
    )(r3(p["cqn"]), r3(p["cqr"]), r3(p["misc"]), kcmp, vcmp, r3(p["cslc"]), r3(p["cwin"]), overlap, hot)


def _merge_body(x_ref, g_ref, oa_ref, ob_ref, oc_ref, wm_ref, wa_ref, wb_ref, wc_ref, wo_ref, o_ref):
    x = x_ref[...]
    d = x.shape[1]
    h = _rms(x, g_ref[...]).astype(BF16)
    merged = None
    for i, (o_r, w_r) in enumerate(((oa_ref, wa_ref), (ob_ref, wb_ref), (oc_ref, wc_ref))):
        gate = jax.nn.sigmoid(jnp.dot(h, wm_ref[:, d * i:d * (i + 1)], preferred_element_type=F32))
        y = gate * jnp.dot(o_r[...], w_r[...], preferred_element_type=F32)
        merged = y if merged is None else merged + y
    o_ref[...] = x + jnp.dot(merged.astype(BF16), wo_ref[...], preferred_element_type=F32)


def _merge(x2, g, oa, ob, oc, wm, wa, wb, wc, wo):
    m, d = x2.shape
    tm = MERGE_TM
    row = lambda c: pl.BlockSpec((tm, c), lambda i: (i, 0))
    const = lambda a: pl.BlockSpec(a.shape, lambda i: (0, 0))
    return pl.pallas_call(
        _merge_body,
        grid=(m // tm,),
        in_specs=[row(d), const(g), row(oa.shape[1]), row(ob.shape[1]), row(oc.shape[1]),
                  const(wm), const(wa), const(wb), const(wc), const(wo)],
        out_specs=row(d),
        out_shape=jax.ShapeDtypeStruct((m, d), F32),
        compiler_params=_cparams(("parallel",)),
        name="merge",
    )(x2, g, oa, ob, oc, wm, wa, wb, wc, wo)


def _rope_tables(seq):
    half = HEAD_DIM // 2
    inv = ROPE_THETA ** (-jnp.arange(half, dtype=F32) / half)
    ang = jnp.arange(seq, dtype=F32)[:, None] * inv[None, :]
    lane = np.arange(LANES)
    ang = ang[:, lane % half]
    second = jnp.asarray((lane % HEAD_DIM) >= half)[None, :]
    sin = jnp.sin(ang)
    return jnp.cos(ang), jnp.where(second, sin, 0.0), jnp.where(second, 0.0, -sin)


def _overlap(seq):
    ncp = seq // CMP_STRIDE
    nc = (seq - CMP_BLOCK) // CMP_STRIDE + 1
    ns = seq // SLC_BLOCK
    cs = np.arange(ncp) * CMP_STRIDE
    ss = np.arange(ns) * SLC_BLOCK
    ov = (cs[:, None] < ss[None, :] + SLC_BLOCK) & (ss[None, :] <= cs[:, None] + CMP_BLOCK - 1)
    ov &= (np.arange(ncp) < nc)[:, None]
    out = np.zeros((ncp, LANES), np.float32)
    out[:, :ns] = ov
    return jnp.asarray(out)


def _block_onehot(seq, block):
    return jnp.asarray(np.arange(seq)[:, None] // block == np.arange(LANES)[None, :], BF16)


def kernel(x, ffn1_norm, ffn1_w_gate, ffn1_w_up, ffn1_w_down, mix_norm, w_in, cmp_pos_k, cmp_w1_k, cmp_w2_k, cmp_pos_v, cmp_w1_v, cmp_w2_v, w_branch_a, w_branch_b, w_branch_c, w_out, ffn2_norm, ffn2_w_gate, ffn2_w_up, ffn2_w_down, final_norm):
    batch, seq, d = x.shape
    depth = w_in.shape[0]
    assert seq % max(KC, PROJ_TM) == 0 and WINDOW + NSA_TQ <= seq <= SLC_BLOCK * LANES and w_in.shape[2] == _N_IN
    nc = (seq - CMP_BLOCK) // CMP_STRIDE + 1
    ncp = seq // CMP_STRIDE

    perm = _in_perm()
    w_perm = jnp.where(jnp.asarray(perm >= 0)[None, None, :],
                       jnp.take(w_in, jnp.asarray(np.maximum(perm, 0)), axis=2), 0.0).astype(BF16)
    w_mg = w_in[:, :, _O_MG:].astype(BF16)
    half_rows = CMP_BLOCK * HEAD_DIM // 2
    w1 = jnp.stack([cmp_w1_k, cmp_w1_v], axis=1)
    w1 = jnp.concatenate([w1[:, :, :half_rows], w1[:, :, half_rows:]], axis=-1)
    w2 = jnp.pad(jnp.stack([cmp_w2_k, cmp_w2_v], axis=1), ((0, 0), (0, 0), (0, 0), (0, LANES - HEAD_DIM)))
    pos = jnp.stack([cmp_pos_k, cmp_pos_v], axis=1).reshape(depth, 2, 2, half_rows)
    pos = jnp.pad(pos, ((0, 0), (0, 0), (0, 6), (0, 0)))
    cos, sa, sb = _rope_tables(seq)
    overlap = _overlap(seq)
    hot_b = _block_onehot(seq, MOBA_BLOCK)
    hot_s = _block_onehot(seq, SLC_BLOCK)
    bf = lambda a: a.astype(BF16)
    row = lambda a: a.reshape(1, d)

    x2 = x.reshape(batch * seq, d)
    for l in range(depth):
        x2 = _ffn(x2, row(ffn1_norm[l]), bf(ffn1_w_gate[l]), bf(ffn1_w_up[l]), bf(ffn1_w_down[l]),
                  row(final_norm), False)
        p = _inproj(x2, row(mix_norm[l]), w_perm[l], cos, sa, sb, seq)
        xc = p["ccmp"].reshape(batch, seq, 4, HEAD_DIM).transpose(0, 2, 1, 3).reshape(batch, 4, ncp, 1024)
        kc, vc = _compress(xc, w1[l], w2[l], pos[l], nc)
        o_a = _dsa(p, batch, seq)
        o_b = _moba(p, hot_b, batch, seq)
        o_c = _nsa(p, kc[:, :C_GROUPS], vc[:, C_GROUPS:], overlap, hot_s, batch, seq)
        flat = lambda a: a.reshape(batch * seq, a.shape[-1])
        x2 = _merge(x2, row(mix_norm[l]), flat(o_a), flat(o_b), flat(o_c), w_mg[l],
                    bf(w_branch_a[l]), bf(w_branch_b[l]), bf(w_branch_c[l]), bf(w_out[l]))
        x2 = _ffn(x2, row(ffn2_norm[l]), bf(ffn2_w_gate[l]), bf(ffn2_w_up[l]), bf(ffn2_w_down[l]),
                  row(final_norm), l == depth - 1)
    return x2.reshape(batch, seq, d)
```

```python
import functools
import math

import numpy as np
import jax
import jax.numpy as jnp
from jax import lax
from jax.experimental import pallas as pl
from jax.experimental.pallas import tpu as pltpu

HEAD_DIM = 64
ROPE_THETA = 10000.0
NORM_EPS = 1e-6
A_HEADS = 4
IDX_HEADS = 4
IDX_DIM = 64
DSA_TOPK_MAX = 256
B_HEADS = 4
MOBA_BLOCK = 256
MOBA_TOPK = 3
C_HEADS = 8
C_GROUPS = 2
C_REP = C_HEADS // C_GROUPS
CMP_BLOCK = 32
CMP_STRIDE = 16
CMP_HIDDEN = 128
SLC_BLOCK = 64
SLC_TOPN = 16
WINDOW = 512
FORCE_SCORE = 1e30

LANES = 128
VMEM_LIMIT = 56 * 1024 * 1024

DSA_TQ = 256
MOBA_TQ = 256
NSA_TQ = 256
KC = 512
COUNT_KEYS = 128
FFN_TM = 512
PROJ_TM = 512
MERGE_TM = 512

NEG_BIG = -1e30
QSCALE = HEAD_DIM ** -0.5 * math.log2(math.e)
F32 = jnp.float32
BF16 = jnp.bfloat16
HIGHEST = lax.Precision.HIGHEST

_O_AQ = 0
_O_AK = 256
_O_AV = 320
_O_IQ = 384
_O_IK = 640
_O_IW = 704
_O_BQ = 708
_O_BK = 964
_O_BV = 1220
_O_CQ = 1476
_O_CKV = 1988
_O_CG = 2756
_O_MG = 2780
_N_IN = 5852

P_AQ = 0
P_AK = 256
P_IK = 320
P_IQ = 384
P_BQ = 640
P_BK = 896
P_CKS = 1152
P_CKW = 1280
P_CQ = 1408
P_ROPE_END = 1920
P_BV = 1920
P_KCMP = 2176
P_VS = 2432
P_VW = 2560
P_AV = 2688
P_MISC = 2752
P_COLS = 2816
MISC_CG = 0
MISC_IW = 24


def _in_perm():
    perm = -np.ones((P_COLS,), np.int64)

    def put(dst, src, n):
        perm[dst:dst + n] = np.arange(src, src + n)

    put(P_AQ, _O_AQ, 256)
    put(P_AK, _O_AK, 64)
    put(P_IK, _O_IK, 64)
    put(P_IQ, _O_IQ, 256)
    put(P_BQ, _O_BQ, 256)
    put(P_BK, _O_BK, 256)
    ckv = lambda s, g: _O_CKV + (s * C_GROUPS + g) * HEAD_DIM
    for g in range(C_GROUPS):
        put(P_CKS + 64 * g, ckv(2, g), 64)
        put(P_CKW + 64 * g, ckv(4, g), 64)
        put(P_KCMP + 64 * g, ckv(0, g), 64)
        put(P_KCMP + 128 + 64 * g, ckv(1, g), 64)
        put(P_VS + 64 * g, ckv(3, g), 64)
        put(P_VW + 64 * g, ckv(5, g), 64)
    put(P_CQ, _O_CQ, 512)
    put(P_BV, _O_BV, 256)
    put(P_AV, _O_AV, 64)
    put(P_MISC + MISC_CG, _O_CG, 24)
    put(P_MISC + MISC_IW, _O_IW, 4)
    return perm


def _cparams(sem):
    return pltpu.CompilerParams(dimension_semantics=sem, vmem_limit_bytes=VMEM_LIMIT)


def _rms(x, g):
    y = x * lax.rsqrt(jnp.mean(x * x, axis=-1, keepdims=True) + NORM_EPS)
    return y * g


def _dot_nt(a, b, precision=None):
    return lax.dot_general(a, b, (((1,), (1,)), ((), ())), precision=precision,
                           preferred_element_type=F32)


def _ffn_body(x_ref, g_ref, wg_ref, wu_ref, wd_ref, fg_ref, o_ref, h_scr, acc_scr, *, final_norm):
    f = pl.program_id(1)

    @pl.when(f == 0)
    def _():
        h_scr[...] = _rms(x_ref[...], g_ref[...]).astype(BF16)
        acc_scr[...] = jnp.zeros_like(acc_scr)

    h = h_scr[...]
    a = jnp.dot(h, wg_ref[...], preferred_element_type=F32)
    u = jnp.dot(h, wu_ref[...], preferred_element_type=F32)
    act = (a * jax.nn.sigmoid(a) * u).astype(BF16)
    acc_scr[...] += jnp.dot(act, wd_ref[...], preferred_element_type=F32)

    @pl.when(f == pl.num_programs(1) - 1)
    def _():
        y = x_ref[...] + 0.5 * acc_scr[...]
        if final_norm:
            y = _rms(y, fg_ref[...])
        o_ref[...] = y


def _ffn(x2, g, wg, wu, wd, fg, final_norm):
    m, d = x2.shape
    dff = wg.shape[1]
    tf = dff // 2 if (dff // 2) % LANES == 0 else dff
    tm = FFN_TM
    return pl.pallas_call(
        functools.partial(_ffn_body, final_norm=final_norm),
        grid=(m // tm, dff // tf),
        in_specs=[
            pl.BlockSpec((tm, d), lambda i, f: (i, 0)),
            pl.BlockSpec((1, d), lambda i, f: (0, 0)),
            pl.BlockSpec((d, tf), lambda i, f: (0, f)),
            pl.BlockSpec((d, tf), lambda i, f: (0, f)),
            pl.BlockSpec((tf, d), lambda i, f: (f, 0)),
            pl.BlockSpec((1, d), lambda i, f: (0, 0)),
        ],
        out_specs=pl.BlockSpec((tm, d), lambda i, f: (i, 0)),
        out_shape=jax.ShapeDtypeStruct((m, d), F32),
        scratch_shapes=[pltpu.VMEM((tm, d), BF16), pltpu.VMEM((tm, d), F32)],
        compiler_params=_cparams(("parallel", "arbitrary")),
        name="ffn",
    )(x2, g, wg, wu, wd, fg)


def _lane_iota(shape):
    return lax.broadcasted_iota(jnp.int32, shape, len(shape) - 1)


def _split_hi_lo(x):
    hi = x.astype(BF16).astype(F32)
    return hi, x - hi


def _inproj_body(x_ref, g_ref, w_ref, cos_ref, sa_ref, sb_ref,
                 aq_ref, akv_ref, iq_ref, ik_ref, misc_ref, bq_ref, bqp_ref, bkv_ref, km_ref,
                 cqn_ref, cqr_ref, cslc_ref, cwin_ref, ccmp_ref):
    tm = x_ref.shape[0]
    h = _rms(x_ref[...], g_ref[...]).astype(BF16)
    cos, sa, sb = cos_ref[...], sa_ref[...], sb_ref[...]
    lane = _lane_iota((tm, LANES))
    low = lane < HEAD_DIM

    raw = []
    rot = []
    for j in range(P_COLS // 256):
        z = jnp.dot(h, w_ref[:, 256 * j:256 * (j + 1)], preferred_element_type=F32)
        for half in range(2):
            p = z[:, LANES * half:LANES * (half + 1)]
            raw.append(p)
            if LANES * len(raw) <= P_ROPE_END:
                rot.append(p * cos + pltpu.roll(p, 32, axis=1) * sa + pltpu.roll(p, 96, axis=1) * sb)

    def pick(pieces, col):
        p = pieces[col // LANES]
        return pltpu.roll(p, HEAD_DIM, axis=1) if col % LANES else p

    def join(lo, hi=None):
        if hi is None:
            return jnp.where(low, lo, 0.0)
        return jnp.where(low, lo, pltpu.roll(hi, HEAD_DIM, axis=1))

    scale = QSCALE

    for hh in range(A_HEADS):
        aq_ref[:, LANES * hh:LANES * (hh + 1)] = join(pick(rot, P_AQ + 64 * hh) * scale).astype(BF16)
        qhi, qlo = _split_hi_lo(pick(rot, P_IQ + 64 * hh))
        iq_ref[:, 256 * hh:256 * hh + LANES] = join(qhi, qlo).astype(BF16)
        iq_ref[:, 256 * hh + LANES:256 * (hh + 1)] = join(qhi).astype(BF16)
    akv_ref[...] = join(pick(rot, P_AK), pick(raw, P_AV)).astype(BF16)
    khi, klo = _split_hi_lo(pick(rot, P_IK))
    ik_ref[:, :LANES] = join(khi, khi).astype(BF16)
    ik_ref[:, LANES:] = join(klo).astype(BF16)
    misc_ref[...] = pick(raw, P_MISC)

    bq_ref[:, :LANES] = rot[P_BQ // LANES]
    bq_ref[:, LANES:] = rot[P_BQ // LANES + 1]
    for hh in range(B_HEADS):
        bqp_ref[:, LANES * hh:LANES * (hh + 1)] = join(pick(rot, P_BQ + 64 * hh) * scale).astype(BF16)
        bkv_ref[:, LANES * hh:LANES * (hh + 1)] = join(pick(rot, P_BK + 64 * hh),
                                                      pick(raw, P_BV + 64 * hh)).astype(BF16)
    nblk = tm // MOBA_BLOCK
    km_ref[...] = jnp.zeros_like(km_ref)
    for half in range(2):
        kp = rot[P_BK // LANES + half]
        for b in range(nblk):
            km_ref[0, b:b + 1, LANES * half:LANES * (half + 1)] = jnp.mean(
                kp[MOBA_BLOCK * b:MOBA_BLOCK * (b + 1)], axis=0, keepdims=True)

    for hh in range(C_HEADS):
        nhi, nlo = _split_hi_lo(pick(raw, P_CQ + 64 * hh) * scale)
        cqn_ref[:, 256 * hh:256 * hh + LANES] = join(nhi, nlo).astype(BF16)
        cqn_ref[:, 256 * hh + LANES:256 * (hh + 1)] = join(nhi).astype(BF16)
        cqr_ref[:, LANES * hh:LANES * (hh + 1)] = join(pick(rot, P_CQ + 64 * hh) * scale).astype(BF16)
    for g in range(C_GROUPS):
        cslc_ref[:, LANES * g:LANES * (g + 1)] = join(pick(rot, P_CKS + 64 * g),
                                                     pick(raw, P_VS + 64 * g)).astype(BF16)
        cwin_ref[:, LANES * g:LANES * (g + 1)] = join(pick(rot, P_CKW + 64 * g),
                                                     pick(raw, P_VW + 64 * g)).astype(BF16)
    ccmp_ref[:, :LANES] = raw[P_KCMP // LANES]
    ccmp_ref[:, LANES:] = raw[P_KCMP // LANES + 1]


def _inproj(x2, g, w, cos, sa, sb, seq):
    m, d = x2.shape
    tm = PROJ_TM
    nt = seq // tm
    row = lambda c: pl.BlockSpec((tm, c), lambda i: (i, 0))
    tab = pl.BlockSpec((tm, LANES), lambda i: (i % nt, 0))
    outs = [
        ("aq", 4 * LANES, BF16), ("akv", LANES, BF16), ("iq", 4 * 256, BF16), ("ik", 256, BF16),
        ("misc", LANES, F32), ("bq", 256, F32), ("bqp", 4 * LANES, BF16), ("bkv", 4 * LANES, BF16),
        ("km", None, F32),
        ("cqn", 8 * 256, BF16), ("cqr", 8 * LANES, BF16), ("cslc", 2 * LANES, BF16),
        ("cwin", 2 * LANES, BF16), ("ccmp", 256, F32),
    ]
    out_specs, out_shape = [], []
    for name, c, dt in outs:
        if name == "km":
            out_specs.append(pl.BlockSpec((1, 8, 256), lambda i: (i, 0, 0)))
            out_shape.append(jax.ShapeDtypeStruct((m // tm, 8, 256), dt))
        else:
            out_specs.append(row(c))
            out_shape.append(jax.ShapeDtypeStruct((m, c), dt))
    res = pl.pallas_call(
        _inproj_body,
        grid=(m // tm,),
        in_specs=[row(d), pl.BlockSpec((1, d), lambda i: (0, 0)),
                  pl.BlockSpec((d, P_COLS), lambda i: (0, 0)), tab, tab, tab],
        out_specs=out_specs,
        out_shape=out_shape,
        compiler_params=_cparams(("parallel",)),
        name="inproj",
    )(x2, g, w, cos, sa, sb)
    return dict(zip([o[0] for o in outs], res))


def _compress_body(x_ref, w1_ref, w2_ref, pos_ref, ok_ref, ov_ref, *, n_valid):
    x = x_ref[0, 0]
    w1 = w1_ref[0]
    pre = jnp.dot(x, w1, precision=HIGHEST, preferred_element_type=F32)
    pp = jnp.dot(pos_ref[0], w1, precision=HIGHEST, preferred_element_type=F32)
    posb = pp[0:1, :CMP_HIDDEN] + pp[1:2, CMP_HIDDEN:]
    ncp = x.shape[0]
    upper = pre[:, :CMP_HIDDEN]
    lower_next = pltpu.roll(pre[:, CMP_HIDDEN:], ncp - 1, axis=0)
    hid = jax.nn.gelu(upper + lower_next + posb)
    out = jnp.dot(hid, w2_ref[0], precision=HIGHEST, preferred_element_type=F32)
    rows = lax.broadcasted_iota(jnp.int32, out.shape, 0)
    out = jnp.where(rows < n_valid, out, 0.0)
    low = _lane_iota(out.shape) < HEAD_DIM
    hi, lo = _split_hi_lo(out)
    ok_ref[0, 0, :, :LANES] = jnp.where(low, hi, pltpu.roll(hi, HEAD_DIM, axis=1)).astype(BF16)
    ok_ref[0, 0, :, LANES:] = lo.astype(BF16)
    ov_ref[0, 0] = pltpu.roll(out, HEAD_DIM, axis=1).astype(BF16)


def _compress(xc, w1, w2, pos, n_valid):
    b, four, ncp, _ = xc.shape
    return pl.pallas_call(
        functools.partial(_compress_body, n_valid=n_valid),
        grid=(b, four),
        in_specs=[
            pl.BlockSpec((1, 1, ncp, 1024), lambda i, j: (i, j, 0, 0)),
            pl.BlockSpec((1, 1024, 256), lambda i, j: (j // 2, 0, 0)),
            pl.BlockSpec((1, CMP_HIDDEN, LANES), lambda i, j: (j // 2, 0, 0)),
            pl.BlockSpec((1, 8, 1024), lambda i, j: (j // 2, 0, 0)),
        ],
        out_specs=[pl.BlockSpec((1, 1, ncp, 256), lambda i, j: (i, j, 0, 0)),
                   pl.BlockSpec((1, 1, ncp, LANES), lambda i, j: (i, j, 0, 0))],
        out_shape=[jax.ShapeDtypeStruct((b, four, ncp, 256), BF16),
                   jax.ShapeDtypeStruct((b, four, ncp, LANES), BF16)],
        compiler_params=_cparams(("parallel", "parallel")),
        name="compress",
    )(xc, w1, w2, pos)


def _softmax_step(carry, s3, pv):
    m_old, l_old, acc = carry
    heads, tq = s3.shape[0], s3.shape[1]
    m_new = jnp.maximum(m_old, jnp.max(s3, axis=-1, keepdims=True))
    p = jnp.exp2(s3 - m_new)
    alpha = jnp.exp2(m_old - m_new)
    l_new = alpha * l_old + jnp.sum(p, axis=-1, keepdims=True)
    acc = alpha.reshape(heads * tq, 1) * acc + pv(p.astype(BF16))
    return m_new, l_new, acc


def _online_init(heads, tq):
    return (jnp.full((heads, tq, 1), NEG_BIG, F32), jnp.zeros((heads, tq, 1), F32),
            jnp.zeros((heads * tq, LANES), F32))


def _online_finish(carry):
    _, l, acc = carry
    return acc / l.reshape(acc.shape[0], 1)


def _pv_shared(kv):
    return lambda p: jnp.dot(p.reshape(p.shape[0] * p.shape[1], p.shape[2]), kv, preferred_element_type=F32)


def _store_heads(o_ref, o, heads, col0=0):
    tq = o.shape[0] // heads
    low = _lane_iota((tq, LANES)) < HEAD_DIM
    for pair in range(heads // 2):
        even = o[tq * (2 * pair):tq * (2 * pair + 1)]
        odd = o[tq * (2 * pair + 1):tq * (2 * pair + 2)]
        piece = jnp.where(low, pltpu.roll(even, HEAD_DIM, axis=1), odd)
        o_ref[0, :, col0 + LANES * pair:col0 + LANES * (pair + 1)] = piece.astype(o_ref.dtype)


def _topn_mask(score, n_pick, axis=-1):
    axis = axis % score.ndim
    idx = lax.broadcasted_iota(jnp.int32, score.shape, axis)
    width = score.shape[axis]
    sel = jnp.zeros(score.shape, F32)
    for _ in range(n_pick):
        m = jnp.max(score, axis=axis, keepdims=True)
        first = jnp.min(jnp.where(score == m, idx, width), axis=axis, keepdims=True)
        hit = idx == first
        sel = jnp.where(hit, 1.0, sel)
        score = jnp.where(hit, -jnp.inf, score)
    return sel


_NEG_INF_KEY = -2139095041
_INT_MIN = -2147483648


def _key_of(x):
    bits = lax.bitcast_convert_type(x, jnp.int32)
    return jnp.where(bits < 0, bits ^ jnp.int32(0x7FFFFFFF), bits)


def _dsa_body(iq_ref, ik_ref, misc_ref, aq_ref, akv_ref, o_ref, key_scr, *, topk, seq):
    TQ = DSA_TQ
    c = pl.program_id(1)
    q0 = c * TQ
    nchunk = (q0 + TQ + KC - 1) // KC
    qpos = q0 + _lane_iota((1, TQ))
    krow = lax.broadcasted_iota(jnp.int32, (KC, 1), 0)
    srow = lax.broadcasted_iota(jnp.int32, (COUNT_KEYS, 1), 0)
    idx_scale = (IDX_HEADS * IDX_DIM) ** -0.5

    iq = jnp.concatenate([iq_ref[0, :, 256 * h:256 * (h + 1)] for h in range(IDX_HEADS)], axis=0)
    misc_t = misc_ref[0].T
    iw = [misc_t[MISC_IW + h:MISC_IW + h + 1] for h in range(IDX_HEADS)]

    def score_body(j, _):
        ks = pl.multiple_of(j * KC, KC)
        lg = jnp.maximum(_dot_nt(ik_ref[0, pl.ds(ks, KC), :], iq), 0.0)
        sc = iw[0] * lg[:, 0:TQ]
        for h in range(1, IDX_HEADS):
            sc = sc + iw[h] * lg[:, TQ * h:TQ * (h + 1)]
        sc = sc * idx_scale
        sc = jnp.where(sc == 0.0, 0.0, sc)
        key_scr[j] = _key_of(jnp.where(ks + krow <= qpos, sc, -jnp.inf))
        return 0

    lax.fori_loop(0, nchunk, score_body, 0)

    def count(hit):
        def body(j, acc):
            for g in range(KC // COUNT_KEYS):
                k = key_scr[j, COUNT_KEYS * g:COUNT_KEYS * (g + 1), :]
                one = jnp.where(hit(k, j * KC + COUNT_KEYS * g), 1.0, 0.0)
                acc = acc + jnp.sum(one.reshape(COUNT_KEYS // 8, 8, TQ), axis=0)
            return acc
        acc = lax.fori_loop(0, nchunk, body, jnp.zeros((8, TQ), F32))
        return jnp.sum(acc, axis=0, keepdims=True)

    def bit_body(i, t):
        cand = t + lax.shift_left(jnp.int32(1), 31 - i)
        cnt = count(lambda k, p0: k >= cand)
        return jnp.where(cnt >= topk, cand, t)

    thr = lax.fori_loop(0, 32, bit_body, jnp.full((1, TQ), _INT_MIN, jnp.int32))

    n_gt = count(lambda k, p0: k > thr)
    n_ge = count(lambda k, p0: k >= thr)
    need = topk - n_gt
    finite = thr > _NEG_INF_KEY
    any_tie = jnp.max(jnp.where(finite, n_ge, 0.0)) > topk

    def tie_cut():
        def pos_body(i, cut):
            cand = cut + lax.shift_left(jnp.int32(1), (seq.bit_length() - 1) - i)
            cnt = count(lambda k, p0: (k == thr) & (p0 + srow < cand))
            return jnp.where(cnt < need, cand, cut)
        return lax.fori_loop(0, seq.bit_length(), pos_body, jnp.zeros((1, TQ), jnp.int32))

    cut = lax.cond(any_tie, tie_cut, lambda: jnp.full((1, TQ), seq, jnp.int32))
    cut = jnp.where(finite, cut, -1)

    q = jnp.concatenate([aq_ref[0, :, LANES * h:LANES * (h + 1)] for h in range(A_HEADS)], axis=0)

    def att_body(j, carry):
        ks = pl.multiple_of(j * KC, KC)
        kv = akv_ref[0, pl.ds(ks, KC), :]
        k = key_scr[j]
        taken = (k > thr) | ((k == thr) & (ks + krow <= cut))
        bias = jnp.where(taken, 0.0, NEG_BIG).T
        s3 = _dot_nt(q, kv).reshape(A_HEADS, TQ, KC) + bias[None]
        return _softmax_step(carry, s3, _pv_shared(kv))

    carry = lax.fori_loop(0, nchunk, att_body, _online_init(A_HEADS, TQ))
    _store_heads(o_ref, _online_finish(carry), A_HEADS)


def _dsa(p, batch, seq):
    topk = min(DSA_TOPK_MAX, seq // 4)
    r3 = lambda a: a.reshape(batch, seq, a.shape[-1])
    qblk = lambda c: pl.BlockSpec((1, DSA_TQ, c), lambda b, i: (b, i, 0))
    full = lambda c: pl.BlockSpec((1, seq, c), lambda b, i: (b, 0, 0))
    return pl.pallas_call(
        functools.partial(_dsa_body, topk=topk, seq=seq),
        grid=(batch, seq // DSA_TQ),
        in_specs=[qblk(4 * 256), full(256), qblk(LANES), qblk(4 * LANES), full(LANES)],
        out_specs=qblk(A_HEADS * HEAD_DIM),
        out_shape=jax.ShapeDtypeStruct((batch, seq, A_HEADS * HEAD_DIM), BF16),
        scratch_shapes=[pltpu.VMEM((seq // KC, KC, DSA_TQ), jnp.int32)],
        compiler_params=_cparams(("parallel", "arbitrary")),
        name="dsa",
    )(r3(p["iq"]), r3(p["ik"]), r3(p["misc"]), r3(p["aq"]), r3(p["akv"]))


def _pv_per_head(kvs):
    return lambda p: jnp.concatenate(
        [jnp.dot(p[h], kvs[h], preferred_element_type=F32) for h in range(len(kvs))], axis=0)


def _moba_body(bq_ref, bqp_ref, bkv_ref, km_ref, hot_ref, o_ref, *, nbp):
    TQ = MOBA_TQ
    c = pl.program_id(1)
    q0 = c * TQ
    own = q0 // MOBA_BLOCK
    row = q0 + lax.broadcasted_iota(jnp.int32, (TQ, 1), 0)
    col = _lane_iota((1, MOBA_BLOCK))
    ks_own = pl.multiple_of(own * MOBA_BLOCK, MOBA_BLOCK)
    heads = range(B_HEADS)
    kv_at = lambda ks, h: bkv_ref[0, pl.ds(ks, MOBA_BLOCK), LANES * h:LANES * (h + 1)]

    blk = lax.broadcasted_iota(jnp.int32, (nbp, 1), 0)
    gate = jnp.concatenate(
        [_dot_nt(km_ref[0, :nbp, HEAD_DIM * h:HEAD_DIM * (h + 1)], bq_ref[0, :, HEAD_DIM * h:HEAD_DIM * (h + 1)],
                 precision=HIGHEST) for h in heads], axis=1)
    picked = (blk < own) & (_topn_mask(jnp.where(blk < own, gate, -jnp.inf), MOBA_TOPK, axis=0) > 0.0)
    bias_t = jnp.where(picked, 0.0, NEG_BIG)
    fill = jnp.zeros((LANES - nbp, TQ), F32)

    q_aug, s_own = [], []
    for h in heads:
        bias = jnp.concatenate([bias_t[:, TQ * h:TQ * (h + 1)], fill], axis=0).T
        q = bqp_ref[0, :, LANES * h:LANES * (h + 1)]
        q_aug.append(jnp.concatenate([q, bias.astype(BF16)], axis=1))
        s_own.append(jnp.where(ks_own + col <= row, _dot_nt(q, kv_at(ks_own, h)), NEG_BIG))

    stack = lambda xs: jnp.concatenate(xs, axis=0).reshape(B_HEADS, TQ, MOBA_BLOCK)
    carry = _softmax_step(_online_init(B_HEADS, TQ), stack(s_own),
                          _pv_per_head([kv_at(ks_own, h) for h in heads]))

    def scores(j):
        ks = pl.multiple_of(j * MOBA_BLOCK, MOBA_BLOCK)
        hot = hot_ref[pl.ds(ks, MOBA_BLOCK), :]
        return stack([_dot_nt(q_aug[h], jnp.concatenate([kv_at(ks, h), hot], axis=1)) for h in heads])

    def body(j, carry):
        state, s_cur = carry
        s_next = scores(jnp.minimum(j + 1, own - 1))
        ks = pl.multiple_of(j * MOBA_BLOCK, MOBA_BLOCK)
        return _softmax_step(state, s_cur, _pv_per_head([kv_at(ks, h) for h in heads])), s_next

    carry, _ = lax.fori_loop(0, own, body, (carry, scores(0)))
    _store_heads(o_ref, _online_finish(carry), B_HEADS)


def _moba(p, hot, batch, seq):
    nb = seq // MOBA_BLOCK
    r3 = lambda a: a.reshape(batch, seq, a.shape[-1])
    km = p["km"][:, :PROJ_TM // MOBA_BLOCK].reshape(batch, nb, 256)
    km = jnp.pad(km, ((0, 0), (0, LANES - nb), (0, 0)))
    qblk = lambda c: pl.BlockSpec((1, MOBA_TQ, c), lambda b, i: (b, i, 0))
    return pl.pallas_call(
        functools.partial(_moba_body, nbp=min(LANES, -(-nb // 8) * 8)),
        grid=(batch, seq // MOBA_TQ),
        in_specs=[qblk(256), qblk(4 * LANES),
                  pl.BlockSpec((1, seq, 4 * LANES), lambda b, i: (b, 0, 0)),
                  pl.BlockSpec((1, LANES, 256), lambda b, i: (b, 0, 0)),
                  pl.BlockSpec((seq, LANES), lambda b, i: (0, 0))],
        out_specs=qblk(B_HEADS * HEAD_DIM),
        out_shape=jax.ShapeDtypeStruct((batch, seq, B_HEADS * HEAD_DIM), BF16),
        compiler_params=_cparams(("parallel", "arbitrary")),
        name="moba",
    )(r3(p["bq"]), r3(p["bqp"]), r3(p["bkv"]), km, hot)


def _nsa_body(cqn_ref, cqr_ref, misc_ref, kc_ref, vc_ref, cslc_ref, cwin_ref, ov_ref, hot_ref, o_ref,
              *, ncp, n_sel):
    TQ = NSA_TQ
    c = pl.program_id(1)
    q0 = c * TQ
    last = (q0 + TQ - 1) // KC
    ks_last = pl.multiple_of(last * KC, KC)
    row = q0 + lax.broadcasted_iota(jnp.int32, (TQ, 1), 0)
    col = _lane_iota((1, KC))
    gates = jax.nn.sigmoid(misc_ref[0])
    cmp_vis = _lane_iota((1, ncp)) * CMP_STRIDE + (CMP_BLOCK - 1) <= row
    wstart = pl.multiple_of(jnp.maximum(q0 - WINDOW, 0), TQ)
    wlen = WINDOW + TQ
    wdiff = row - (wstart + _lane_iota((1, wlen)))
    wmask = (wdiff >= 0) & (wdiff < WINDOW)

    o_cmp, imp_t = [], []
    for g in range(C_GROUPS):
        qn = jnp.concatenate([cqn_ref[0, :, 256 * (C_REP * g + r):256 * (C_REP * g + r + 1)]
                              for r in range(C_REP)], axis=0)
        s3 = _dot_nt(qn, kc_ref[0, g]).reshape(C_REP, TQ, ncp)
        m = jnp.max(jnp.where(cmp_vis[None], s3, -jnp.inf), axis=-1, keepdims=True)
        m = jnp.where(m > -jnp.inf, m, 0.0)
        pc = jnp.where(cmp_vis[None], jnp.exp2(s3 - m), 0.0)
        den = jnp.sum(pc, axis=-1, keepdims=True)
        pc = pc / jnp.where(den > 0, den, 1.0)
        o_cmp.append(jnp.dot(pc.reshape(C_REP * TQ, ncp).astype(BF16), vc_ref[0, g],
                             preferred_element_type=F32))
        psum = pc[0]
        for r in range(1, C_REP):
            psum = psum + pc[r]
        imp = jnp.dot(psum, ov_ref[...], precision=HIGHEST, preferred_element_type=F32)
        imp_t.append(imp.T)

    blk = lax.broadcasted_iota(jnp.int32, (LANES, 1), 0)
    jq = (q0 + _lane_iota((1, C_GROUPS * TQ)) % TQ) // SLC_BLOCK
    adm = blk <= jq
    forced = adm & ((blk == 0) | (blk == jq) | (blk == jq - 1))
    score = jnp.where(forced, FORCE_SCORE, jnp.where(adm, jnp.concatenate(imp_t, axis=1), -jnp.inf))
    picked = adm & (_topn_mask(score, n_sel, axis=0) > 0.0)
    bias_t = jnp.where(picked, 0.0, NEG_BIG)

    for g in range(C_GROUPS):
        o_c = o_cmp[g]
        bias = bias_t[:, TQ * g:TQ * (g + 1)].T.astype(BF16)

        qr = jnp.concatenate([cqr_ref[0, :, LANES * (C_REP * g + r):LANES * (C_REP * g + r + 1)]
                              for r in range(C_REP)], axis=0)
        qa = jnp.concatenate([qr, jnp.concatenate([bias] * C_REP, axis=0)], axis=1)
        kv_at = lambda j: cslc_ref[0, pl.ds(pl.multiple_of(j * KC, KC), KC), LANES * g:LANES * (g + 1)]
        slc_scores = lambda j: _dot_nt(qa, jnp.concatenate(
            [kv_at(j), hot_ref[pl.ds(pl.multiple_of(j * KC, KC), KC), :]], axis=1))

        def slc_body(j, carry):
            return _softmax_step(carry, slc_scores(j).reshape(C_REP, TQ, KC), _pv_shared(kv_at(j)))

        carry = lax.fori_loop(0, last, slc_body, _online_init(C_REP, TQ))
        s3 = jnp.where((ks_last + col <= row)[None], slc_scores(last).reshape(C_REP, TQ, KC), NEG_BIG)
        o_s = _online_finish(_softmax_step(carry, s3, _pv_shared(kv_at(last))))

        kvw = cwin_ref[0, pl.ds(wstart, wlen), LANES * g:LANES * (g + 1)]
        s3 = jnp.where(wmask[None], _dot_nt(qr, kvw).reshape(C_REP, TQ, wlen), NEG_BIG)
        o_w = _online_finish(_softmax_step(_online_init(C_REP, TQ), s3, _pv_shared(kvw)))

        outs = []
        for r in range(C_REP):
            hh = C_REP * g + r
            gcol = lambda j: gates[:, MISC_CG + 3 * hh + j:MISC_CG + 3 * hh + j + 1]
            rows = slice(TQ * r, TQ * (r + 1))
            outs.append(gcol(0) * o_c[rows] + gcol(1) * o_s[rows] + gcol(2) * o_w[rows])
        _store_heads(o_ref, jnp.concatenate(outs, axis=0), C_REP, col0=C_REP * HEAD_DIM * g)


def _nsa(p, kcmp, vcmp, overlap, hot, batch, seq):
    ncp = seq // CMP_STRIDE
    n_sel = min(SLC_TOPN, seq // SLC_BLOCK)
    r3 = lambda a: a.reshape(batch, seq, a.shape[-1])
    qblk = lambda c: pl.BlockSpec((1, NSA_TQ, c), lambda b, i: (b, i, 0))
    full = lambda c: pl.BlockSpec((1, seq, c), lambda b, i: (b, 0, 0))
    return pl.pallas_call(
        functools.partial(_nsa_body, ncp=ncp, n_sel=n_sel),
        grid=(batch, seq // NSA_TQ),
        in_specs=[qblk(8 * 256), qblk(8 * LANES), qblk(LANES),
                  pl.BlockSpec((1, C_GROUPS, ncp, 256), lambda b, i: (b, 0, 0, 0)),
                  pl.BlockSpec((1, C_GROUPS, ncp, LANES), lambda b, i: (b, 0, 0, 0)),
                  full(2 * LANES), full(2 * LANES),
                  pl.BlockSpec((ncp, LANES), lambda b, i: (0, 0)),
                  pl.BlockSpec((seq, LANES), lambda b, i: (0, 0))],
        out_specs=qblk(C_HEADS * HEAD_DIM),
        out_shape=jax.ShapeDtypeStruct((batch, seq, C_HEADS * HEAD_DIM), BF16),
        compiler_params=_cparams(("parallel", "arbitrary")),
        name="nsa",
    )(r3(p["cqn"]), r3(p["cqr"]), r3(p["misc"]), kcmp, vcmp, r3(p["cslc"]), r3(p["cwin"]), overlap, hot)


def _merge_body(x_ref, g_ref, oa_ref, ob_ref, oc_ref, wm_ref, wa_ref, wb_ref, wc_ref, wo_ref, o_ref):
    x = x_ref[...]
    d = x.shape[1]
    h = _rms(x, g_ref[...]).astype(BF16)
    merged = None
    for i, (o_r, w_r) in enumerate(((oa_ref, wa_ref), (ob_ref, wb_ref), (oc_ref, wc_ref))):
        gate = jax.nn.sigmoid(jnp.dot(h, wm_ref[:, d * i:d * (i + 1)], preferred_element_type=F32))
        y = gate * jnp.dot(o_r[...], w_r[...], preferred_element_type=F32)
        merged = y if merged is None else merged + y
    o_ref[...] = x + jnp.dot(merged.astype(BF16), wo_ref[...], preferred_element_type=F32)


def _merge(x2, g, oa, ob, oc, wm, wa, wb, wc, wo):
    m, d = x2.shape
    tm = MERGE_TM
    row = lambda c: pl.BlockSpec((tm, c), lambda i: (i, 0))
    const = lambda a: pl.BlockSpec(a.shape, lambda i: (0, 0))
    return pl.pallas_call(
        _merge_body,
        grid=(m // tm,),
        in_specs=[row(d), const(g), row(oa.shape[1]), row(ob.shape[1]), row(oc.shape[1]),
                  const(wm), const(wa), const(wb), const(wc), const(wo)],
        out_specs=row(d),
        out_shape=jax.ShapeDtypeStruct((m, d), F32),
        compiler_params=_cparams(("parallel",)),
        name="merge",
    )(x2, g, oa, ob, oc, wm, wa, wb, wc, wo)


def _rope_tables(seq):
    half = HEAD_DIM // 2
    inv = ROPE_THETA ** (-jnp.arange(half, dtype=F32) / half)
    ang = jnp.arange(seq, dtype=F32)[:, None] * inv[None, :]
    lane = np.arange(LANES)
    ang = ang[:, lane % half]
    second = jnp.asarray((lane % HEAD_DIM) >= half)[None, :]
    sin = jnp.sin(ang)
    return jnp.cos(ang), jnp.where(second, sin, 0.0), jnp.where(second, 0.0, -sin)


def _overlap(seq):
    ncp = seq // CMP_STRIDE
    nc = (seq - CMP_BLOCK) // CMP_STRIDE + 1
    ns = seq // SLC_BLOCK
    cs = np.arange(ncp) * CMP_STRIDE
    ss = np.arange(ns) * SLC_BLOCK
    ov = (cs[:, None] < ss[None, :] + SLC_BLOCK) & (ss[None, :] <= cs[:, None] + CMP_BLOCK - 1)
    ov &= (np.arange(ncp) < nc)[:, None]
    out = np.zeros((ncp, LANES), np.float32)
    out[:, :ns] = ov
    return jnp.asarray(out)


def _block_onehot(seq, block):
    return jnp.asarray(np.arange(seq)[:, None] // block == np.arange(LANES)[None, :], BF16)


def kernel(x, ffn1_norm, ffn1_w_gate, ffn1_w_up, ffn1_w_down, mix_norm, w_in, cmp_pos_k, cmp_w1_k, cmp_w2_k, cmp_pos_v, cmp_w1_v, cmp_w2_v, w_branch_a, w_branch_b, w_branch_c, w_out, ffn2_norm, ffn2_w_gate, ffn2_w_up, ffn2_w_down, final_norm):
    batch, seq, d = x.shape
    depth = w_in.shape[0]
    assert seq % max(KC, PROJ_TM) == 0 and WINDOW + NSA_TQ <= seq <= SLC_BLOCK * LANES and w_in.shape[2] == _N_IN
    nc = (seq - CMP_BLOCK) // CMP_STRIDE + 1
    ncp = seq // CMP_STRIDE

    perm = _in_perm()
    w_perm = jnp.where(jnp.asarray(perm >= 0)[None, None, :],
                       jnp.take(w_in, jnp.asarray(np.maximum(perm, 0)), axis=2), 0.0).astype(BF16)
    w_mg = w_in[:, :, _O_MG:].astype(BF16)
    half_rows = CMP_BLOCK * HEAD_DIM // 2
    w1 = jnp.stack([cmp_w1_k, cmp_w1_v], axis=1)
    w1 = jnp.concatenate([w1[:, :, :half_rows], w1[:, :, half_rows:]], axis=-1)
    w2 = jnp.pad(jnp.stack([cmp_w2_k, cmp_w2_v], axis=1), ((0, 0), (0, 0), (0, 0), (0, LANES - HEAD_DIM)))
    pos = jnp.stack([cmp_pos_k, cmp_pos_v], axis=1).reshape(depth, 2, 2, half_rows)
    pos = jnp.pad(pos, ((0, 0), (0, 0), (0, 6), (0, 0)))
    cos, sa, sb = _rope_tables(seq)
    overlap = _overlap(seq)
    hot_b = _block_onehot(seq, MOBA_BLOCK)
    hot_s = _block_onehot(seq, SLC_BLOCK)
    bf = lambda a: a.astype(BF16)
    row = lambda a: a.reshape(1, d)

    x2 = x.reshape(batch * seq, d)
    for l in range(depth):
        x2 = _ffn(x2, row(ffn1_norm[l]), bf(ffn1_w_gate[l]), bf(ffn1_w_up[l]), bf(ffn1_w_down[l]),
                  row(final_norm), False)
        p = _inproj(x2, row(mix_norm[l]), w_perm[l], cos, sa, sb, seq)
        xc = p["ccmp"].reshape(batch, seq, 4, HEAD_DIM).transpose(0, 2, 1, 3).reshape(batch, 4, ncp, 1024)
        kc, vc = _compress(xc, w1[l], w2[l], pos[l], nc)
        o_a = _dsa(p, batch, seq)
        o_b = _moba(p, hot_b, batch, seq)
        o_c = _nsa(p, kc[:, :C_GROUPS], vc[:, C_GROUPS:], overlap, hot_s, batch, seq)
        flat = lambda a: a.reshape(batch * seq, a.shape[-1])
        x2 = _merge(x2, row(mix_norm[l]), flat(o_a), flat(o_b), flat(o_c), w_mg[l],
                    bf(w_branch_a[l]), bf(w_branch_b[l]), bf(w_branch_c[l]), bf(w_out[l]))
        x2 = _ffn(x2, row(ffn2_norm[l]), bf(ffn2_w_gate[l]), bf(ffn2_w_up[l]), bf(ffn2_w_down[l]),
                  row(final_norm), l == depth - 1)
    return x2.reshape(batch, seq, d)
```

```python
import functools
import math

import numpy as np
import jax
import jax.numpy as jnp
from jax import lax
from jax.experimental import pallas as pl
from jax.experimental.pallas import tpu as pltpu

HEAD_DIM = 64
ROPE_THETA = 10000.0
NORM_EPS = 1e-6
A_HEADS = 4
IDX_HEADS = 4
IDX_DIM = 64
DSA_TOPK_MAX = 256
B_HEADS = 4
MOBA_BLOCK = 256
MOBA_TOPK = 3
C_HEADS = 8
C_GROUPS = 2
C_REP = C_HEADS // C_GROUPS
CMP_BLOCK = 32
CMP_STRIDE = 16
CMP_HIDDEN = 128
SLC_BLOCK = 64
SLC_TOPN = 16
WINDOW = 512
FORCE_SCORE = 1e30

LANES = 128
VMEM_LIMIT = 56 * 1024 * 1024

DSA_TQ = 256
MOBA_TQ = 256
NSA_TQ = 256
KC = 512
COUNT_KEYS = 128
FFN_TM = 512
PROJ_TM = 512
MERGE_TM = 512

NEG_BIG = -1e30
QSCALE = HEAD_DIM ** -0.5 * math.log2(math.e)
F32 = jnp.float32
BF16 = jnp.bfloat16
HIGHEST = lax.Precision.HIGHEST

_O_AQ = 0
_O_AK = 256
_O_AV = 320
_O_IQ = 384
_O_IK = 640
_O_IW = 704
_O_BQ = 708
_O_BK = 964
_O_BV = 1220
_O_CQ = 1476
_O_CKV = 1988
_O_CG = 2756
_O_MG = 2780
_N_IN = 5852

P_AQ = 0
P_AK = 256
P_IK = 320
P_IQ = 384
P_BQ = 640
P_BK = 896
P_CKS = 1152
P_CKW = 1280
P_CQ = 1408
P_ROPE_END = 1920
P_BV = 1920
P_KCMP = 2176
P_VS = 2432
P_VW = 2560
P_AV = 2688
P_MISC = 2752
P_COLS = 2816
MISC_CG = 0
MISC_IW = 24


def _in_perm():
    perm = -np.ones((P_COLS,), np.int64)

    def put(dst, src, n):
        perm[dst:dst + n] = np.arange(src, src + n)

    put(P_AQ, _O_AQ, 256)
    put(P_AK, _O_AK, 64)
    put(P_IK, _O_IK, 64)
    put(P_IQ, _O_IQ, 256)
    put(P_BQ, _O_BQ, 256)
    put(P_BK, _O_BK, 256)
    ckv = lambda s, g: _O_CKV + (s * C_GROUPS + g) * HEAD_DIM
    for g in range(C_GROUPS):
        put(P_CKS + 64 * g, ckv(2, g), 64)
        put(P_CKW + 64 * g, ckv(4, g), 64)
        put(P_KCMP + 64 * g, ckv(0, g), 64)
        put(P_KCMP + 128 + 64 * g, ckv(1, g), 64)
        put(P_VS + 64 * g, ckv(3, g), 64)
        put(P_VW + 64 * g, ckv(5, g), 64)
    put(P_CQ, _O_CQ, 512)
    put(P_BV, _O_BV, 256)
    put(P_AV, _O_AV, 64)
    put(P_MISC + MISC_CG, _O_CG, 24)
    put(P_MISC + MISC_IW, _O_IW, 4)
    return perm


def _cparams(sem):
    return pltpu.CompilerParams(dimension_semantics=sem, vmem_limit_bytes=VMEM_LIMIT)


def _rms(x, g):
    y = x * lax.rsqrt(jnp.mean(x * x, axis=-1, keepdims=True) + NORM_EPS)
    return y * g


def _dot_nt(a, b, precision=None):
    return lax.dot_general(a, b, (((1,), (1,)), ((), ())), precision=precision,
                           preferred_element_type=F32)


def _ffn_body(x_ref, g_ref, wg_ref, wu_ref, wd_ref, fg_ref, o_ref, h_scr, acc_scr, *, final_norm):
    f = pl.program_id(1)

    @pl.when(f == 0)
    def _():
        h_scr[...] = _rms(x_ref[...], g_ref[...]).astype(BF16)
        acc_scr[...] = jnp.zeros_like(acc_scr)

    h = h_scr[...]
    a = jnp.dot(h, wg_ref[...], preferred_element_type=F32)
    u = jnp.dot(h, wu_ref[...], preferred_element_type=F32)
    act = (a * jax.nn.sigmoid(a) * u).astype(BF16)
    acc_scr[...] += jnp.dot(act, wd_ref[...], preferred_element_type=F32)

    @pl.when(f == pl.num_programs(1) - 1)
    def _():
        y = x_ref[...] + 0.5 * acc_scr[...]
        if final_norm:
            y = _rms(y, fg_ref[...])
        o_ref[...] = y


def _ffn(x2, g, wg, wu, wd, fg, final_norm):
    m, d = x2.shape
    dff = wg.shape[1]
    tf = dff // 2 if (dff // 2) % LANES == 0 else dff
    tm = FFN_TM
    return pl.pallas_call(
        functools.partial(_ffn_body, final_norm=final_norm),
        grid=(m // tm, dff // tf),
        in_specs=[
            pl.BlockSpec((tm, d), lambda i, f: (i, 0)),
            pl.BlockSpec((1, d), lambda i, f: (0, 0)),
            pl.BlockSpec((d, tf), lambda i, f: (0, f)),
            pl.BlockSpec((d, tf), lambda i, f: (0, f)),
            pl.BlockSpec((tf, d), lambda i, f: (f, 0)),
            pl.BlockSpec((1, d), lambda i, f: (0, 0)),
        ],
        out_specs=pl.BlockSpec((tm, d), lambda i, f: (i, 0)),
        out_shape=jax.ShapeDtypeStruct((m, d), F32),
        scratch_shapes=[pltpu.VMEM((tm, d), BF16), pltpu.VMEM((tm, d), F32)],
        compiler_params=_cparams(("parallel", "arbitrary")),
        name="ffn",
    )(x2, g, wg, wu, wd, fg)


def _lane_iota(shape):
    return lax.broadcasted_iota(jnp.int32, shape, len(shape) - 1)


def _split_hi_lo(x):
    hi = x.astype(BF16).astype(F32)
    return hi, x - hi


def _inproj_body(x_ref, g_ref, w_ref, cos_ref, sa_ref, sb_ref,
                 aq_ref, akv_ref, iq_ref, ik_ref, misc_ref, bq_ref, bqp_ref, bkv_ref, km_ref,
                 cqn_ref, cqr_ref, cslc_ref, cwin_ref, ccmp_ref):
    tm = x_ref.shape[0]
    h = _rms(x_ref[...], g_ref[...]).astype(BF16)
    cos, sa, sb = cos_ref[...], sa_ref[...], sb_ref[...]
    lane = _lane_iota((tm, LANES))
    low = lane < HEAD_DIM

    raw = []
    rot = []
    for j in range(P_COLS // 256):
        z = jnp.dot(h, w_ref[:, 256 * j:256 * (j + 1)], preferred_element_type=F32)
        for half in range(2):
            p = z[:, LANES * half:LANES * (half + 1)]
            raw.append(p)
            if LANES * len(raw) <= P_ROPE_END:
                rot.append(p * cos + pltpu.roll(p, 32, axis=1) * sa + pltpu.roll(p, 96, axis=1) * sb)

    def pick(pieces, col):
        p = pieces[col // LANES]
        return pltpu.roll(p, HEAD_DIM, axis=1) if col % LANES else p

    def join(lo, hi=None):
        if hi is None:
            return jnp.where(low, lo, 0.0)
        return jnp.where(low, lo, pltpu.roll(hi, HEAD_DIM, axis=1))

    scale = QSCALE

    for hh in range(A_HEADS):
        aq_ref[:, LANES * hh:LANES * (hh + 1)] = join(pick(rot, P_AQ + 64 * hh) * scale).astype(BF16)
        qhi, qlo = _split_hi_lo(pick(rot, P_IQ + 64 * hh))
        iq_ref[:, 256 * hh:256 * hh + LANES] = join(qhi, qlo).astype(BF16)
        iq_ref[:, 256 * hh + LANES:256 * (hh + 1)] = join(qhi).astype(BF16)
    akv_ref[...] = join(pick(rot, P_AK), pick(raw, P_AV)).astype(BF16)
    khi, klo = _split_hi_lo(pick(rot, P_IK))
    ik_ref[:, :LANES] = join(khi, khi).astype(BF16)
    ik_ref[:, LANES:] = join(klo).astype(BF16)
    misc_ref[...] = pick(raw, P_MISC)

    bq_ref[:, :LANES] = rot[P_BQ // LANES]
    bq_ref[:, LANES:] = rot[P_BQ // LANES + 1]
    for hh in range(B_HEADS):
        bqp_ref[:, LANES * hh:LANES * (hh + 1)] = join(pick(rot, P_BQ + 64 * hh) * scale).astype(BF16)
        bkv_ref[:, LANES * hh:LANES * (hh + 1)] = join(pick(rot, P_BK + 64 * hh),
                                                      pick(raw, P_BV + 64 * hh)).astype(BF16)
    nblk = tm // MOBA_BLOCK
    km_ref[...] = jnp.zeros_like(km_ref)
    for half in range(2):
        kp = rot[P_BK // LANES + half]
        for b in range(nblk):
            km_ref[0, b:b + 1, LANES * half:LANES * (half + 1)] = jnp.mean(
                kp[MOBA_BLOCK * b:MOBA_BLOCK * (b + 1)], axis=0, keepdims=True)

    for hh in range(C_HEADS):
        nhi, nlo = _split_hi_lo(pick(raw, P_CQ + 64 * hh) * scale)
        cqn_ref[:, 256 * hh:256 * hh + LANES] = join(nhi, nlo).astype(BF16)
        cqn_ref[:, 256 * hh + LANES:256 * (hh + 1)] = join(nhi).astype(BF16)
        cqr_ref[:, LANES * hh:LANES * (hh + 1)] = join(pick(rot, P_CQ + 64 * hh) * scale).astype(BF16)
    for g in range(C_GROUPS):
        cslc_ref[:, LANES * g:LANES * (g + 1)] = join(pick(rot, P_CKS + 64 * g),
                                                     pick(raw, P_VS + 64 * g)).astype(BF16)
        cwin_ref[:, LANES * g:LANES * (g + 1)] = join(pick(rot, P_CKW + 64 * g),
                                                     pick(raw, P_VW + 64 * g)).astype(BF16)
    ccmp_ref[:, :LANES] = raw[P_KCMP // LANES]
    ccmp_ref[:, LANES:] = raw[P_KCMP // LANES + 1]


def _inproj(x2, g, w, cos, sa, sb, seq):
    m, d = x2.shape
    tm = PROJ_TM
    nt = seq // tm
    row = lambda c: pl.BlockSpec((tm, c), lambda i: (i, 0))
    tab = pl.BlockSpec((tm, LANES), lambda i: (i % nt, 0))
    outs = [
        ("aq", 4 * LANES, BF16), ("akv", LANES, BF16), ("iq", 4 * 256, BF16), ("ik", 256, BF16),
        ("misc", LANES, F32), ("bq", 256, F32), ("bqp", 4 * LANES, BF16), ("bkv", 4 * LANES, BF16),
        ("km", None, F32),
        ("cqn", 8 * 256, BF16), ("cqr", 8 * LANES, BF16), ("cslc", 2 * LANES, BF16),
        ("cwin", 2 * LANES, BF16), ("ccmp", 256, F32),
    ]
    out_specs, out_shape = [], []
    for name, c, dt in outs:
        if name == "km":
            out_specs.append(pl.BlockSpec((1, 8, 256), lambda i: (i, 0, 0)))
            out_shape.append(jax.ShapeDtypeStruct((m // tm, 8, 256), dt))
        else:
            out_specs.append(row(c))
            out_shape.append(jax.ShapeDtypeStruct((m, c), dt))
    res = pl.pallas_call(
        _inproj_body,
        grid=(m // tm,),
        in_specs=[row(d), pl.BlockSpec((1, d), lambda i: (0, 0)),
                  pl.BlockSpec((d, P_COLS), lambda i: (0, 0)), tab, tab, tab],
        out_specs=out_specs,
        out_shape=out_shape,
        compiler_params=_cparams(("parallel",)),
        name="inproj",
    )(x2, g, w, cos, sa, sb)
    return dict(zip([o[0] for o in outs], res))


def _compress_body(x_ref, w1_ref, w2_ref, pos_ref, ok_ref, ov_ref, *, n_valid):
    x = x_ref[0, 0]
    w1 = w1_ref[0]
    pre = jnp.dot(x, w1, precision=HIGHEST, preferred_element_type=F32)
    pp = jnp.dot(pos_ref[0], w1, precision=HIGHEST, preferred_element_type=F32)
    posb = pp[0:1, :CMP_HIDDEN] + pp[1:2, CMP_HIDDEN:]
    ncp = x.shape[0]
    upper = pre[:, :CMP_HIDDEN]
    lower_next = pltpu.roll(pre[:, CMP_HIDDEN:], ncp - 1, axis=0)
    hid = jax.nn.gelu(upper + lower_next + posb)
    out = jnp.dot(hid, w2_ref[0], precision=HIGHEST, preferred_element_type=F32)
    rows = lax.broadcasted_iota(jnp.int32, out.shape, 0)
    out = jnp.where(rows < n_valid, out, 0.0)
    low = _lane_iota(out.shape) < HEAD_DIM
    hi, lo = _split_hi_lo(out)
    ok_ref[0, 0, :, :LANES] = jnp.where(low, hi, pltpu.roll(hi, HEAD_DIM, axis=1)).astype(BF16)
    ok_ref[0, 0, :, LANES:] = lo.astype(BF16)
    ov_ref[0, 0] = pltpu.roll(out, HEAD_DIM, axis=1).astype(BF16)


def _compress(xc, w1, w2, pos, n_valid):
    b, four, ncp, _ = xc.shape
    return pl.pallas_call(
        functools.partial(_compress_body, n_valid=n_valid),
        grid=(b, four),
        in_specs=[
            pl.BlockSpec((1, 1, ncp, 1024), lambda i, j: (i, j, 0, 0)),
            pl.BlockSpec((1, 1024, 256), lambda i, j: (j // 2, 0, 0)),
            pl.BlockSpec((1, CMP_HIDDEN, LANES), lambda i, j: (j // 2, 0, 0)),
            pl.BlockSpec((1, 8, 1024), lambda i, j: (j // 2, 0, 0)),
        ],
        out_specs=[pl.BlockSpec((1, 1, ncp, 256), lambda i, j: (i, j, 0, 0)),
                   pl.BlockSpec((1, 1, ncp, LANES), lambda i, j: (i, j, 0, 0))],
        out_shape=[jax.ShapeDtypeStruct((b, four, ncp, 256), BF16),
                   jax.ShapeDtypeStruct((b, four, ncp, LANES), BF16)],
        compiler_params=_cparams(("parallel", "parallel")),
        name="compress",
    )(xc, w1, w2, pos)


def _softmax_step(carry, s3, pv):
    m_old, l_old, acc = carry
    heads, tq = s3.shape[0], s3.shape[1]
    m_new = jnp.maximum(m_old, jnp.max(s3, axis=-1, keepdims=True))
    p = jnp.exp2(s3 - m_new)
    alpha = jnp.exp2(m_old - m_new)
    l_new = alpha * l_old + jnp.sum(p, axis=-1, keepdims=True)
    acc = alpha.reshape(heads * tq, 1) * acc + pv(p.astype(BF16))
    return m_new, l_new, acc


def _online_init(heads, tq):
    return (jnp.full((heads, tq, 1), NEG_BIG, F32), jnp.zeros((heads, tq, 1), F32),
            jnp.zeros((heads * tq, LANES), F32))


def _online_finish(carry):
    _, l, acc = carry
    return acc / l.reshape(acc.shape[0], 1)


def _pv_shared(kv):
    return lambda p: jnp.dot(p.reshape(p.shape[0] * p.shape[1], p.shape[2]), kv, preferred_element_type=F32)


def _store_heads(o_ref, o, heads, col0=0):
    tq = o.shape[0] // heads
    low = _lane_iota((tq, LANES)) < HEAD_DIM
    for pair in range(heads // 2):
        even = o[tq * (2 * pair):tq * (2 * pair + 1)]
        odd = o[tq * (2 * pair + 1):tq * (2 * pair + 2)]
        piece = jnp.where(low, pltpu.roll(even, HEAD_DIM, axis=1), odd)
        o_ref[0, :, col0 + LANES * pair:col0 + LANES * (pair + 1)] = piece.astype(o_ref.dtype)


def _topn_mask(score, n_pick, axis=-1):
    axis = axis % score.ndim
    idx = lax.broadcasted_iota(jnp.int32, score.shape, axis)
    width = score.shape[axis]
    sel = jnp.zeros(score.shape, F32)
    for _ in range(n_pick):
        m = jnp.max(score, axis=axis, keepdims=True)
        first = jnp.min(jnp.where(score == m, idx, width), axis=axis, keepdims=True)
        hit = idx == first
        sel = jnp.where(hit, 1.0, sel)
        score = jnp.where(hit, -jnp.inf, score)
    return sel


_NEG_INF_KEY = -2139095041


def _key_of(x):
    bits = lax.bitcast_convert_type(x, jnp.int32)
    return jnp.where(bits < 0, bits ^ jnp.int32(0x7FFFFFFF), bits)


def _float_of(key):
    return lax.bitcast_convert_type(jnp.where(key < 0, key ^ jnp.int32(0x7FFFFFFF), key), F32)


def _dsa_body(iq_ref, ik_ref, misc_ref, aq_ref, akv_ref, o_ref, key_scr, *, topk, seq):
    TQ = DSA_TQ
    c = pl.program_id(1)
    q0 = c * TQ
    nchunk = (q0 + TQ + KC - 1) // KC
    qpos = q0 + _lane_iota((1, TQ))
    krow = lax.broadcasted_iota(jnp.int32, (KC, 1), 0)
    srow = lax.broadcasted_iota(jnp.int32, (COUNT_KEYS, 1), 0)
    idx_scale = (IDX_HEADS * IDX_DIM) ** -0.5

    iq = jnp.concatenate([iq_ref[0, :, 256 * h:256 * (h + 1)] for h in range(IDX_HEADS)], axis=0)
    misc_t = misc_ref[0].T
    iw = [misc_t[MISC_IW + h:MISC_IW + h + 1] for h in range(IDX_HEADS)]

    def score_body(j, carry):
        mx, mn = carry
        ks = pl.multiple_of(j * KC, KC)
        lg = jnp.maximum(_dot_nt(ik_ref[0, pl.ds(ks, KC), :], iq), 0.0)
        sc = iw[0] * lg[:, 0:TQ]
        for h in range(1, IDX_HEADS):
            sc = sc + iw[h] * lg[:, TQ * h:TQ * (h + 1)]
        sc = sc * idx_scale
        sc = jnp.where(sc == 0.0, 0.0, sc)
        part = sc.reshape(KC // 8, 8, TQ)
        mx, mn = jnp.maximum(mx, jnp.max(part, axis=0)), jnp.minimum(mn, jnp.min(part, axis=0))
        key_scr[j] = _key_of(jnp.where(ks + krow <= qpos, sc, -jnp.inf))
        return mx, mn

    mx, mn = lax.fori_loop(0, nchunk, score_body,
                           (jnp.full((8, TQ), -jnp.inf, F32), jnp.full((8, TQ), jnp.inf, F32)))

    def count(*hits):
        def body(j, accs):
            accs = list(accs)
            for g in range(KC // COUNT_KEYS):
                k = key_scr[j, COUNT_KEYS * g:COUNT_KEYS * (g + 1), :]
                for i, hit in enumerate(hits):
                    one = jnp.where(hit(k, j * KC + COUNT_KEYS * g), 1.0, 0.0)
                    accs[i] = accs[i] + jnp.sum(one.reshape(COUNT_KEYS // 8, 8, TQ), axis=0)
            return tuple(accs)
        accs = lax.fori_loop(0, nchunk, body, tuple(jnp.zeros((8, TQ), F32) for _ in hits))
        out = tuple(jnp.sum(a, axis=0, keepdims=True) for a in accs)
        return out if len(out) > 1 else out[0]

    n_fin = (qpos + 1).astype(F32)
    crowded = n_fin > topk
    c_pos, c_nn = count(lambda k, p0: k > 0, lambda k, p0: k >= 0)
    positive = c_pos >= topk
    zero_tie = crowded & jnp.logical_not(positive) & (c_nn >= topk)
    lo0 = jnp.where(positive, 1, _key_of(jnp.min(mn, axis=0, keepdims=True)))
    hi0 = jnp.where(positive, _key_of(jnp.max(mx, axis=0, keepdims=True)) + 1, 0)
    c_lo0 = jnp.where(positive, c_pos, n_fin)
    c_hi0 = jnp.where(positive, 0.0, c_nn)
    lo0 = jnp.where(zero_tie, 0, lo0)
    c_hi0 = jnp.where(zero_tie, c_pos, c_hi0)
    found0 = positive & (c_pos == topk)
    act0 = jnp.where(crowded & jnp.logical_not(zero_tie | found0) & (lo0 + 1 < hi0), 1.0, 0.0)
    below0 = jnp.where(found0, 0, _NEG_INF_KEY)
    log_target = math.log(topk - 0.5)

    def search_round(st):
        it, _, lo, hi, c_lo, c_hi, act, found, below = st
        lo_f, hi_f = _float_of(lo), _float_of(hi)
        la, lb = jnp.log(c_lo), jnp.log(jnp.maximum(c_hi, 0.5))
        frac = jnp.clip((la - log_target) / (la - lb), 0.02, 0.98)
        phase = it % 3
        guess = jnp.where(phase == 0, lo_f + (hi_f - lo_f) * frac, (lo_f + hi_f) * 0.5)
        mid = (lo >> 1) + (hi >> 1) + (lo & hi & 1)
        cand = jnp.clip(jnp.where(phase == 2, mid, _key_of(guess)), lo + 1, hi - 1)
        c = count(lambda k, p0: k >= cand)
        live = act > 0.5
        exact = live & (c == topk)
        up = live & (c > topk)
        dn = live & (c < topk)
        below = jnp.where(exact, cand - 1, below)
        found = jnp.where(exact, 1.0, found)
        lo, c_lo = jnp.where(up, cand, lo), jnp.where(up, c, c_lo)
        hi, c_hi = jnp.where(dn, cand, hi), jnp.where(dn, c, c_hi)
        act = jnp.where(live & jnp.logical_not(exact) & (lo + 1 < hi), 1.0, 0.0)
        return it + 1, jnp.sum(act), lo, hi, c_lo, c_hi, act, found, below

    st = lax.while_loop(lambda st: (st[0] < 100) & (st[1] > 0.5), search_round,
                        (jnp.int32(0), jnp.sum(act0), lo0, hi0, c_lo0, c_hi0, act0,
                         jnp.where(found0, 1.0, 0.0), below0))
    _, _, lo, _, _, c_hi, _, found, below = st

    tied = crowded & (found < 0.5)
    thr = jnp.where(tied, lo, below)
    need = topk - c_hi
    any_tie = jnp.max(jnp.where(tied, 1.0, 0.0)) > 0.5

    def tie_cut():
        def pos_body(i, cut):
            cand = cut + lax.shift_left(jnp.int32(1), (seq.bit_length() - 1) - i)
            cnt = count(lambda k, p0: (k == thr) & (p0 + srow < cand))
            return jnp.where(cnt < need, cand, cut)
        return lax.fori_loop(0, seq.bit_length(), pos_body, jnp.zeros((1, TQ), jnp.int32))

    cut = lax.cond(any_tie, tie_cut, lambda: jnp.full((1, TQ), -1, jnp.int32))
    cut = jnp.where(tied, cut, -1)

    q = jnp.concatenate([aq_ref[0, :, LANES * h:LANES * (h + 1)] for h in range(A_HEADS)], axis=0)

    def att_body(j, carry):
        ks = pl.multiple_of(j * KC, KC)
        kv = akv_ref[0, pl.ds(ks, KC), :]
        k = key_scr[j]
        taken = (k > thr) | ((k == thr) & (ks + krow <= cut))
        bias = jnp.where(taken, 0.0, NEG_BIG).T
        s3 = _dot_nt(q, kv).reshape(A_HEADS, TQ, KC) + bias[None]
        return _softmax_step(carry, s3, _pv_shared(kv))

    carry = lax.fori_loop(0, nchunk, att_body, _online_init(A_HEADS, TQ))
    _store_heads(o_ref, _online_finish(carry), A_HEADS)


def _dsa(p, batch, seq):
    topk = min(DSA_TOPK_MAX, seq // 4)
    r3 = lambda a: a.reshape(batch, seq, a.shape[-1])
    qblk = lambda c: pl.BlockSpec((1, DSA_TQ, c), lambda b, i: (b, i, 0))
    full = lambda c: pl.BlockSpec((1, seq, c), lambda b, i: (b, 0, 0))
    return pl.pallas_call(
        functools.partial(_dsa_body, topk=topk, seq=seq),
        grid=(batch, seq // DSA_TQ),
        in_specs=[qblk(4 * 256), full(256), qblk(LANES), qblk(4 * LANES), full(LANES)],
        out_specs=qblk(A_HEADS * HEAD_DIM),
        out_shape=jax.ShapeDtypeStruct((batch, seq, A_HEADS * HEAD_DIM), BF16),
        scratch_shapes=[pltpu.VMEM((seq // KC, KC, DSA_TQ), jnp.int32)],
        compiler_params=_cparams(("parallel", "arbitrary")),
        name="dsa",
    )(r3(p["iq"]), r3(p["ik"]), r3(p["misc"]), r3(p["aq"]), r3(p["akv"]))


def _pv_per_head(kvs):
    return lambda p: jnp.concatenate(
        [jnp.dot(p[h], kvs[h], preferred_element_type=F32) for h in range(len(kvs))], axis=0)


def _moba_body(bq_ref, bqp_ref, bkv_ref, km_ref, hot_ref, o_ref, *, nbp):
    TQ = MOBA_TQ
    c = pl.program_id(1)
    q0 = c * TQ
    own = q0 // MOBA_BLOCK
    row = q0 + lax.broadcasted_iota(jnp.int32, (TQ, 1), 0)
    col = _lane_iota((1, MOBA_BLOCK))
    ks_own = pl.multiple_of(own * MOBA_BLOCK, MOBA_BLOCK)
    heads = range(B_HEADS)
    kv_at = lambda ks, h: bkv_ref[0, pl.ds(ks, MOBA_BLOCK), LANES * h:LANES * (h + 1)]

    blk = lax.broadcasted_iota(jnp.int32, (nbp, 1), 0)
    gate = jnp.concatenate(
        [_dot_nt(km_ref[0, :nbp, HEAD_DIM * h:HEAD_DIM * (h + 1)], bq_ref[0, :, HEAD_DIM * h:HEAD_DIM * (h + 1)],
                 precision=HIGHEST) for h in heads], axis=1)
    picked = (blk < own) & (_topn_mask(jnp.where(blk < own, gate, -jnp.inf), MOBA_TOPK, axis=0) > 0.0)
    bias_t = jnp.where(picked, 0.0, NEG_BIG)
    fill = jnp.zeros((LANES - nbp, TQ), F32)

    q_aug, s_own = [], []
    for h in heads:
        bias = jnp.concatenate([bias_t[:, TQ * h:TQ * (h + 1)], fill], axis=0).T
        q = bqp_ref[0, :, LANES * h:LANES * (h + 1)]
        q_aug.append(jnp.concatenate([q, bias.astype(BF16)], axis=1))
        s_own.append(jnp.where(ks_own + col <= row, _dot_nt(q, kv_at(ks_own, h)), NEG_BIG))

    stack = lambda xs: jnp.concatenate(xs, axis=0).reshape(B_HEADS, TQ, MOBA_BLOCK)
    carry = _softmax_step(_online_init(B_HEADS, TQ), stack(s_own),
                          _pv_per_head([kv_at(ks_own, h) for h in heads]))

    def scores(j):
        ks = pl.multiple_of(j * MOBA_BLOCK, MOBA_BLOCK)
        hot = hot_ref[pl.ds(ks, MOBA_BLOCK), :]
        return stack([_dot_nt(q_aug[h], jnp.concatenate([kv_at(ks, h), hot], axis=1)) for h in heads])

    def body(j, carry):
        state, s_cur = carry
        s_next = scores(jnp.minimum(j + 1, own - 1))
        ks = pl.multiple_of(j * MOBA_BLOCK, MOBA_BLOCK)
        return _softmax_step(state, s_cur, _pv_per_head([kv_at(ks, h) for h in heads])), s_next

    carry, _ = lax.fori_loop(0, own, body, (carry, scores(0)))
    _store_heads(o_ref, _online_finish(carry), B_HEADS)


def _moba(p, hot, batch, seq):
    nb = seq // MOBA_BLOCK
    r3 = lambda a: a.reshape(batch, seq, a.shape[-1])
    km = p["km"][:, :PROJ_TM // MOBA_BLOCK].reshape(batch, nb, 256)
    km = jnp.pad(km, ((0, 0), (0, LANES - nb), (0, 0)))
    qblk = lambda c: pl.BlockSpec((1, MOBA_TQ, c), lambda b, i: (b, i, 0))
    return pl.pallas_call(
        functools.partial(_moba_body, nbp=min(LANES, -(-nb // 8) * 8)),
        grid=(batch, seq // MOBA_TQ),
        in_specs=[qblk(256), qblk(4 * LANES),
                  pl.BlockSpec((1, seq, 4 * LANES), lambda b, i: (b, 0, 0)),
                  pl.BlockSpec((1, LANES, 256), lambda b, i: (b, 0, 0)),
                  pl.BlockSpec((seq, LANES), lambda b, i: (0, 0))],
        out_specs=qblk(B_HEADS * HEAD_DIM),
        out_shape=jax.ShapeDtypeStruct((batch, seq, B_HEADS * HEAD_DIM), BF16),
        compiler_params=_cparams(("parallel", "arbitrary")),
        name="moba",
    )(r3(p["bq"]), r3(p["bqp"]), r3(p["bkv"]), km, hot)


def _nsa_body(cqn_ref, cqr_ref, misc_ref, kc_ref, vc_ref, cslc_ref, cwin_ref, ov_ref, hot_ref, o_ref,
              *, ncp, n_sel):
    TQ = NSA_TQ
    c = pl.program_id(1)
    q0 = c * TQ
    last = (q0 + TQ - 1) // KC
    ks_last = pl.multiple_of(last * KC, KC)
    row = q0 + lax.broadcasted_iota(jnp.int32, (TQ, 1), 0)
    col = _lane_iota((1, KC))
    gates = jax.nn.sigmoid(misc_ref[0])
    cmp_vis = _lane_iota((1, ncp)) * CMP_STRIDE + (CMP_BLOCK - 1) <= row
    wstart = pl.multiple_of(jnp.maximum(q0 - WINDOW, 0), TQ)
    wlen = WINDOW + TQ
    wdiff = row - (wstart + _lane_iota((1, wlen)))
    wmask = (wdiff >= 0) & (wdiff < WINDOW)

    o_cmp, imp_t = [], []
    for g in range(C_GROUPS):
        qn = jnp.concatenate([cqn_ref[0, :, 256 * (C_REP * g + r):256 * (C_REP * g + r + 1)]
                              for r in range(C_REP)], axis=0)
        s3 = _dot_nt(qn, kc_ref[0, g]).reshape(C_REP, TQ, ncp)
        m = jnp.max(jnp.where(cmp_vis[None], s3, -jnp.inf), axis=-1, keepdims=True)
        m = jnp.where(m > -jnp.inf, m, 0.0)
        pc = jnp.where(cmp_vis[None], jnp.exp2(s3 - m), 0.0)
        den = jnp.sum(pc, axis=-1, keepdims=True)
        pc = pc / jnp.where(den > 0, den, 1.0)
        o_cmp.append(jnp.dot(pc.reshape(C_REP * TQ, ncp).astype(BF16), vc_ref[0, g],
                             preferred_element_type=F32))
        psum = pc[0]
        for r in range(1, C_REP):
            psum = psum + pc[r]
        hi = psum.astype(BF16)
        rest = psum - hi.astype(F32)
        mid = rest.astype(BF16)
        low = (rest - mid.astype(F32)).astype(BF16)
        imp = jnp.dot(jnp.concatenate([hi, mid, low], axis=1), ov_ref[...], preferred_element_type=F32)
        imp_t.append(imp.T)

    blk = lax.broadcasted_iota(jnp.int32, (LANES, 1), 0)
    jq = (q0 + _lane_iota((1, C_GROUPS * TQ)) % TQ) // SLC_BLOCK
    adm = blk <= jq
    forced = adm & ((blk == 0) | (blk == jq) | (blk == jq - 1))
    score = jnp.where(forced, FORCE_SCORE, jnp.where(adm, jnp.concatenate(imp_t, axis=1), -jnp.inf))
    picked = adm & (_topn_mask(score, n_sel, axis=0) > 0.0)
    bias_t = jnp.where(picked, 0.0, NEG_BIG)

    for g in range(C_GROUPS):
        o_c = o_cmp[g]
        bias = bias_t[:, TQ * g:TQ * (g + 1)].T.astype(BF16)

        qr = jnp.concatenate([cqr_ref[0, :, LANES * (C_REP * g + r):LANES * (C_REP * g + r + 1)]
                              for r in range(C_REP)], axis=0)
        qa = jnp.concatenate([qr, jnp.concatenate([bias] * C_REP, axis=0)], axis=1)
        kv_at = lambda j: cslc_ref[0, pl.ds(pl.multiple_of(j * KC, KC), KC), LANES * g:LANES * (g + 1)]
        slc_scores = lambda j: _dot_nt(qa, jnp.concatenate(
            [kv_at(j), hot_ref[pl.ds(pl.multiple_of(j * KC, KC), KC), :]], axis=1))

        def slc_body(j, carry):
            return _softmax_step(carry, slc_scores(j).reshape(C_REP, TQ, KC), _pv_shared(kv_at(j)))

        carry = lax.fori_loop(0, last, slc_body, _online_init(C_REP, TQ))
        s3 = jnp.where((ks_last + col <= row)[None], slc_scores(last).reshape(C_REP, TQ, KC), NEG_BIG)
        o_s = _online_finish(_softmax_step(carry, s3, _pv_shared(kv_at(last))))

        kvw = cwin_ref[0, pl.ds(wstart, wlen), LANES * g:LANES * (g + 1)]
        s3 = jnp.where(wmask[None], _dot_nt(qr, kvw).reshape(C_REP, TQ, wlen), NEG_BIG)
        o_w = _online_finish(_softmax_step(_online_init(C_REP, TQ), s3, _pv_shared(kvw)))

        outs = []
        for r in range(C_REP):
            hh = C_REP * g + r
            gcol = lambda j: gates[:, MISC_CG + 3 * hh + j:MISC_CG + 3 * hh + j + 1]
            rows = slice(TQ * r, TQ * (r + 1))
            outs.append(gcol(0) * o_c[rows] + gcol(1) * o_s[rows] + gcol(2) * o_w[rows])
        _store_heads(o_ref, jnp.concatenate(outs, axis=0), C_REP, col0=C_REP * HEAD_DIM * g)


def _nsa(p, kcmp, vcmp, overlap, hot, batch, seq):
    ncp = seq // CMP_STRIDE
    n_sel = min(SLC_TOPN, seq // SLC_BLOCK)
    r3 = lambda a: a.reshape(batch, seq, a.shape[-1])
    qblk = lambda c: pl.BlockSpec((1, NSA_TQ, c), lambda b, i: (b, i, 0))
    full = lambda c: pl.BlockSpec((1, seq, c), lambda b, i: (b, 0, 0))
    return pl.pallas_call(
        functools.partial(_nsa_body, ncp=ncp, n_sel=n_sel),
        grid=(batch, seq // NSA_TQ),
        in_specs=[qblk(8 * 256), qblk(8 * LANES), qblk(LANES),
                  pl.BlockSpec((1, C_GROUPS, ncp, 256), lambda b, i: (b, 0, 0, 0)),
                  pl.BlockSpec((1, C_GROUPS, ncp, LANES), lambda b, i: (b, 0, 0, 0)),
                  full(2 * LANES), full(2 * LANES),
                  pl.BlockSpec((3 * ncp, LANES), lambda b, i: (0, 0)),
                  pl.BlockSpec((seq, LANES), lambda b, i: (0, 0))],
        out_specs=qblk(C_HEADS * HEAD_DIM),
        out_shape=jax.ShapeDtypeStruct((batch, seq, C_HEADS * HEAD_DIM), BF16),
        compiler_params=_cparams(("parallel", "arbitrary")),
        name="nsa",
    )(r3(p["cqn"]), r3(p["cqr"]), r3(p["misc"]), kcmp, vcmp, r3(p["cslc"]), r3(p["cwin"]), overlap, hot)


def _merge_body(x_ref, g_ref, oa_ref, ob_ref, oc_ref, wm_ref, wa_ref, wb_ref, wc_ref, wo_ref, o_ref):
    x = x_ref[...]
    d = x.shape[1]
    h = _rms(x, g_ref[...]).astype(BF16)
    merged = None
    for i, (o_r, w_r) in enumerate(((oa_ref, wa_ref), (ob_ref, wb_ref), (oc_ref, wc_ref))):
        gate = jax.nn.sigmoid(jnp.dot(h, wm_ref[:, d * i:d * (i + 1)], preferred_element_type=F32))
        y = gate * jnp.dot(o_r[...], w_r[...], preferred_element_type=F32)
        merged = y if merged is None else merged + y
    o_ref[...] = x + jnp.dot(merged.astype(BF16), wo_ref[...], preferred_element_type=F32)


def _merge(x2, g, oa, ob, oc, wm, wa, wb, wc, wo):
    m, d = x2.shape
    tm = MERGE_TM
    row = lambda c: pl.BlockSpec((tm, c), lambda i: (i, 0))
    const = lambda a: pl.BlockSpec(a.shape, lambda i: (0, 0))
    return pl.pallas_call(
        _merge_body,
        grid=(m // tm,),
        in_specs=[row(d), const(g), row(oa.shape[1]), row(ob.shape[1]), row(oc.shape[1]),
                  const(wm), const(wa), const(wb), const(wc), const(wo)],
        out_specs=row(d),
        out_shape=jax.ShapeDtypeStruct((m, d), F32),
        compiler_params=_cparams(("parallel",)),
        name="merge",
    )(x2, g, oa, ob, oc, wm, wa, wb, wc, wo)


def _rope_tables(seq):
    half = HEAD_DIM // 2
    inv = ROPE_THETA ** (-jnp.arange(half, dtype=F32) / half)
    ang = jnp.arange(seq, dtype=F32)[:, None] * inv[None, :]
    lane = np.arange(LANES)
    ang = ang[:, lane % half]
    second = jnp.asarray((lane % HEAD_DIM) >= half)[None, :]
    sin = jnp.sin(ang)
    return jnp.cos(ang), jnp.where(second, sin, 0.0), jnp.where(second, 0.0, -sin)


def _overlap(seq):
    ncp = seq // CMP_STRIDE
    nc = (seq - CMP_BLOCK) // CMP_STRIDE + 1
    ns = seq // SLC_BLOCK
    cs = np.arange(ncp) * CMP_STRIDE
    ss = np.arange(ns) * SLC_BLOCK
    ov = (cs[:, None] < ss[None, :] + SLC_BLOCK) & (ss[None, :] <= cs[:, None] + CMP_BLOCK - 1)
    ov &= (np.arange(ncp) < nc)[:, None]
    out = np.zeros((ncp, LANES), np.float32)
    out[:, :ns] = ov
    return jnp.asarray(np.tile(out, (3, 1)), BF16)


def _block_onehot(seq, block):
    return jnp.asarray(np.arange(seq)[:, None] // block == np.arange(LANES)[None, :], BF16)


def kernel(x, ffn1_norm, ffn1_w_gate, ffn1_w_up, ffn1_w_down, mix_norm, w_in, cmp_pos_k, cmp_w1_k, cmp_w2_k, cmp_pos_v, cmp_w1_v, cmp_w2_v, w_branch_a, w_branch_b, w_branch_c, w_out, ffn2_norm, ffn2_w_gate, ffn2_w_up, ffn2_w_down, final_norm):
    batch, seq, d = x.shape
    depth = w_in.shape[0]
    assert seq % max(KC, PROJ_TM) == 0 and WINDOW + NSA_TQ <= seq <= SLC_BLOCK * LANES and w_in.shape[2] == _N_IN
    nc = (seq - CMP_BLOCK) // CMP_STRIDE + 1
    ncp = seq // CMP_STRIDE

    perm = _in_perm()
    w_perm = jnp.where(jnp.asarray(perm >= 0)[None, None, :],
                       jnp.take(w_in, jnp.asarray(np.maximum(perm, 0)), axis=2), 0.0).astype(BF16)
    w_mg = w_in[:, :, _O_MG:].astype(BF16)
    half_rows = CMP_BLOCK * HEAD_DIM // 2
    w1 = jnp.stack([cmp_w1_k, cmp_w1_v], axis=1)
    w1 = jnp.concatenate([w1[:, :, :half_rows], w1[:, :, half_rows:]], axis=-1)
    w2 = jnp.pad(jnp.stack([cmp_w2_k, cmp_w2_v], axis=1), ((0, 0), (0, 0), (0, 0), (0, LANES - HEAD_DIM)))
    pos = jnp.stack([cmp_pos_k, cmp_pos_v], axis=1).reshape(depth, 2, 2, half_rows)
    pos = jnp.pad(pos, ((0, 0), (0, 0), (0, 6), (0, 0)))
    cos, sa, sb = _rope_tables(seq)
    overlap = _overlap(seq)
    hot_b = _block_onehot(seq, MOBA_BLOCK)
    hot_s = _block_onehot(seq, SLC_BLOCK)
    bf = lambda a: a.astype(BF16)
    row = lambda a: a.reshape(1, d)

    x2 = x.reshape(batch * seq, d)
    for l in range(depth):
        x2 = _ffn(x2, row(ffn1_norm[l]), bf(ffn1_w_gate[l]), bf(ffn1_w_up[l]), bf(ffn1_w_down[l]),
                  row(final_norm), False)
        p = _inproj(x2, row(mix_norm[l]), w_perm[l], cos, sa, sb, seq)
        xc = p["ccmp"].reshape(batch, seq, 4, HEAD_DIM).transpose(0, 2, 1, 3).reshape(batch, 4, ncp, 1024)
        kc, vc = _compress(xc, w1[l], w2[l], pos[l], nc)
        o_a = _dsa(p, batch, seq)
        o_b = _moba(p, hot_b, batch, seq)
        o_c = _nsa(p, kc[:, :C_GROUPS], vc[:, C_GROUPS:], overlap, hot_s, batch, seq)
        flat = lambda a: a.reshape(batch * seq, a.shape[-1])
        x2 = _merge(x2, row(mix_norm[l]), flat(o_a), flat(o_b), flat(o_c), w_mg[l],
                    bf(w_branch_a[l]), bf(w_branch_b[l]), bf(w_branch_c[l]), bf(w_out[l]))
        x2 = _ffn(x2, row(ffn2_norm[l]), bf(ffn2_w_gate[l]), bf(ffn2_w_up[l]), bf(ffn2_w_down[l]),
                  row(final_norm), l == depth - 1)
    return x2.reshape(batch, seq, d)
```

```python
import functools
import math

import numpy as np
import jax
import jax.numpy as jnp
from jax import lax
from jax.experimental import pallas as pl
from jax.experimental.pallas import tpu as pltpu

HEAD_DIM = 64
ROPE_THETA = 10000.0
NORM_EPS = 1e-6
A_HEADS = 4
IDX_HEADS = 4
IDX_DIM = 64
DSA_TOPK_MAX = 256
B_HEADS = 4
MOBA_BLOCK = 256
MOBA_TOPK = 3
C_HEADS = 8
C_GROUPS = 2
C_REP = C_HEADS // C_GROUPS
CMP_BLOCK = 32
CMP_STRIDE = 16
CMP_HIDDEN = 128
SLC_BLOCK = 64
SLC_TOPN = 16
WINDOW = 512
FORCE_SCORE = 1e30

LANES = 128
VMEM_LIMIT = 56 * 1024 * 1024

DSA_TQ = 256
MOBA_TQ = 256
NSA_TQ = 256
KC = 512
COUNT_KEYS = 128
FFN_TM = 512
PROJ_TM = 512
MERGE_TM = 512

NEG_BIG = -1e30
QSCALE = HEAD_DIM ** -0.5 * math.log2(math.e)
F32 = jnp.float32
BF16 = jnp.bfloat16
HIGHEST = lax.Precision.HIGHEST

_O_AQ = 0
_O_AK = 256
_O_AV = 320
_O_IQ = 384
_O_IK = 640
_O_IW = 704
_O_BQ = 708
_O_BK = 964
_O_BV = 1220
_O_CQ = 1476
_O_CKV = 1988
_O_CG = 2756
_O_MG = 2780
_N_IN = 5852

P_AQ = 0
P_AK = 256
P_IK = 320
P_IQ = 384
P_BQ = 640
P_BK = 896
P_CKS = 1152
P_CKW = 1280
P_CQ = 1408
P_ROPE_END = 1920
P_BV = 1920
P_KCMP = 2176
P_VS = 2432
P_VW = 2560
P_AV = 2688
P_MISC = 2752
P_COLS = 2816
MISC_CG = 0
MISC_IW = 24


def _in_perm():
    perm = -np.ones((P_COLS,), np.int64)

    def put(dst, src, n):
        perm[dst:dst + n] = np.arange(src, src + n)

    put(P_AQ, _O_AQ, 256)
    put(P_AK, _O_AK, 64)
    put(P_IK, _O_IK, 64)
    put(P_IQ, _O_IQ, 256)
    put(P_BQ, _O_BQ, 256)
    put(P_BK, _O_BK, 256)
    ckv = lambda s, g: _O_CKV + (s * C_GROUPS + g) * HEAD_DIM
    for g in range(C_GROUPS):
        put(P_CKS + 64 * g, ckv(2, g), 64)
        put(P_CKW + 64 * g, ckv(4, g), 64)
        put(P_KCMP + 64 * g, ckv(0, g), 64)
        put(P_KCMP + 128 + 64 * g, ckv(1, g), 64)
        put(P_VS + 64 * g, ckv(3, g), 64)
        put(P_VW + 64 * g, ckv(5, g), 64)
    put(P_CQ, _O_CQ, 512)
    put(P_BV, _O_BV, 256)
    put(P_AV, _O_AV, 64)
    put(P_MISC + MISC_CG, _O_CG, 24)
    put(P_MISC + MISC_IW, _O_IW, 4)
    return perm


def _cparams(sem):
    return pltpu.CompilerParams(dimension_semantics=sem, vmem_limit_bytes=VMEM_LIMIT)


def _rms(x, g):
    y = x * lax.rsqrt(jnp.mean(x * x, axis=-1, keepdims=True) + NORM_EPS)
    return y * g


def _dot_nt(a, b, precision=None):
    return lax.dot_general(a, b, (((1,), (1,)), ((), ())), precision=precision,
                           preferred_element_type=F32)


def _ffn_body(x_ref, g_ref, wg_ref, wu_ref, wd_ref, fg_ref, o_ref, h_scr, acc_scr, *, final_norm):
    f = pl.program_id(1)

    @pl.when(f == 0)
    def _():
        h_scr[...] = _rms(x_ref[...], g_ref[...]).astype(BF16)
        acc_scr[...] = jnp.zeros_like(acc_scr)

    h = h_scr[...]
    a = jnp.dot(h, wg_ref[...], preferred_element_type=F32)
    u = jnp.dot(h, wu_ref[...], preferred_element_type=F32)
    act = (a * jax.nn.sigmoid(a) * u).astype(BF16)
    acc_scr[...] += jnp.dot(act, wd_ref[...], preferred_element_type=F32)

    @pl.when(f == pl.num_programs(1) - 1)
    def _():
        y = x_ref[...] + 0.5 * acc_scr[...]
        if final_norm:
            y = _rms(y, fg_ref[...])
        o_ref[...] = y


def _ffn(x2, g, wg, wu, wd, fg, final_norm):
    m, d = x2.shape
    dff = wg.shape[1]
    tf = dff // 2 if (dff // 2) % LANES == 0 else dff
    tm = FFN_TM
    return pl.pallas_call(
        functools.partial(_ffn_body, final_norm=final_norm),
        grid=(m // tm, dff // tf),
        in_specs=[
            pl.BlockSpec((tm, d), lambda i, f: (i, 0)),
            pl.BlockSpec((1, d), lambda i, f: (0, 0)),
            pl.BlockSpec((d, tf), lambda i, f: (0, f)),
            pl.BlockSpec((d, tf), lambda i, f: (0, f)),
            pl.BlockSpec((tf, d), lambda i, f: (f, 0)),
            pl.BlockSpec((1, d), lambda i, f: (0, 0)),
        ],
        out_specs=pl.BlockSpec((tm, d), lambda i, f: (i, 0)),
        out_shape=jax.ShapeDtypeStruct((m, d), F32),
        scratch_shapes=[pltpu.VMEM((tm, d), BF16), pltpu.VMEM((tm, d), F32)],
        compiler_params=_cparams(("parallel", "arbitrary")),
        name="ffn",
    )(x2, g, wg, wu, wd, fg)


def _lane_iota(shape):
    return lax.broadcasted_iota(jnp.int32, shape, len(shape) - 1)


def _split_hi_lo(x):
    hi = x.astype(BF16).astype(F32)
    return hi, x - hi


def _inproj_body(x_ref, g_ref, w_ref, cos_ref, sa_ref, sb_ref,
                 aq_ref, akv_ref, iq_ref, ik_ref, misc_ref, bq_ref, bqp_ref, bkv_ref, km_ref,
                 cqn_ref, cqr_ref, cslc_ref, cwin_ref, ccmp_ref, av1_ref, bv1_ref, cvs1_ref, cvw1_ref):
    tm = x_ref.shape[0]
    h = _rms(x_ref[...], g_ref[...]).astype(BF16)
    cos, sa, sb = cos_ref[...], sa_ref[...], sb_ref[...]
    lane = _lane_iota((tm, LANES))
    low = lane < HEAD_DIM

    raw = []
    rot = []
    for j in range(P_COLS // 256):
        z = jnp.dot(h, w_ref[:, 256 * j:256 * (j + 1)], preferred_element_type=F32)
        for half in range(2):
            p = z[:, LANES * half:LANES * (half + 1)]
            raw.append(p)
            if LANES * len(raw) <= P_ROPE_END:
                rot.append(p * cos + pltpu.roll(p, 32, axis=1) * sa + pltpu.roll(p, 96, axis=1) * sb)

    def pick(pieces, col):
        p = pieces[col // LANES]
        return pltpu.roll(p, HEAD_DIM, axis=1) if col % LANES else p

    def join(lo, hi=None):
        if hi is None:
            return jnp.where(low, lo, 0.0)
        return jnp.where(low, lo, pltpu.roll(hi, HEAD_DIM, axis=1))

    def one_v(v):
        return jnp.where(low, 1.0, pltpu.roll(v, HEAD_DIM, axis=1)).astype(BF16)

    scale = QSCALE

    for hh in range(A_HEADS):
        aq_ref[:, LANES * hh:LANES * (hh + 1)] = join(pick(rot, P_AQ + 64 * hh) * scale).astype(BF16)
        qhi, qlo = _split_hi_lo(pick(rot, P_IQ + 64 * hh))
        iq_ref[:, 256 * hh:256 * hh + LANES] = join(qhi, qlo).astype(BF16)
        iq_ref[:, 256 * hh + LANES:256 * (hh + 1)] = join(qhi).astype(BF16)
    akv_ref[...] = join(pick(rot, P_AK), pick(raw, P_AV)).astype(BF16)
    av1_ref[...] = one_v(pick(raw, P_AV))
    khi, klo = _split_hi_lo(pick(rot, P_IK))
    ik_ref[:, :LANES] = join(khi, khi).astype(BF16)
    ik_ref[:, LANES:] = join(klo).astype(BF16)
    misc_ref[...] = pick(raw, P_MISC)

    bq_ref[:, :LANES] = rot[P_BQ // LANES]
    bq_ref[:, LANES:] = rot[P_BQ // LANES + 1]
    for hh in range(B_HEADS):
        bqp_ref[:, LANES * hh:LANES * (hh + 1)] = join(pick(rot, P_BQ + 64 * hh) * scale).astype(BF16)
        bkv_ref[:, LANES * hh:LANES * (hh + 1)] = join(pick(rot, P_BK + 64 * hh),
                                                      pick(raw, P_BV + 64 * hh)).astype(BF16)
        bv1_ref[:, LANES * hh:LANES * (hh + 1)] = one_v(pick(raw, P_BV + 64 * hh))
    nblk = tm // MOBA_BLOCK
    km_ref[...] = jnp.zeros_like(km_ref)
    for half in range(2):
        kp = rot[P_BK // LANES + half]
        for b in range(nblk):
            km_ref[0, b:b + 1, LANES * half:LANES * (half + 1)] = jnp.mean(
                kp[MOBA_BLOCK * b:MOBA_BLOCK * (b + 1)], axis=0, keepdims=True)

    for hh in range(C_HEADS):
        nhi, nlo = _split_hi_lo(pick(raw, P_CQ + 64 * hh) * scale)
        cqn_ref[:, 256 * hh:256 * hh + LANES] = join(nhi, nlo).astype(BF16)
        cqn_ref[:, 256 * hh + LANES:256 * (hh + 1)] = join(nhi).astype(BF16)
        cqr_ref[:, LANES * hh:LANES * (hh + 1)] = join(pick(rot, P_CQ + 64 * hh) * scale).astype(BF16)
    for g in range(C_GROUPS):
        cslc_ref[:, LANES * g:LANES * (g + 1)] = join(pick(rot, P_CKS + 64 * g),
                                                     pick(raw, P_VS + 64 * g)).astype(BF16)
        cwin_ref[:, LANES * g:LANES * (g + 1)] = join(pick(rot, P_CKW + 64 * g),
                                                     pick(raw, P_VW + 64 * g)).astype(BF16)
        cvs1_ref[:, LANES * g:LANES * (g + 1)] = one_v(pick(raw, P_VS + 64 * g))
        cvw1_ref[:, LANES * g:LANES * (g + 1)] = one_v(pick(raw, P_VW + 64 * g))
    ccmp_ref[:, :LANES] = raw[P_KCMP // LANES]
    ccmp_ref[:, LANES:] = raw[P_KCMP // LANES + 1]


def _inproj(x2, g, w, cos, sa, sb, seq):
    m, d = x2.shape
    tm = PROJ_TM
    nt = seq // tm
    row = lambda c: pl.BlockSpec((tm, c), lambda i: (i, 0))
    tab = pl.BlockSpec((tm, LANES), lambda i: (i % nt, 0))
    outs = [
        ("aq", 4 * LANES, BF16), ("akv", LANES, BF16), ("iq", 4 * 256, BF16), ("ik", 256, BF16),
        ("misc", LANES, F32), ("bq", 256, F32), ("bqp", 4 * LANES, BF16), ("bkv", 4 * LANES, BF16),
        ("km", None, F32),
        ("cqn", 8 * 256, BF16), ("cqr", 8 * LANES, BF16), ("cslc", 2 * LANES, BF16),
        ("cwin", 2 * LANES, BF16), ("ccmp", 256, F32),
        ("av1", LANES, BF16), ("bv1", 4 * LANES, BF16), ("cvs1", 2 * LANES, BF16), ("cvw1", 2 * LANES, BF16),
    ]
    out_specs, out_shape = [], []
    for name, c, dt in outs:
        if name == "km":
            out_specs.append(pl.BlockSpec((1, 8, 256), lambda i: (i, 0, 0)))
            out_shape.append(jax.ShapeDtypeStruct((m // tm, 8, 256), dt))
        else:
            out_specs.append(row(c))
            out_shape.append(jax.ShapeDtypeStruct((m, c), dt))
    res = pl.pallas_call(
        _inproj_body,
        grid=(m // tm,),
        in_specs=[row(d), pl.BlockSpec((1, d), lambda i: (0, 0)),
                  pl.BlockSpec((d, P_COLS), lambda i: (0, 0)), tab, tab, tab],
        out_specs=out_specs,
        out_shape=out_shape,
        compiler_params=_cparams(("parallel",)),
        name="inproj",
    )(x2, g, w, cos, sa, sb)
    return dict(zip([o[0] for o in outs], res))


def _compress_body(x_ref, w1_ref, w2_ref, pos_ref, ok_ref, ov_ref, *, n_valid):
    x = x_ref[0, 0]
    w1 = w1_ref[0]
    pre = jnp.dot(x, w1, precision=HIGHEST, preferred_element_type=F32)
    pp = jnp.dot(pos_ref[0], w1, precision=HIGHEST, preferred_element_type=F32)
    posb = pp[0:1, :CMP_HIDDEN] + pp[1:2, CMP_HIDDEN:]
    ncp = x.shape[0]
    upper = pre[:, :CMP_HIDDEN]
    lower_next = pltpu.roll(pre[:, CMP_HIDDEN:], ncp - 1, axis=0)
    hid = jax.nn.gelu(upper + lower_next + posb)
    out = jnp.dot(hid, w2_ref[0], precision=HIGHEST, preferred_element_type=F32)
    rows = lax.broadcasted_iota(jnp.int32, out.shape, 0)
    out = jnp.where(rows < n_valid, out, 0.0)
    low = _lane_iota(out.shape) < HEAD_DIM
    hi, lo = _split_hi_lo(out)
    ok_ref[0, 0, :, :LANES] = jnp.where(low, hi, pltpu.roll(hi, HEAD_DIM, axis=1)).astype(BF16)
    ok_ref[0, 0, :, LANES:] = lo.astype(BF16)
    ov_ref[0, 0] = pltpu.roll(out, HEAD_DIM, axis=1).astype(BF16)


def _compress(xc, w1, w2, pos, n_valid):
    b, four, ncp, _ = xc.shape
    return pl.pallas_call(
        functools.partial(_compress_body, n_valid=n_valid),
        grid=(b, four),
        in_specs=[
            pl.BlockSpec((1, 1, ncp, 1024), lambda i, j: (i, j, 0, 0)),
            pl.BlockSpec((1, 1024, 256), lambda i, j: (j // 2, 0, 0)),
            pl.BlockSpec((1, CMP_HIDDEN, LANES), lambda i, j: (j // 2, 0, 0)),
            pl.BlockSpec((1, 8, 1024), lambda i, j: (j // 2, 0, 0)),
        ],
        out_specs=[pl.BlockSpec((1, 1, ncp, 256), lambda i, j: (i, j, 0, 0)),
                   pl.BlockSpec((1, 1, ncp, LANES), lambda i, j: (i, j, 0, 0))],
        out_shape=[jax.ShapeDtypeStruct((b, four, ncp, 256), BF16),
                   jax.ShapeDtypeStruct((b, four, ncp, LANES), BF16)],
        compiler_params=_cparams(("parallel", "parallel")),
        name="compress",
    )(xc, w1, w2, pos)


def _softmax_step(carry, s3, pv):
    m_old, acc = carry
    heads, tq = s3.shape[0], s3.shape[1]
    m_new = jnp.maximum(m_old, jnp.max(s3, axis=-1, keepdims=True))
    p = jnp.exp2((s3 - m_new).astype(BF16))
    alpha = jnp.exp2(m_old - m_new)
    return m_new, alpha.reshape(heads * tq, 1) * acc + pv(p)


def _online_init(heads, tq):
    return jnp.full((heads, tq, 1), NEG_BIG, F32), jnp.zeros((heads * tq, LANES), F32)


def _online_finish(carry):
    _, acc = carry
    return acc / acc[:, 0:1]


def _pv_shared(v1):
    return lambda p: jnp.dot(p.reshape(p.shape[0] * p.shape[1], p.shape[2]), v1, preferred_element_type=F32)


def _store_heads(o_ref, o, heads, col0=0):
    tq = o.shape[0] // heads
    low = _lane_iota((tq, LANES)) < HEAD_DIM
    for pair in range(heads // 2):
        even = o[tq * (2 * pair):tq * (2 * pair + 1)]
        odd = o[tq * (2 * pair + 1):tq * (2 * pair + 2)]
        piece = jnp.where(low, pltpu.roll(even, HEAD_DIM, axis=1), odd)
        o_ref[0, :, col0 + LANES * pair:col0 + LANES * (pair + 1)] = piece.astype(o_ref.dtype)


def _topn_mask(score, n_pick, axis=-1):
    axis = axis % score.ndim
    idx = lax.broadcasted_iota(jnp.int32, score.shape, axis)
    width = score.shape[axis]
    sel = jnp.zeros(score.shape, F32)
    for _ in range(n_pick):
        m = jnp.max(score, axis=axis, keepdims=True)
        first = jnp.min(jnp.where(score == m, idx, width), axis=axis, keepdims=True)
        hit = idx == first
        sel = jnp.where(hit, 1.0, sel)
        score = jnp.where(hit, -jnp.inf, score)
    return sel


_NEG_INF_KEY = -2139095041


def _key_of(x):
    bits = lax.bitcast_convert_type(x, jnp.int32)
    return jnp.where(bits < 0, bits ^ jnp.int32(0x7FFFFFFF), bits)


def _float_of(key):
    return lax.bitcast_convert_type(jnp.where(key < 0, key ^ jnp.int32(0x7FFFFFFF), key), F32)


def _dsa_body(iq_ref, ik_ref, misc_ref, aq_ref, akv_ref, av1_ref, o_ref, key_scr, *, topk, seq):
    TQ = DSA_TQ
    c = pl.program_id(1)
    q0 = c * TQ
    nchunk = (q0 + TQ + KC - 1) // KC
    qpos = q0 + _lane_iota((1, TQ))
    krow = lax.broadcasted_iota(jnp.int32, (KC, 1), 0)
    srow = lax.broadcasted_iota(jnp.int32, (COUNT_KEYS, 1), 0)
    idx_scale = (IDX_HEADS * IDX_DIM) ** -0.5

    iq = jnp.concatenate([iq_ref[0, :, 256 * h:256 * (h + 1)] for h in range(IDX_HEADS)], axis=0)
    misc_t = misc_ref[0].T
    iw = [misc_t[MISC_IW + h:MISC_IW + h + 1] for h in range(IDX_HEADS)]

    def score_body(j, carry):
        mx, mn = carry
        ks = pl.multiple_of(j * KC, KC)
        lg = jnp.maximum(_dot_nt(ik_ref[0, pl.ds(ks, KC), :], iq), 0.0)
        sc = iw[0] * lg[:, 0:TQ]
        for h in range(1, IDX_HEADS):
            sc = sc + iw[h] * lg[:, TQ * h:TQ * (h + 1)]
        sc = sc * idx_scale
        sc = jnp.where(sc == 0.0, 0.0, sc)
        part = sc.reshape(KC // 8, 8, TQ)
        mx, mn = jnp.maximum(mx, jnp.max(part, axis=0)), jnp.minimum(mn, jnp.min(part, axis=0))
        key_scr[j] = _key_of(jnp.where(ks + krow <= qpos, sc, -jnp.inf))
        return mx, mn

    mx, mn = lax.fori_loop(0, nchunk, score_body,
                           (jnp.full((8, TQ), -jnp.inf, F32), jnp.full((8, TQ), jnp.inf, F32)))

    def count(*hits):
        def body(j, accs):
            accs = list(accs)
            for g in range(KC // COUNT_KEYS):
                k = key_scr[j, COUNT_KEYS * g:COUNT_KEYS * (g + 1), :]
                for i, hit in enumerate(hits):
                    one = jnp.where(hit(k, j * KC + COUNT_KEYS * g), 1.0, 0.0)
                    accs[i] = accs[i] + jnp.sum(one.reshape(COUNT_KEYS // 8, 8, TQ), axis=0)
            return tuple(accs)
        accs = lax.fori_loop(0, nchunk, body, tuple(jnp.zeros((8, TQ), F32) for _ in hits))
        out = tuple(jnp.sum(a, axis=0, keepdims=True) for a in accs)
        return out if len(out) > 1 else out[0]

    n_fin = (qpos + 1).astype(F32)
    crowded = n_fin > topk
    c_pos, c_nn = count(lambda k, p0: k > 0, lambda k, p0: k >= 0)
    positive = c_pos >= topk
    zero_tie = crowded & jnp.logical_not(positive) & (c_nn >= topk)
    lo0 = jnp.where(positive, 1, _key_of(jnp.min(mn, axis=0, keepdims=True)))
    hi0 = jnp.where(positive, _key_of(jnp.max(mx, axis=0, keepdims=True)) + 1, 0)
    c_lo0 = jnp.where(positive, c_pos, n_fin)
    c_hi0 = jnp.where(positive, 0.0, c_nn)
    lo0 = jnp.where(zero_tie, 0, lo0)
    c_hi0 = jnp.where(zero_tie, c_pos, c_hi0)
    found0 = positive & (c_pos == topk)
    act0 = jnp.where(crowded & jnp.logical_not(zero_tie | found0) & (lo0 + 1 < hi0), 1.0, 0.0)
    below0 = jnp.where(found0, 0, _NEG_INF_KEY)
    log_target = math.log(topk - 0.5)

    def search_round(st):
        it, _, lo, hi, c_lo, c_hi, act, found, below = st
        lo_f, hi_f = _float_of(lo), _float_of(hi)
        la, lb = jnp.log(c_lo), jnp.log(jnp.maximum(c_hi, 0.5))
        frac = jnp.clip((la - log_target) / (la - lb), 0.02, 0.98)
        phase = it % 3
        guess = jnp.where(phase == 0, lo_f + (hi_f - lo_f) * frac, (lo_f + hi_f) * 0.5)
        mid = (lo >> 1) + (hi >> 1) + (lo & hi & 1)
        cand = jnp.clip(jnp.where(phase == 2, mid, _key_of(guess)), lo + 1, hi - 1)
        c = count(lambda k, p0: k >= cand)
        live = act > 0.5
        exact = live & (c == topk)
        up = live & (c > topk)
        dn = live & (c < topk)
        below = jnp.where(exact, cand - 1, below)
        found = jnp.where(exact, 1.0, found)
        lo, c_lo = jnp.where(up, cand, lo), jnp.where(up, c, c_lo)
        hi, c_hi = jnp.where(dn, cand, hi), jnp.where(dn, c, c_hi)
        act = jnp.where(live & jnp.logical_not(exact) & (lo + 1 < hi), 1.0, 0.0)
        return it + 1, jnp.sum(act), lo, hi, c_lo, c_hi, act, found, below

    st = lax.while_loop(lambda st: (st[0] < 100) & (st[1] > 0.5), search_round,
                        (jnp.int32(0), jnp.sum(act0), lo0, hi0, c_lo0, c_hi0, act0,
                         jnp.where(found0, 1.0, 0.0), below0))
    _, _, lo, _, _, c_hi, _, found, below = st

    tied = crowded & (found < 0.5)
    thr = jnp.where(tied, lo, below)
    need = topk - c_hi
    any_tie = jnp.max(jnp.where(tied, 1.0, 0.0)) > 0.5

    def tie_cut():
        def pos_body(i, cut):
            cand = cut + lax.shift_left(jnp.int32(1), (seq.bit_length() - 1) - i)
            cnt = count(lambda k, p0: (k == thr) & (p0 + srow < cand))
            return jnp.where(cnt < need, cand, cut)
        return lax.fori_loop(0, seq.bit_length(), pos_body, jnp.zeros((1, TQ), jnp.int32))

    cut = lax.cond(any_tie, tie_cut, lambda: jnp.full((1, TQ), -1, jnp.int32))
    cut = jnp.where(tied, cut, -1)

    q = jnp.concatenate([aq_ref[0, :, LANES * h:LANES * (h + 1)] for h in range(A_HEADS)], axis=0)

    def att_body(j, carry):
        ks = pl.multiple_of(j * KC, KC)
        kv = akv_ref[0, pl.ds(ks, KC), :]
        k = key_scr[j]
        taken = (k > thr) | ((k == thr) & (ks + krow <= cut))
        bias = jnp.where(taken, 0.0, NEG_BIG).T
        s3 = _dot_nt(q, kv).reshape(A_HEADS, TQ, KC) + bias[None]
        return _softmax_step(carry, s3, _pv_shared(av1_ref[0, pl.ds(ks, KC), :]))

    carry = lax.fori_loop(0, nchunk, att_body, _online_init(A_HEADS, TQ))
    _store_heads(o_ref, _online_finish(carry), A_HEADS)


def _dsa(p, batch, seq):
    topk = min(DSA_TOPK_MAX, seq // 4)
    r3 = lambda a: a.reshape(batch, seq, a.shape[-1])
    qblk = lambda c: pl.BlockSpec((1, DSA_TQ, c), lambda b, i: (b, i, 0))
    full = lambda c: pl.BlockSpec((1, seq, c), lambda b, i: (b, 0, 0))
    return pl.pallas_call(
        functools.partial(_dsa_body, topk=topk, seq=seq),
        grid=(batch, seq // DSA_TQ),
        in_specs=[qblk(4 * 256), full(256), qblk(LANES), qblk(4 * LANES), full(LANES), full(LANES)],
        out_specs=qblk(A_HEADS * HEAD_DIM),
        out_shape=jax.ShapeDtypeStruct((batch, seq, A_HEADS * HEAD_DIM), BF16),
        scratch_shapes=[pltpu.VMEM((seq // KC, KC, DSA_TQ), jnp.int32)],
        compiler_params=_cparams(("parallel", "arbitrary")),
        name="dsa",
    )(r3(p["iq"]), r3(p["ik"]), r3(p["misc"]), r3(p["aq"]), r3(p["akv"]), r3(p["av1"]))


def _pv_per_head(v1s):
    return lambda p: jnp.concatenate(
        [jnp.dot(p[h], v1s[h], preferred_element_type=F32) for h in range(len(v1s))], axis=0)


def _moba_body(bq_ref, bqp_ref, bkv_ref, bv1_ref, km_ref, hot_ref, o_ref, *, nbp):
    TQ = MOBA_TQ
    c = pl.program_id(1)
    q0 = c * TQ
    own = q0 // MOBA_BLOCK
    row = q0 + lax.broadcasted_iota(jnp.int32, (TQ, 1), 0)
    col = _lane_iota((1, MOBA_BLOCK))
    ks_own = pl.multiple_of(own * MOBA_BLOCK, MOBA_BLOCK)
    heads = range(B_HEADS)
    kv_at = lambda ks, h: bkv_ref[0, pl.ds(ks, MOBA_BLOCK), LANES * h:LANES * (h + 1)]
    v1_at = lambda ks, h: bv1_ref[0, pl.ds(ks, MOBA_BLOCK), LANES * h:LANES * (h + 1)]

    blk = lax.broadcasted_iota(jnp.int32, (nbp, 1), 0)
    gate = jnp.concatenate(
        [_dot_nt(km_ref[0, :nbp, HEAD_DIM * h:HEAD_DIM * (h + 1)], bq_ref[0, :, HEAD_DIM * h:HEAD_DIM * (h + 1)],
                 precision=HIGHEST) for h in heads], axis=1)
    picked = (blk < own) & (_topn_mask(jnp.where(blk < own, gate, -jnp.inf), MOBA_TOPK, axis=0) > 0.0)
    bias_t = jnp.where(picked, 0.0, NEG_BIG)
    fill = jnp.zeros((LANES - nbp, TQ), F32)

    q_aug, s_own = [], []
    for h in heads:
        bias = jnp.concatenate([bias_t[:, TQ * h:TQ * (h + 1)], fill], axis=0).T
        q = bqp_ref[0, :, LANES * h:LANES * (h + 1)]
        q_aug.append(jnp.concatenate([q, bias.astype(BF16)], axis=1))
        s_own.append(jnp.where(ks_own + col <= row, _dot_nt(q, kv_at(ks_own, h)), NEG_BIG))

    stack = lambda xs: jnp.concatenate(xs, axis=0).reshape(B_HEADS, TQ, MOBA_BLOCK)
    carry = _softmax_step(_online_init(B_HEADS, TQ), stack(s_own),
                          _pv_per_head([v1_at(ks_own, h) for h in heads]))

    def scores(j):
        ks = pl.multiple_of(j * MOBA_BLOCK, MOBA_BLOCK)
        hot = hot_ref[pl.ds(ks, MOBA_BLOCK), :]
        return stack([_dot_nt(q_aug[h], jnp.concatenate([kv_at(ks, h), hot], axis=1)) for h in heads])

    def body(j, carry):
        state, s_cur = carry
        s_next = scores(jnp.minimum(j + 1, own - 1))
        ks = pl.multiple_of(j * MOBA_BLOCK, MOBA_BLOCK)
        return _softmax_step(state, s_cur, _pv_per_head([v1_at(ks, h) for h in heads])), s_next

    carry, _ = lax.fori_loop(0, own, body, (carry, scores(0)))
    _store_heads(o_ref, _online_finish(carry), B_HEADS)


def _moba(p, hot, batch, seq):
    nb = seq // MOBA_BLOCK
    r3 = lambda a: a.reshape(batch, seq, a.shape[-1])
    km = p["km"][:, :PROJ_TM // MOBA_BLOCK].reshape(batch, nb, 256)
    km = jnp.pad(km, ((0, 0), (0, LANES - nb), (0, 0)))
    qblk = lambda c: pl.BlockSpec((1, MOBA_TQ, c), lambda b, i: (b, i, 0))
    return pl.pallas_call(
        functools.partial(_moba_body, nbp=min(LANES, -(-nb // 8) * 8)),
        grid=(batch, seq // MOBA_TQ),
        in_specs=[qblk(256), qblk(4 * LANES),
                  pl.BlockSpec((1, seq, 4 * LANES), lambda b, i: (b, 0, 0)),
                  pl.BlockSpec((1, seq, 4 * LANES), lambda b, i: (b, 0, 0)),
                  pl.BlockSpec((1, LANES, 256), lambda b, i: (b, 0, 0)),
                  pl.BlockSpec((seq, LANES), lambda b, i: (0, 0))],
        out_specs=qblk(B_HEADS * HEAD_DIM),
        out_shape=jax.ShapeDtypeStruct((batch, seq, B_HEADS * HEAD_DIM), BF16),
        compiler_params=_cparams(("parallel", "arbitrary")),
        name="moba",
    )(r3(p["bq"]), r3(p["bqp"]), r3(p["bkv"]), r3(p["bv1"]), km, hot)


def _nsa_body(cqn_ref, cqr_ref, misc_ref, kc_ref, vc_ref, cslc_ref, cwin_ref, cvs1_ref, cvw1_ref, ov_ref, hot_ref,
              o_ref,
              *, ncp, n_sel):
    TQ = NSA_TQ
    c = pl.program_id(1)
    q0 = c * TQ
    last = (q0 + TQ - 1) // KC
    ks_last = pl.multiple_of(last * KC, KC)
    row = q0 + lax.broadcasted_iota(jnp.int32, (TQ, 1), 0)
    col = _lane_iota((1, KC))
    gates = jax.nn.sigmoid(misc_ref[0])
    cmp_vis = _lane_iota((1, ncp)) * CMP_STRIDE + (CMP_BLOCK - 1) <= row
    wstart = pl.multiple_of(jnp.maximum(q0 - WINDOW, 0), TQ)
    wlen = WINDOW + TQ
    wdiff = row - (wstart + _lane_iota((1, wlen)))
    wmask = (wdiff >= 0) & (wdiff < WINDOW)

    o_cmp, imp_t = [], []
    for g in range(C_GROUPS):
        qn = jnp.concatenate([cqn_ref[0, :, 256 * (C_REP * g + r):256 * (C_REP * g + r + 1)]
                              for r in range(C_REP)], axis=0)
        s3 = _dot_nt(qn, kc_ref[0, g]).reshape(C_REP, TQ, ncp)
        m = jnp.max(jnp.where(cmp_vis[None], s3, -jnp.inf), axis=-1, keepdims=True)
        m = jnp.where(m > -jnp.inf, m, 0.0)
        pc = jnp.where(cmp_vis[None], jnp.exp2(s3 - m), 0.0)
        den = jnp.sum(pc, axis=-1, keepdims=True)
        pc = pc / jnp.where(den > 0, den, 1.0)
        o_cmp.append(jnp.dot(pc.reshape(C_REP * TQ, ncp).astype(BF16), vc_ref[0, g],
                             preferred_element_type=F32))
        psum = pc[0]
        for r in range(1, C_REP):
            psum = psum + pc[r]
        hi = psum.astype(BF16)
        rest = psum - hi.astype(F32)
        mid = rest.astype(BF16)
        low = (rest - mid.astype(F32)).astype(BF16)
        imp = jnp.dot(jnp.concatenate([hi, mid, low], axis=1), ov_ref[...], preferred_element_type=F32)
        imp_t.append(imp.T)

    blk = lax.broadcasted_iota(jnp.int32, (LANES, 1), 0)
    jq = (q0 + _lane_iota((1, C_GROUPS * TQ)) % TQ) // SLC_BLOCK
    adm = blk <= jq
    forced = adm & ((blk == 0) | (blk == jq) | (blk == jq - 1))
    score = jnp.where(forced, FORCE_SCORE, jnp.where(adm, jnp.concatenate(imp_t, axis=1), -jnp.inf))
    picked = adm & (_topn_mask(score, n_sel, axis=0) > 0.0)
    bias_t = jnp.where(picked, 0.0, NEG_BIG)

    for g in range(C_GROUPS):
        o_c = o_cmp[g]
        bias = bias_t[:, TQ * g:TQ * (g + 1)].T.astype(BF16)

        qr = jnp.concatenate([cqr_ref[0, :, LANES * (C_REP * g + r):LANES * (C_REP * g + r + 1)]
                              for r in range(C_REP)], axis=0)
        qa = jnp.concatenate([qr, jnp.concatenate([bias] * C_REP, axis=0)], axis=1)
        kv_at = lambda j: cslc_ref[0, pl.ds(pl.multiple_of(j * KC, KC), KC), LANES * g:LANES * (g + 1)]
        v1_at = lambda j: cvs1_ref[0, pl.ds(pl.multiple_of(j * KC, KC), KC), LANES * g:LANES * (g + 1)]
        slc_scores = lambda j: _dot_nt(qa, jnp.concatenate(
            [kv_at(j), hot_ref[pl.ds(pl.multiple_of(j * KC, KC), KC), :]], axis=1))

        def slc_body(j, carry):
            return _softmax_step(carry, slc_scores(j).reshape(C_REP, TQ, KC), _pv_shared(v1_at(j)))

        carry = lax.fori_loop(0, last, slc_body, _online_init(C_REP, TQ))
        s3 = jnp.where((ks_last + col <= row)[None], slc_scores(last).reshape(C_REP, TQ, KC), NEG_BIG)
        o_s = _online_finish(_softmax_step(carry, s3, _pv_shared(v1_at(last))))

        kvw = cwin_ref[0, pl.ds(wstart, wlen), LANES * g:LANES * (g + 1)]
        s3 = jnp.where(wmask[None], _dot_nt(qr, kvw).reshape(C_REP, TQ, wlen), NEG_BIG)
        v1w = cvw1_ref[0, pl.ds(wstart, wlen), LANES * g:LANES * (g + 1)]
        o_w = _online_finish(_softmax_step(_online_init(C_REP, TQ), s3, _pv_shared(v1w)))

        outs = []
        for r in range(C_REP):
            hh = C_REP * g + r
            gcol = lambda j: gates[:, MISC_CG + 3 * hh + j:MISC_CG + 3 * hh + j + 1]
            rows = slice(TQ * r, TQ * (r + 1))
            outs.append(gcol(0) * o_c[rows] + gcol(1) * o_s[rows] + gcol(2) * o_w[rows])
        _store_heads(o_ref, jnp.concatenate(outs, axis=0), C_REP, col0=C_REP * HEAD_DIM * g)


def _nsa(p, kcmp, vcmp, overlap, hot, batch, seq):
    ncp = seq // CMP_STRIDE
    n_sel = min(SLC_TOPN, seq // SLC_BLOCK)
    r3 = lambda a: a.reshape(batch, seq, a.shape[-1])
    qblk = lambda c: pl.BlockSpec((1, NSA_TQ, c), lambda b, i: (b, i, 0))
    full = lambda c: pl.BlockSpec((1, seq, c), lambda b, i: (b, 0, 0))
    return pl.pallas_call(
        functools.partial(_nsa_body, ncp=ncp, n_sel=n_sel),
        grid=(batch, seq // NSA_TQ),
        in_specs=[qblk(8 * 256), qblk(8 * LANES), qblk(LANES),
                  pl.BlockSpec((1, C_GROUPS, ncp, 256), lambda b, i: (b, 0, 0, 0)),
                  pl.BlockSpec((1, C_GROUPS, ncp, LANES), lambda b, i: (b, 0, 0, 0)),
                  full(2 * LANES), full(2 * LANES), full(2 * LANES), full(2 * LANES),
                  pl.BlockSpec((3 * ncp, LANES), lambda b, i: (0, 0)),
                  pl.BlockSpec((seq, LANES), lambda b, i: (0, 0))],
        out_specs=qblk(C_HEADS * HEAD_DIM),
        out_shape=jax.ShapeDtypeStruct((batch, seq, C_HEADS * HEAD_DIM), BF16),
        compiler_params=_cparams(("parallel", "arbitrary")),
        name="nsa",
    )(r3(p["cqn"]), r3(p["cqr"]), r3(p["misc"]), kcmp, vcmp, r3(p["cslc"]), r3(p["cwin"]),
      r3(p["cvs1"]), r3(p["cvw1"]), overlap, hot)


def _merge_body(x_ref, g_ref, oa_ref, ob_ref, oc_ref, wm_ref, wa_ref, wb_ref, wc_ref, wo_ref, o_ref):
    x = x_ref[...]
    d = x.shape[1]
    h = _rms(x, g_ref[...]).astype(BF16)
    merged = None
    for i, (o_r, w_r) in enumerate(((oa_ref, wa_ref), (ob_ref, wb_ref), (oc_ref, wc_ref))):
        gate = jax.nn.sigmoid(jnp.dot(h, wm_ref[:, d * i:d * (i + 1)], preferred_element_type=F32))
        y = gate * jnp.dot(o_r[...], w_r[...], preferred_element_type=F32)
        merged = y if merged is None else merged + y
    o_ref[...] = x + jnp.dot(merged.astype(BF16), wo_ref[...], preferred_element_type=F32)


def _merge(x2, g, oa, ob, oc, wm, wa, wb, wc, wo):
    m, d = x2.shape
    tm = MERGE_TM
    row = lambda c: pl.BlockSpec((tm, c), lambda i: (i, 0))
    const = lambda a: pl.BlockSpec(a.shape, lambda i: (0, 0))
    return pl.pallas_call(
        _merge_body,
        grid=(m // tm,),
        in_specs=[row(d), const(g), row(oa.shape[1]), row(ob.shape[1]), row(oc.shape[1]),
                  const(wm), const(wa), const(wb), const(wc), const(wo)],
        out_specs=row(d),
        out_shape=jax.ShapeDtypeStruct((m, d), F32),
        compiler_params=_cparams(("parallel",)),
        name="merge",
    )(x2, g, oa, ob, oc, wm, wa, wb, wc, wo)


def _rope_tables(seq):
    half = HEAD_DIM // 2
    inv = ROPE_THETA ** (-jnp.arange(half, dtype=F32) / half)
    ang = jnp.arange(seq, dtype=F32)[:, None] * inv[None, :]
    lane = np.arange(LANES)
    ang = ang[:, lane % half]
    second = jnp.asarray((lane % HEAD_DIM) >= half)[None, :]
    sin = jnp.sin(ang)
    return jnp.cos(ang), jnp.where(second, sin, 0.0), jnp.where(second, 0.0, -sin)


def _overlap(seq):
    ncp = seq // CMP_STRIDE
    nc = (seq - CMP_BLOCK) // CMP_STRIDE + 1
    ns = seq // SLC_BLOCK
    cs = np.arange(ncp) * CMP_STRIDE
    ss = np.arange(ns) * SLC_BLOCK
    ov = (cs[:, None] < ss[None, :] + SLC_BLOCK) & (ss[None, :] <= cs[:, None] + CMP_BLOCK - 1)
    ov &= (np.arange(ncp) < nc)[:, None]
    out = np.zeros((ncp, LANES), np.float32)
    out[:, :ns] = ov
    return jnp.asarray(np.tile(out, (3, 1)), BF16)


def _block_onehot(seq, block):
    return jnp.asarray(np.arange(seq)[:, None] // block == np.arange(LANES)[None, :], BF16)


def kernel(x, ffn1_norm, ffn1_w_gate, ffn1_w_up, ffn1_w_down, mix_norm, w_in, cmp_pos_k, cmp_w1_k, cmp_w2_k, cmp_pos_v, cmp_w1_v, cmp_w2_v, w_branch_a, w_branch_b, w_branch_c, w_out, ffn2_norm, ffn2_w_gate, ffn2_w_up, ffn2_w_down, final_norm):
    batch, seq, d = x.shape
    depth = w_in.shape[0]
    assert seq % max(KC, PROJ_TM) == 0 and WINDOW + NSA_TQ <= seq <= SLC_BLOCK * LANES and w_in.shape[2] == _N_IN
    nc = (seq - CMP_BLOCK) // CMP_STRIDE + 1
    ncp = seq // CMP_STRIDE

    perm = _in_perm()
    w_perm = jnp.where(jnp.asarray(perm >= 0)[None, None, :],
                       jnp.take(w_in, jnp.asarray(np.maximum(perm, 0)), axis=2), 0.0).astype(BF16)
    w_mg = w_in[:, :, _O_MG:].astype(BF16)
    half_rows = CMP_BLOCK * HEAD_DIM // 2
    w1 = jnp.stack([cmp_w1_k, cmp_w1_v], axis=1)
    w1 = jnp.concatenate([w1[:, :, :half_rows], w1[:, :, half_rows:]], axis=-1)
    w2 = jnp.pad(jnp.stack([cmp_w2_k, cmp_w2_v], axis=1), ((0, 0), (0, 0), (0, 0), (0, LANES - HEAD_DIM)))
    pos = jnp.stack([cmp_pos_k, cmp_pos_v], axis=1).reshape(depth, 2, 2, half_rows)
    pos = jnp.pad(pos, ((0, 0), (0, 0), (0, 6), (0, 0)))
    cos, sa, sb = _rope_tables(seq)
    overlap = _overlap(seq)
    hot_b = _block_onehot(seq, MOBA_BLOCK)
    hot_s = _block_onehot(seq, SLC_BLOCK)
    bf = lambda a: a.astype(BF16)
    row = lambda a: a.reshape(1, d)

    x2 = x.reshape(batch * seq, d)
    for l in range(depth):
        x2 = _ffn(x2, row(ffn1_norm[l]), bf(ffn1_w_gate[l]), bf(ffn1_w_up[l]), bf(ffn1_w_down[l]),
                  row(final_norm), False)
        p = _inproj(x2, row(mix_norm[l]), w_perm[l], cos, sa, sb, seq)
        xc = p["ccmp"].reshape(batch, seq, 4, HEAD_DIM).transpose(0, 2, 1, 3).reshape(batch, 4, ncp, 1024)
        kc, vc = _compress(xc, w1[l], w2[l], pos[l], nc)
        o_a = _dsa(p, batch, seq)
        o_b = _moba(p, hot_b, batch, seq)
        o_c = _nsa(p, kc[:, :C_GROUPS], vc[:, C_GROUPS:], overlap, hot_s, batch, seq)
        flat = lambda a: a.reshape(batch * seq, a.shape[-1])
        x2 = _merge(x2, row(mix_norm[l]), flat(o_a), flat(o_b), flat(o_c), w_mg[l],
                    bf(w_branch_a[l]), bf(w_branch_b[l]), bf(w_branch_c[l]), bf(w_out[l]))
        x2 = _ffn(x2, row(ffn2_norm[l]), bf(ffn2_w_gate[l]), bf(ffn2_w_up[l]), bf(ffn2_w_down[l]),
                  row(final_norm), l == depth - 1)
    return x2.reshape(batch, seq, d)
```

```python
import functools
import math

import numpy as np
import jax
import jax.numpy as jnp
from jax import lax
from jax.experimental import pallas as pl
from jax.experimental.pallas import tpu as pltpu

HEAD_DIM = 64
ROPE_THETA = 10000.0
NORM_EPS = 1e-6
A_HEADS = 4
IDX_HEADS = 4
IDX_DIM = 64
DSA_TOPK_MAX = 256
B_HEADS = 4
MOBA_BLOCK = 256
MOBA_TOPK = 3
C_HEADS = 8
C_GROUPS = 2
C_REP = C_HEADS // C_GROUPS
CMP_BLOCK = 32
CMP_STRIDE = 16
CMP_HIDDEN = 128
SLC_BLOCK = 64
SLC_TOPN = 16
WINDOW = 512
FORCE_SCORE = 1e30

LANES = 128
VMEM_LIMIT = 56 * 1024 * 1024

DSA_TQ = 256
MOBA_TQ = 256
NSA_TQ = 256
KC = 512
COUNT_KEYS = 128
FFN_TM = 512
PROJ_TM = 512
MERGE_TM = 512

NEG_BIG = -1e30
QSCALE = HEAD_DIM ** -0.5 * math.log2(math.e)
F32 = jnp.float32
BF16 = jnp.bfloat16
HIGHEST = lax.Precision.HIGHEST

_O_AQ = 0
_O_AK = 256
_O_AV = 320
_O_IQ = 384
_O_IK = 640
_O_IW = 704
_O_BQ = 708
_O_BK = 964
_O_BV = 1220
_O_CQ = 1476
_O_CKV = 1988
_O_CG = 2756
_O_MG = 2780
_N_IN = 5852

P_AQ = 0
P_AK = 256
P_IK = 320
P_IQ = 384
P_BQ = 640
P_BK = 896
P_CKS = 1152
P_CKW = 1280
P_CQ = 1408
P_ROPE_END = 1920
P_BV = 1920
P_KCMP = 2176
P_VS = 2432
P_VW = 2560
P_AV = 2688
P_MISC = 2752
P_COLS = 2816
MISC_CG = 0
MISC_IW = 24


def _in_perm():
    perm = -np.ones((P_COLS,), np.int64)

    def put(dst, src, n):
        perm[dst:dst + n] = np.arange(src, src + n)

    put(P_AQ, _O_AQ, 256)
    put(P_AK, _O_AK, 64)
    put(P_IK, _O_IK, 64)
    put(P_IQ, _O_IQ, 256)
    put(P_BQ, _O_BQ, 256)
    put(P_BK, _O_BK, 256)
    ckv = lambda s, g: _O_CKV + (s * C_GROUPS + g) * HEAD_DIM
    for g in range(C_GROUPS):
        put(P_CKS + 64 * g, ckv(2, g), 64)
        put(P_CKW + 64 * g, ckv(4, g), 64)
        put(P_KCMP + 64 * g, ckv(0, g), 64)
        put(P_KCMP + 128 + 64 * g, ckv(1, g), 64)
        put(P_VS + 64 * g, ckv(3, g), 64)
        put(P_VW + 64 * g, ckv(5, g), 64)
    put(P_CQ, _O_CQ, 512)
    put(P_BV, _O_BV, 256)
    put(P_AV, _O_AV, 64)
    put(P_MISC + MISC_CG, _O_CG, 24)
    put(P_MISC + MISC_IW, _O_IW, 4)
    return perm


def _perm_runs():
    perm = _in_perm()
    runs, start = [], 0
    for i in range(1, len(perm) + 1):
        if i == len(perm) or (perm[i] != perm[i - 1] + 1 if perm[i - 1] >= 0 else perm[i] >= 0):
            runs.append((int(perm[start]), i - start))
            start = i
    return runs


def _cparams(sem):
    return pltpu.CompilerParams(dimension_semantics=sem, vmem_limit_bytes=VMEM_LIMIT)


def _rms(x, g):
    y = x * lax.rsqrt(jnp.mean(x * x, axis=-1, keepdims=True) + NORM_EPS)
    return y * g


def _dot_nt(a, b, precision=None):
    return lax.dot_general(a, b, (((1,), (1,)), ((), ())), precision=precision,
                           preferred_element_type=F32)


def _ffn_body(x_ref, g_ref, wg_ref, wu_ref, wd_ref, fg_ref, o_ref, h_scr, acc_scr, *, final_norm):
    f = pl.program_id(1)

    @pl.when(f == 0)
    def _():
        h_scr[...] = _rms(x_ref[...], g_ref[...]).astype(BF16)
        acc_scr[...] = jnp.zeros_like(acc_scr)

    h = h_scr[...]
    a = jnp.dot(h, wg_ref[...], preferred_element_type=F32)
    u = jnp.dot(h, wu_ref[...], preferred_element_type=F32)
    act = (a * jax.nn.sigmoid(a) * u).astype(BF16)
    acc_scr[...] += jnp.dot(act, wd_ref[...], preferred_element_type=F32)

    @pl.when(f == pl.num_programs(1) - 1)
    def _():
        y = x_ref[...] + 0.5 * acc_scr[...]
        if final_norm:
            y = _rms(y, fg_ref[...])
        o_ref[...] = y


def _ffn(x2, g, wg, wu, wd, fg, layer, final_norm):
    m, d = x2.shape
    dff = wg.shape[2]
    tf = dff // 2 if (dff // 2) % LANES == 0 else dff
    tm = FFN_TM
    return pl.pallas_call(
        functools.partial(_ffn_body, final_norm=final_norm),
        grid=(m // tm, dff // tf),
        in_specs=[
            pl.BlockSpec((tm, d), lambda i, f: (i, 0)),
            pl.BlockSpec((1, d), lambda i, f: (0, 0)),
            pl.BlockSpec((None, d, tf), lambda i, f: (layer, 0, f)),
            pl.BlockSpec((None, d, tf), lambda i, f: (layer, 0, f)),
            pl.BlockSpec((None, tf, d), lambda i, f: (layer, f, 0)),
            pl.BlockSpec((1, d), lambda i, f: (0, 0)),
        ],
        out_specs=pl.BlockSpec((tm, d), lambda i, f: (i, 0)),
        out_shape=jax.ShapeDtypeStruct((m, d), F32),
        scratch_shapes=[pltpu.VMEM((tm, d), BF16), pltpu.VMEM((tm, d), F32)],
        compiler_params=_cparams(("parallel", "arbitrary")),
        name="ffn",
    )(x2, g, wg, wu, wd, fg)


def _lane_iota(shape):
    return lax.broadcasted_iota(jnp.int32, shape, len(shape) - 1)


def _split_hi_lo(x):
    hi = x.astype(BF16).astype(F32)
    return hi, x - hi


def _inproj_body(x_ref, g_ref, w_ref, cos_ref, sa_ref, sb_ref,
                 aq_ref, akv_ref, iq_ref, ik_ref, misc_ref, bq_ref, bqp_ref, bkv_ref, km_ref,
                 cqn_ref, cqr_ref, cslc_ref, cwin_ref, ccmp_ref, av1_ref, bv1_ref, cvs1_ref, cvw1_ref):
    tm = x_ref.shape[0]
    h = _rms(x_ref[...], g_ref[...]).astype(BF16)
    cos, sa, sb = cos_ref[...], sa_ref[...], sb_ref[...]
    lane = _lane_iota((tm, LANES))
    low = lane < HEAD_DIM

    raw = []
    rot = []
    for j in range(P_COLS // 256):
        z = jnp.dot(h, w_ref[:, 256 * j:256 * (j + 1)], preferred_element_type=F32)
        for half in range(2):
            p = z[:, LANES * half:LANES * (half + 1)]
            raw.append(p)
            if LANES * len(raw) <= P_ROPE_END:
                rot.append(p * cos + pltpu.roll(p, 32, axis=1) * sa + pltpu.roll(p, 96, axis=1) * sb)

    def pick(pieces, col):
        p = pieces[col // LANES]
        return pltpu.roll(p, HEAD_DIM, axis=1) if col % LANES else p

    def join(lo, hi=None):
        if hi is None:
            return jnp.where(low, lo, 0.0)
        return jnp.where(low, lo, pltpu.roll(hi, HEAD_DIM, axis=1))

    def one_v(v):
        return jnp.where(low, 1.0, pltpu.roll(v, HEAD_DIM, axis=1)).astype(BF16)

    scale = QSCALE

    for hh in range(A_HEADS):
        aq_ref[:, LANES * hh:LANES * (hh + 1)] = join(pick(rot, P_AQ + 64 * hh) * scale).astype(BF16)
        qhi, qlo = _split_hi_lo(pick(rot, P_IQ + 64 * hh))
        iq_ref[:, 256 * hh:256 * hh + LANES] = join(qhi, qlo).astype(BF16)
        iq_ref[:, 256 * hh + LANES:256 * (hh + 1)] = join(qhi).astype(BF16)
    akv_ref[...] = join(pick(rot, P_AK), pick(raw, P_AV)).astype(BF16)
    av1_ref[...] = one_v(pick(raw, P_AV))
    khi, klo = _split_hi_lo(pick(rot, P_IK))
    ik_ref[:, :LANES] = join(khi, khi).astype(BF16)
    ik_ref[:, LANES:] = join(klo).astype(BF16)
    misc_ref[...] = pick(raw, P_MISC)

    bq_ref[:, :LANES] = rot[P_BQ // LANES]
    bq_ref[:, LANES:] = rot[P_BQ // LANES + 1]
    for hh in range(B_HEADS):
        bqp_ref[:, LANES * hh:LANES * (hh + 1)] = join(pick(rot, P_BQ + 64 * hh) * scale).astype(BF16)
        bkv_ref[:, LANES * hh:LANES * (hh + 1)] = join(pick(rot, P_BK + 64 * hh),
                                                      pick(raw, P_BV + 64 * hh)).astype(BF16)
        bv1_ref[:, LANES * hh:LANES * (hh + 1)] = one_v(pick(raw, P_BV + 64 * hh))
    nblk = tm // MOBA_BLOCK
    km_ref[...] = jnp.zeros_like(km_ref)
    for half in range(2):
        kp = rot[P_BK // LANES + half]
        for b in range(nblk):
            km_ref[0, b:b + 1, LANES * half:LANES * (half + 1)] = jnp.mean(
                kp[MOBA_BLOCK * b:MOBA_BLOCK * (b + 1)], axis=0, keepdims=True)

    for hh in range(C_HEADS):
        nhi, nlo = _split_hi_lo(pick(raw, P_CQ + 64 * hh) * scale)
        cqn_ref[:, 256 * hh:256 * hh + LANES] = join(nhi, nlo).astype(BF16)
        cqn_ref[:, 256 * hh + LANES:256 * (hh + 1)] = join(nhi).astype(BF16)
        cqr_ref[:, LANES * hh:LANES * (hh + 1)] = join(pick(rot, P_CQ + 64 * hh) * scale).astype(BF16)
    for g in range(C_GROUPS):
        cslc_ref[:, LANES * g:LANES * (g + 1)] = join(pick(rot, P_CKS + 64 * g),
                                                     pick(raw, P_VS + 64 * g)).astype(BF16)
        cwin_ref[:, LANES * g:LANES * (g + 1)] = join(pick(rot, P_CKW + 64 * g),
                                                     pick(raw, P_VW + 64 * g)).astype(BF16)
        cvs1_ref[:, LANES * g:LANES * (g + 1)] = one_v(pick(raw, P_VS + 64 * g))
        cvw1_ref[:, LANES * g:LANES * (g + 1)] = one_v(pick(raw, P_VW + 64 * g))
    ccmp_ref[:, :LANES] = raw[P_KCMP // LANES]
    ccmp_ref[:, LANES:] = raw[P_KCMP // LANES + 1]


def _inproj(x2, g, w, layer, cos, sa, sb, seq):
    m, d = x2.shape
    tm = PROJ_TM
    nt = seq // tm
    row = lambda c: pl.BlockSpec((tm, c), lambda i: (i, 0))
    tab = pl.BlockSpec((tm, LANES), lambda i: (i % nt, 0))
    outs = [
        ("aq", 4 * LANES, BF16), ("akv", LANES, BF16), ("iq", 4 * 256, BF16), ("ik", 256, BF16),
        ("misc", LANES, F32), ("bq", 256, F32), ("bqp", 4 * LANES, BF16), ("bkv", 4 * LANES, BF16),
        ("km", None, F32),
        ("cqn", 8 * 256, BF16), ("cqr", 8 * LANES, BF16), ("cslc", 2 * LANES, BF16),
        ("cwin", 2 * LANES, BF16), ("ccmp", 256, F32),
        ("av1", LANES, BF16), ("bv1", 4 * LANES, BF16), ("cvs1", 2 * LANES, BF16), ("cvw1", 2 * LANES, BF16),
    ]
    out_specs, out_shape = [], []
    for name, c, dt in outs:
        if name == "km":
            out_specs.append(pl.BlockSpec((1, 8, 256), lambda i: (i, 0, 0)))
            out_shape.append(jax.ShapeDtypeStruct((m // tm, 8, 256), dt))
        else:
            out_specs.append(row(c))
            out_shape.append(jax.ShapeDtypeStruct((m, c), dt))
    res = pl.pallas_call(
        _inproj_body,
        grid=(m // tm,),
        in_specs=[row(d), pl.BlockSpec((1, d), lambda i: (0, 0)),
                  pl.BlockSpec((None, d, P_COLS), lambda i: (layer, 0, 0)), tab, tab, tab],
        out_specs=out_specs,
        out_shape=out_shape,
        compiler_params=_cparams(("parallel",)),
        name="inproj",
    )(x2, g, w, cos, sa, sb)
    return dict(zip([o[0] for o in outs], res))


def _compress_body(x_ref, w1_ref, w2_ref, pos_ref, ok_ref, ov_ref, *, n_valid):
    x = x_ref[0, 0]
    w1 = w1_ref[0]
    pre = jnp.dot(x, w1, precision=HIGHEST, preferred_element_type=F32)
    pp = jnp.dot(pos_ref[0], w1, precision=HIGHEST, preferred_element_type=F32)
    posb = pp[0:1, :CMP_HIDDEN] + pp[1:2, CMP_HIDDEN:]
    ncp = x.shape[0]
    upper = pre[:, :CMP_HIDDEN]
    lower_next = pltpu.roll(pre[:, CMP_HIDDEN:], ncp - 1, axis=0)
    hid = jax.nn.gelu(upper + lower_next + posb)
    out = jnp.dot(hid, w2_ref[0], precision=HIGHEST, preferred_element_type=F32)
    rows = lax.broadcasted_iota(jnp.int32, out.shape, 0)
    out = jnp.where(rows < n_valid, out, 0.0)
    low = _lane_iota(out.shape) < HEAD_DIM
    hi, lo = _split_hi_lo(out)
    ok_ref[0, 0, :, :LANES] = jnp.where(low, hi, pltpu.roll(hi, HEAD_DIM, axis=1)).astype(BF16)
    ok_ref[0, 0, :, LANES:] = lo.astype(BF16)
    ov_ref[0, 0] = pltpu.roll(out, HEAD_DIM, axis=1).astype(BF16)


def _compress(xc, w1, w2, pos, layer, n_valid):
    b, four, ncp, _ = xc.shape
    return pl.pallas_call(
        functools.partial(_compress_body, n_valid=n_valid),
        grid=(b, four),
        in_specs=[
            pl.BlockSpec((1, 1, ncp, 1024), lambda i, j: (i, j, 0, 0)),
            pl.BlockSpec((None, 1, 1024, 256), lambda i, j: (layer, j // 2, 0, 0)),
            pl.BlockSpec((None, 1, CMP_HIDDEN, LANES), lambda i, j: (layer, j // 2, 0, 0)),
            pl.BlockSpec((None, 1, 8, 1024), lambda i, j: (layer, j // 2, 0, 0)),
        ],
        out_specs=[pl.BlockSpec((1, 1, ncp, 256), lambda i, j: (i, j, 0, 0)),
                   pl.BlockSpec((1, 1, ncp, LANES), lambda i, j: (i, j, 0, 0))],
        out_shape=[jax.ShapeDtypeStruct((b, four, ncp, 256), BF16),
                   jax.ShapeDtypeStruct((b, four, ncp, LANES), BF16)],
        compiler_params=_cparams(("parallel", "parallel")),
        name="compress",
    )(xc, w1, w2, pos)


def _softmax_step(carry, s3, pv):
    m_old, acc = carry
    heads, tq = s3.shape[0], s3.shape[1]
    m_new = jnp.maximum(m_old, jnp.max(s3, axis=-1, keepdims=True))
    p = jnp.exp2((s3 - m_new).astype(BF16))
    alpha = jnp.exp2(m_old - m_new)
    return m_new, alpha.reshape(heads * tq, 1) * acc + pv(p)


def _online_init(heads, tq):
    return jnp.full((heads, tq, 1), NEG_BIG, F32), jnp.zeros((heads * tq, LANES), F32)


def _online_finish(carry):
    _, acc = carry
    return acc / acc[:, 0:1]


def _pv_shared(v1):
    return lambda p: jnp.dot(p.reshape(p.shape[0] * p.shape[1], p.shape[2]), v1, preferred_element_type=F32)


def _store_heads(o_ref, o, heads, col0=0):
    tq = o.shape[0] // heads
    low = _lane_iota((tq, LANES)) < HEAD_DIM
    for pair in range(heads // 2):
        even = o[tq * (2 * pair):tq * (2 * pair + 1)]
        odd = o[tq * (2 * pair + 1):tq * (2 * pair + 2)]
        piece = jnp.where(low, pltpu.roll(even, HEAD_DIM, axis=1), odd)
        o_ref[0, :, col0 + LANES * pair:col0 + LANES * (pair + 1)] = piece.astype(o_ref.dtype)


def _topn_mask(score, n_pick, axis=-1):
    axis = axis % score.ndim
    idx = lax.broadcasted_iota(jnp.int32, score.shape, axis)
    width = score.shape[axis]
    sel = jnp.zeros(score.shape, F32)
    for _ in range(n_pick):
        m = jnp.max(score, axis=axis, keepdims=True)
        first = jnp.min(jnp.where(score == m, idx, width), axis=axis, keepdims=True)
        hit = idx == first
        sel = jnp.where(hit, 1.0, sel)
        score = jnp.where(hit, -jnp.inf, score)
    return sel


_NEG_INF_KEY = -2139095041


def _key_of(x):
    bits = lax.bitcast_convert_type(x, jnp.int32)
    return jnp.where(bits < 0, bits ^ jnp.int32(0x7FFFFFFF), bits)


def _float_of(key):
    return lax.bitcast_convert_type(jnp.where(key < 0, key ^ jnp.int32(0x7FFFFFFF), key), F32)


def _dsa_body(iq_ref, ik_ref, misc_ref, aq_ref, akv_ref, av1_ref, o_ref, key_scr, *, topk, seq):
    TQ = DSA_TQ
    c = pl.program_id(1)
    q0 = c * TQ
    nchunk = (q0 + TQ + KC - 1) // KC
    qpos = q0 + _lane_iota((1, TQ))
    krow = lax.broadcasted_iota(jnp.int32, (KC, 1), 0)
    srow = lax.broadcasted_iota(jnp.int32, (COUNT_KEYS, 1), 0)
    idx_scale = (IDX_HEADS * IDX_DIM) ** -0.5

    iq = jnp.concatenate([iq_ref[0, :, 256 * h:256 * (h + 1)] for h in range(IDX_HEADS)], axis=0)
    misc_t = misc_ref[0].T
    iw = [misc_t[MISC_IW + h:MISC_IW + h + 1] * idx_scale for h in range(IDX_HEADS)]

    def score_body(j, carry):
        mx, mn = carry
        ks = pl.multiple_of(j * KC, KC)
        lg = jnp.maximum(_dot_nt(ik_ref[0, pl.ds(ks, KC), :], iq), 0.0)
        sc = iw[0] * lg[:, 0:TQ]
        for h in range(1, IDX_HEADS):
            sc = sc + iw[h] * lg[:, TQ * h:TQ * (h + 1)]
        sc = jnp.where(sc == 0.0, 0.0, sc)
        part = sc.reshape(KC // 8, 8, TQ)
        mx, mn = jnp.maximum(mx, jnp.max(part, axis=0)), jnp.minimum(mn, jnp.min(part, axis=0))
        key_scr[j] = _key_of(jnp.where(ks + krow <= qpos, sc, -jnp.inf))
        return mx, mn

    mx, mn = lax.fori_loop(0, nchunk, score_body,
                           (jnp.full((8, TQ), -jnp.inf, F32), jnp.full((8, TQ), jnp.inf, F32)))

    def count(*hits):
        def body(j, accs):
            accs = list(accs)
            for g in range(KC // COUNT_KEYS):
                k = key_scr[j, COUNT_KEYS * g:COUNT_KEYS * (g + 1), :]
                for i, hit in enumerate(hits):
                    one = jnp.where(hit(k, j * KC + COUNT_KEYS * g), 1.0, 0.0)
                    accs[i] = accs[i] + jnp.sum(one.reshape(COUNT_KEYS // 8, 8, TQ), axis=0)
            return tuple(accs)
        accs = lax.fori_loop(0, nchunk, body, tuple(jnp.zeros((8, TQ), F32) for _ in hits))
        out = tuple(jnp.sum(a, axis=0, keepdims=True) for a in accs)
        return out if len(out) > 1 else out[0]

    n_fin = (qpos + 1).astype(F32)
    crowded = n_fin > topk
    c_pos, c_nn = count(lambda k, p0: k > 0, lambda k, p0: k >= 0)
    positive = c_pos >= topk
    zero_tie = crowded & jnp.logical_not(positive) & (c_nn >= topk)
    lo0 = jnp.where(positive, 1, _key_of(jnp.min(mn, axis=0, keepdims=True)))
    hi0 = jnp.where(positive, _key_of(jnp.max(mx, axis=0, keepdims=True)) + 1, 0)
    c_lo0 = jnp.where(positive, c_pos, n_fin)
    c_hi0 = jnp.where(positive, 0.0, c_nn)
    lo0 = jnp.where(zero_tie, 0, lo0)
    c_hi0 = jnp.where(zero_tie, c_pos, c_hi0)
    found0 = positive & (c_pos == topk)
    act0 = jnp.where(crowded & jnp.logical_not(zero_tie | found0) & (lo0 + 1 < hi0), 1.0, 0.0)
    below0 = jnp.where(found0, 0, _NEG_INF_KEY)
    log_target = math.log(topk - 0.5)

    def search_round(st):
        it, _, lo, hi, c_lo, c_hi, act, found, below = st
        lo_f, hi_f = _float_of(lo), _float_of(hi)
        la, lb = jnp.log(c_lo), jnp.log(jnp.maximum(c_hi, 0.5))
        frac = jnp.clip((la - log_target) / (la - lb), 0.02, 0.98)
        phase = it % 3
        guess = jnp.where(phase == 0, lo_f + (hi_f - lo_f) * frac, (lo_f + hi_f) * 0.5)
        mid = (lo >> 1) + (hi >> 1) + (lo & hi & 1)
        cand = jnp.clip(jnp.where(phase == 2, mid, _key_of(guess)), lo + 1, hi - 1)
        c = count(lambda k, p0: k >= cand)
        live = act > 0.5
        exact = live & (c == topk)
        up = live & (c > topk)
        dn = live & (c < topk)
        below = jnp.where(exact, cand - 1, below)
        found = jnp.where(exact, 1.0, found)
        lo, c_lo = jnp.where(up, cand, lo), jnp.where(up, c, c_lo)
        hi, c_hi = jnp.where(dn, cand, hi), jnp.where(dn, c, c_hi)
        act = jnp.where(live & jnp.logical_not(exact) & (lo + 1 < hi), 1.0, 0.0)
        return it + 1, jnp.sum(act), lo, hi, c_lo, c_hi, act, found, below

    st = lax.while_loop(lambda st: (st[0] < 100) & (st[1] > 0.5), search_round,
                        (jnp.int32(0), jnp.sum(act0), lo0, hi0, c_lo0, c_hi0, act0,
                         jnp.where(found0, 1.0, 0.0), below0))
    _, _, lo, _, _, c_hi, _, found, below = st

    tied = crowded & (found < 0.5)
    thr = jnp.where(tied, lo, below)
    need = topk - c_hi
    any_tie = jnp.max(jnp.where(tied, 1.0, 0.0)) > 0.5

    def tie_cut():
        def pos_body(i, cut):
            cand = cut + lax.shift_left(jnp.int32(1), (seq.bit_length() - 1) - i)
            cnt = count(lambda k, p0: (k == thr) & (p0 + srow < cand))
            return jnp.where(cnt < need, cand, cut)
        return lax.fori_loop(0, seq.bit_length(), pos_body, jnp.zeros((1, TQ), jnp.int32))

    cut = lax.cond(any_tie, tie_cut, lambda: jnp.full((1, TQ), -1, jnp.int32))
    cut = jnp.where(tied, cut, -1)

    q = jnp.concatenate([aq_ref[0, :, LANES * h:LANES * (h + 1)] for h in range(A_HEADS)], axis=0)

    def att_body(j, carry):
        ks = pl.multiple_of(j * KC, KC)
        kv = akv_ref[0, pl.ds(ks, KC), :]
        k = key_scr[j]
        taken = (k > thr) | ((k == thr) & (ks + krow <= cut))
        bias = jnp.where(taken, 0.0, NEG_BIG).T
        s3 = _dot_nt(q, kv).reshape(A_HEADS, TQ, KC) + bias[None]
        return _softmax_step(carry, s3, _pv_shared(av1_ref[0, pl.ds(ks, KC), :]))

    carry = lax.fori_loop(0, nchunk, att_body, _online_init(A_HEADS, TQ))
    _store_heads(o_ref, _online_finish(carry), A_HEADS)


def _dsa(p, batch, seq):
    topk = min(DSA_TOPK_MAX, seq // 4)
    r3 = lambda a: a.reshape(batch, seq, a.shape[-1])
    qblk = lambda c: pl.BlockSpec((1, DSA_TQ, c), lambda b, i: (b, i, 0))
    full = lambda c: pl.BlockSpec((1, seq, c), lambda b, i: (b, 0, 0))
    return pl.pallas_call(
        functools.partial(_dsa_body, topk=topk, seq=seq),
        grid=(batch, seq // DSA_TQ),
        in_specs=[qblk(4 * 256), full(256), qblk(LANES), qblk(4 * LANES), full(LANES), full(LANES)],
        out_specs=qblk(A_HEADS * HEAD_DIM),
        out_shape=jax.ShapeDtypeStruct((batch, seq, A_HEADS * HEAD_DIM), BF16),
        scratch_shapes=[pltpu.VMEM((seq // KC, KC, DSA_TQ), jnp.int32)],
        compiler_params=_cparams(("parallel", "arbitrary")),
        name="dsa",
    )(r3(p["iq"]), r3(p["ik"]), r3(p["misc"]), r3(p["aq"]), r3(p["akv"]), r3(p["av1"]))


def _pv_per_head(v1s):
    return lambda p: jnp.concatenate(
        [jnp.dot(p[h], v1s[h], preferred_element_type=F32) for h in range(len(v1s))], axis=0)


def _moba_body(bq_ref, bqp_ref, bkv_ref, bv1_ref, km_ref, hot_ref, o_ref, *, nbp):
    TQ = MOBA_TQ
    c = pl.program_id(1)
    q0 = c * TQ
    own = q0 // MOBA_BLOCK
    row = q0 + lax.broadcasted_iota(jnp.int32, (TQ, 1), 0)
    col = _lane_iota((1, MOBA_BLOCK))
    ks_own = pl.multiple_of(own * MOBA_BLOCK, MOBA_BLOCK)
    heads = range(B_HEADS)
    kv_at = lambda ks, h: bkv_ref[0, pl.ds(ks, MOBA_BLOCK), LANES * h:LANES * (h + 1)]
    v1_at = lambda ks, h: bv1_ref[0, pl.ds(ks, MOBA_BLOCK), LANES * h:LANES * (h + 1)]

    blk = lax.broadcasted_iota(jnp.int32, (nbp, 1), 0)
    gate = jnp.concatenate(
        [_dot_nt(km_ref[0, :nbp, HEAD_DIM * h:HEAD_DIM * (h + 1)], bq_ref[0, :, HEAD_DIM * h:HEAD_DIM * (h + 1)],
                 precision=HIGHEST) for h in heads], axis=1)
    picked = (blk < own) & (_topn_mask(jnp.where(blk < own, gate, -jnp.inf), MOBA_TOPK, axis=0) > 0.0)
    bias_t = jnp.where(picked, 0.0, NEG_BIG)
    fill = jnp.zeros((LANES - nbp, TQ), F32)

    q_aug, s_own = [], []
    for h in heads:
        bias = jnp.concatenate([bias_t[:, TQ * h:TQ * (h + 1)], fill], axis=0).T
        q = bqp_ref[0, :, LANES * h:LANES * (h + 1)]
        q_aug.append(jnp.concatenate([q, bias.astype(BF16)], axis=1))
        s_own.append(jnp.where(ks_own + col <= row, _dot_nt(q, kv_at(ks_own, h)), NEG_BIG))

    stack = lambda xs: jnp.concatenate(xs, axis=0).reshape(B_HEADS, TQ, MOBA_BLOCK)
    carry = _softmax_step(_online_init(B_HEADS, TQ), stack(s_own),
                          _pv_per_head([v1_at(ks_own, h) for h in heads]))

    def scores(j):
        ks = pl.multiple_of(j * MOBA_BLOCK, MOBA_BLOCK)
        hot = hot_ref[pl.ds(ks, MOBA_BLOCK), :]
        return stack([_dot_nt(q_aug[h], jnp.concatenate([kv_at(ks, h), hot], axis=1)) for h in heads])

    def body(j, carry):
        state, s_cur = carry
        s_next = scores(jnp.minimum(j + 1, own - 1))
        ks = pl.multiple_of(j * MOBA_BLOCK, MOBA_BLOCK)
        return _softmax_step(state, s_cur, _pv_per_head([v1_at(ks, h) for h in heads])), s_next

    carry, _ = lax.fori_loop(0, own, body, (carry, scores(0)))
    _store_heads(o_ref, _online_finish(carry), B_HEADS)


def _moba(p, hot, batch, seq):
    nb = seq // MOBA_BLOCK
    r3 = lambda a: a.reshape(batch, seq, a.shape[-1])
    km = p["km"][:, :PROJ_TM // MOBA_BLOCK].reshape(batch, nb, 256)
    km = jnp.pad(km, ((0, 0), (0, LANES - nb), (0, 0)))
    qblk = lambda c: pl.BlockSpec((1, MOBA_TQ, c), lambda b, i: (b, i, 0))
    return pl.pallas_call(
        functools.partial(_moba_body, nbp=min(LANES, -(-nb // 8) * 8)),
        grid=(batch, seq // MOBA_TQ),
        in_specs=[qblk(256), qblk(4 * LANES),
                  pl.BlockSpec((1, seq, 4 * LANES), lambda b, i: (b, 0, 0)),
                  pl.BlockSpec((1, seq, 4 * LANES), lambda b, i: (b, 0, 0)),
                  pl.BlockSpec((1, LANES, 256), lambda b, i: (b, 0, 0)),
                  pl.BlockSpec((seq, LANES), lambda b, i: (0, 0))],
        out_specs=qblk(B_HEADS * HEAD_DIM),
        out_shape=jax.ShapeDtypeStruct((batch, seq, B_HEADS * HEAD_DIM), BF16),
        compiler_params=_cparams(("parallel", "arbitrary")),
        name="moba",
    )(r3(p["bq"]), r3(p["bqp"]), r3(p["bkv"]), r3(p["bv1"]), km, hot)


def _nsa_body(cqn_ref, cqr_ref, misc_ref, kc_ref, vc_ref, cslc_ref, cwin_ref, cvs1_ref, cvw1_ref, ov_ref, hot_ref,
              o_ref,
              *, ncp, n_sel):
    TQ = NSA_TQ
    c = pl.program_id(1)
    q0 = c * TQ
    last = (q0 + TQ - 1) // KC
    ks_last = pl.multiple_of(last * KC, KC)
    row = q0 + lax.broadcasted_iota(jnp.int32, (TQ, 1), 0)
    col = _lane_iota((1, KC))
    gates = jax.nn.sigmoid(misc_ref[0])
    cmp_vis = _lane_iota((1, ncp)) * CMP_STRIDE + (CMP_BLOCK - 1) <= row
    wstart = pl.multiple_of(jnp.maximum(q0 - WINDOW, 0), TQ)
    wlen = WINDOW + TQ
    wdiff = row - (wstart + _lane_iota((1, wlen)))
    wmask = (wdiff >= 0) & (wdiff < WINDOW)

    o_cmp, imp_t = [], []
    for g in range(C_GROUPS):
        qn = jnp.concatenate([cqn_ref[0, :, 256 * (C_REP * g + r):256 * (C_REP * g + r + 1)]
                              for r in range(C_REP)], axis=0)
        s3 = _dot_nt(qn, kc_ref[0, g]).reshape(C_REP, TQ, ncp)
        m = jnp.max(jnp.where(cmp_vis[None], s3, -jnp.inf), axis=-1, keepdims=True)
        m = jnp.where(m > -jnp.inf, m, 0.0)
        pc = jnp.where(cmp_vis[None], jnp.exp2(s3 - m), 0.0)
        den = jnp.sum(pc, axis=-1, keepdims=True)
        pc = pc / jnp.where(den > 0, den, 1.0)
        o_cmp.append(jnp.dot(pc.reshape(C_REP * TQ, ncp).astype(BF16), vc_ref[0, g],
                             preferred_element_type=F32))
        psum = pc[0]
        for r in range(1, C_REP):
            psum = psum + pc[r]
        hi = psum.astype(BF16)
        rest = psum - hi.astype(F32)
        mid = rest.astype(BF16)
        low = (rest - mid.astype(F32)).astype(BF16)
        imp = jnp.dot(jnp.concatenate([hi, mid, low], axis=1), ov_ref[...], preferred_element_type=F32)
        imp_t.append(imp.T)

    blk = lax.broadcasted_iota(jnp.int32, (LANES, 1), 0)
    jq = (q0 + _lane_iota((1, C_GROUPS * TQ)) % TQ) // SLC_BLOCK
    adm = blk <= jq
    forced = adm & ((blk == 0) | (blk == jq) | (blk == jq - 1))
    score = jnp.where(forced, FORCE_SCORE, jnp.where(adm, jnp.concatenate(imp_t, axis=1), -jnp.inf))
    picked = adm & (_topn_mask(score, n_sel, axis=0) > 0.0)
    bias_t = jnp.where(picked, 0.0, NEG_BIG)

    for g in range(C_GROUPS):
        o_c = o_cmp[g]
        bias = bias_t[:, TQ * g:TQ * (g + 1)].T.astype(BF16)

        qr = jnp.concatenate([cqr_ref[0, :, LANES * (C_REP * g + r):LANES * (C_REP * g + r + 1)]
                              for r in range(C_REP)], axis=0)
        qa = jnp.concatenate([qr, jnp.concatenate([bias] * C_REP, axis=0)], axis=1)
        kv_at = lambda j: cslc_ref[0, pl.ds(pl.multiple_of(j * KC, KC), KC), LANES * g:LANES * (g + 1)]
        v1_at = lambda j: cvs1_ref[0, pl.ds(pl.multiple_of(j * KC, KC), KC), LANES * g:LANES * (g + 1)]
        slc_scores = lambda j: _dot_nt(qa, jnp.concatenate(
            [kv_at(j), hot_ref[pl.ds(pl.multiple_of(j * KC, KC), KC), :]], axis=1))

        def slc_body(j, carry):
            return _softmax_step(carry, slc_scores(j).reshape(C_REP, TQ, KC), _pv_shared(v1_at(j)))

        carry = lax.fori_loop(0, last, slc_body, _online_init(C_REP, TQ))
        s3 = jnp.where((ks_last + col <= row)[None], slc_scores(last).reshape(C_REP, TQ, KC), NEG_BIG)
        o_s = _online_finish(_softmax_step(carry, s3, _pv_shared(v1_at(last))))

        kvw = cwin_ref[0, pl.ds(wstart, wlen), LANES * g:LANES * (g + 1)]
        s3 = jnp.where(wmask[None], _dot_nt(qr, kvw).reshape(C_REP, TQ, wlen), NEG_BIG)
        v1w = cvw1_ref[0, pl.ds(wstart, wlen), LANES * g:LANES * (g + 1)]
        o_w = _online_finish(_softmax_step(_online_init(C_REP, TQ), s3, _pv_shared(v1w)))

        outs = []
        for r in range(C_REP):
            hh = C_REP * g + r
            gcol = lambda j: gates[:, MISC_CG + 3 * hh + j:MISC_CG + 3 * hh + j + 1]
            rows = slice(TQ * r, TQ * (r + 1))
            outs.append(gcol(0) * o_c[rows] + gcol(1) * o_s[rows] + gcol(2) * o_w[rows])
        _store_heads(o_ref, jnp.concatenate(outs, axis=0), C_REP, col0=C_REP * HEAD_DIM * g)


def _nsa(p, kcmp, vcmp, overlap, hot, batch, seq):
    ncp = seq // CMP_STRIDE
    n_sel = min(SLC_TOPN, seq // SLC_BLOCK)
    r3 = lambda a: a.reshape(batch, seq, a.shape[-1])
    qblk = lambda c: pl.BlockSpec((1, NSA_TQ, c), lambda b, i: (b, i, 0))
    full = lambda c: pl.BlockSpec((1, seq, c), lambda b, i: (b, 0, 0))
    return pl.pallas_call(
        functools.partial(_nsa_body, ncp=ncp, n_sel=n_sel),
        grid=(batch, seq // NSA_TQ),
        in_specs=[qblk(8 * 256), qblk(8 * LANES), qblk(LANES),
                  pl.BlockSpec((1, C_GROUPS, ncp, 256), lambda b, i: (b, 0, 0, 0)),
                  pl.BlockSpec((1, C_GROUPS, ncp, LANES), lambda b, i: (b, 1, 0, 0)),
                  full(2 * LANES), full(2 * LANES), full(2 * LANES), full(2 * LANES),
                  pl.BlockSpec((3 * ncp, LANES), lambda b, i: (0, 0)),
                  pl.BlockSpec((seq, LANES), lambda b, i: (0, 0))],
        out_specs=qblk(C_HEADS * HEAD_DIM),
        out_shape=jax.ShapeDtypeStruct((batch, seq, C_HEADS * HEAD_DIM), BF16),
        compiler_params=_cparams(("parallel", "arbitrary")),
        name="nsa",
    )(r3(p["cqn"]), r3(p["cqr"]), r3(p["misc"]), kcmp, vcmp, r3(p["cslc"]), r3(p["cwin"]),
      r3(p["cvs1"]), r3(p["cvw1"]), overlap, hot)


def _merge_body(x_ref, g_ref, oa_ref, ob_ref, oc_ref, wm_ref, wa_ref, wb_ref, wc_ref, wo_ref, o_ref):
    x = x_ref[...]
    d = x.shape[1]
    h = _rms(x, g_ref[...]).astype(BF16)
    merged = None
    for i, (o_r, w_r) in enumerate(((oa_ref, wa_ref), (ob_ref, wb_ref), (oc_ref, wc_ref))):
        gate = jax.nn.sigmoid(jnp.dot(h, wm_ref[:, d * i:d * (i + 1)], preferred_element_type=F32))
        y = gate * jnp.dot(o_r[...], w_r[...], preferred_element_type=F32)
        merged = y if merged is None else merged + y
    o_ref[...] = x + jnp.dot(merged.astype(BF16), wo_ref[...], preferred_element_type=F32)


def _merge(x2, g, oa, ob, oc, wm, wa, wb, wc, wo, layer):
    m, d = x2.shape
    tm = MERGE_TM
    row = lambda c: pl.BlockSpec((tm, c), lambda i: (i, 0))
    const = lambda a: pl.BlockSpec(a.shape, lambda i: (0, 0))
    lay = lambda a: pl.BlockSpec((None,) + a.shape[1:], lambda i: (layer, 0, 0))
    return pl.pallas_call(
        _merge_body,
        grid=(m // tm,),
        in_specs=[row(d), const(g), row(oa.shape[1]), row(ob.shape[1]), row(oc.shape[1]),
                  lay(wm), lay(wa), lay(wb), lay(wc), lay(wo)],
        out_specs=row(d),
        out_shape=jax.ShapeDtypeStruct((m, d), F32),
        compiler_params=_cparams(("parallel",)),
        name="merge",
    )(x2, g, oa, ob, oc, wm, wa, wb, wc, wo)


def _rope_tables(seq):
    half = HEAD_DIM // 2
    inv = ROPE_THETA ** (-jnp.arange(half, dtype=F32) / half)
    ang = jnp.arange(seq, dtype=F32)[:, None] * inv[None, :]
    lane = np.arange(LANES)
    ang = ang[:, lane % half]
    second = jnp.asarray((lane % HEAD_DIM) >= half)[None, :]
    sin = jnp.sin(ang)
    return jnp.cos(ang), jnp.where(second, sin, 0.0), jnp.where(second, 0.0, -sin)


def _overlap(seq):
    ncp = seq // CMP_STRIDE
    nc = (seq - CMP_BLOCK) // CMP_STRIDE + 1
    ns = seq // SLC_BLOCK
    cs = np.arange(ncp) * CMP_STRIDE
    ss = np.arange(ns) * SLC_BLOCK
    ov = (cs[:, None] < ss[None, :] + SLC_BLOCK) & (ss[None, :] <= cs[:, None] + CMP_BLOCK - 1)
    ov &= (np.arange(ncp) < nc)[:, None]
    out = np.zeros((ncp, LANES), np.float32)
    out[:, :ns] = ov
    return jnp.asarray(np.tile(out, (3, 1)), BF16)


def _block_onehot(seq, block):
    return jnp.asarray(np.arange(seq)[:, None] // block == np.arange(LANES)[None, :], BF16)


def kernel(x, ffn1_norm, ffn1_w_gate, ffn1_w_up, ffn1_w_down, mix_norm, w_in, cmp_pos_k, cmp_w1_k, cmp_w2_k, cmp_pos_v, cmp_w1_v, cmp_w2_v, w_branch_a, w_branch_b, w_branch_c, w_out, ffn2_norm, ffn2_w_gate, ffn2_w_up, ffn2_w_down, final_norm):
    batch, seq, d = x.shape
    depth = w_in.shape[0]
    assert seq % max(KC, PROJ_TM) == 0 and WINDOW + NSA_TQ <= seq <= SLC_BLOCK * LANES and w_in.shape[2] == _N_IN
    nc = (seq - CMP_BLOCK) // CMP_STRIDE + 1
    ncp = seq // CMP_STRIDE

    bf = lambda a: a.astype(BF16)
    zero_cols = lambda n: jnp.zeros(w_in.shape[:2] + (n,), w_in.dtype)
    w_perm = bf(jnp.concatenate([w_in[:, :, a:a + n] if a >= 0 else zero_cols(n) for a, n in _perm_runs()], axis=2))
    w_mg = bf(w_in[:, :, _O_MG:])
    ffn1 = (bf(ffn1_w_gate), bf(ffn1_w_up), bf(ffn1_w_down))
    ffn2 = (bf(ffn2_w_gate), bf(ffn2_w_up), bf(ffn2_w_down))
    w_a, w_b, w_c, w_o = bf(w_branch_a), bf(w_branch_b), bf(w_branch_c), bf(w_out)
    half_rows = CMP_BLOCK * HEAD_DIM // 2
    w1 = jnp.stack([cmp_w1_k, cmp_w1_v], axis=1)
    w1 = jnp.concatenate([w1[:, :, :half_rows], w1[:, :, half_rows:]], axis=-1)
    w2 = jnp.pad(jnp.stack([cmp_w2_k, cmp_w2_v], axis=1), ((0, 0), (0, 0), (0, 0), (0, LANES - HEAD_DIM)))
    pos = jnp.stack([cmp_pos_k, cmp_pos_v], axis=1).reshape(depth, 2, 2, half_rows)
    pos = jnp.pad(pos, ((0, 0), (0, 0), (0, 6), (0, 0)))
    cos, sa, sb = _rope_tables(seq)
    overlap = _overlap(seq)
    hot_b = _block_onehot(seq, MOBA_BLOCK)
    hot_s = _block_onehot(seq, SLC_BLOCK)
    row = lambda a: a.reshape(1, d)

    x2 = x.reshape(batch * seq, d)
    for l in range(depth):
        x2 = _ffn(x2, row(ffn1_norm[l]), *ffn1, row(final_norm), l, False)
        p = _inproj(x2, row(mix_norm[l]), w_perm, l, cos, sa, sb, seq)
        xc = p["ccmp"].reshape(batch, seq, 4, HEAD_DIM).transpose(0, 2, 1, 3).reshape(batch, 4, ncp, 1024)
        kc, vc = _compress(xc, w1, w2, pos, l, nc)
        o_a = _dsa(p, batch, seq)
        o_b = _moba(p, hot_b, batch, seq)
        o_c = _nsa(p, kc, vc, overlap, hot_s, batch, seq)
        flat = lambda a: a.reshape(batch * seq, a.shape[-1])
        x2 = _merge(x2, row(mix_norm[l]), flat(o_a), flat(o_b), flat(o_c), w_mg, w_a, w_b, w_c, w_o, l)
        x2 = _ffn(x2, row(ffn2_norm[l]), *ffn2, row(final_norm), l, l == depth - 1)
    return x2.reshape(batch, seq, d)
```

```python
import functools
import math

import numpy as np
import jax
import jax.numpy as jnp
from jax import lax
from jax.experimental import pallas as pl
from jax.experimental.pallas import tpu as pltpu

HEAD_DIM = 64
ROPE_THETA = 10000.0
NORM_EPS = 1e-6
A_HEADS = 4
IDX_HEADS = 4
IDX_DIM = 64
DSA_TOPK_MAX = 256
B_HEADS = 4
MOBA_BLOCK = 256
MOBA_TOPK = 3
C_HEADS = 8
C_GROUPS = 2
C_REP = C_HEADS // C_GROUPS
CMP_BLOCK = 32
CMP_STRIDE = 16
CMP_HIDDEN = 128
SLC_BLOCK = 64
SLC_TOPN = 16
WINDOW = 512
FORCE_SCORE = 1e30

LANES = 128
VMEM_LIMIT = 56 * 1024 * 1024

DSA_TQ = 256
MOBA_TQ = 256
NSA_TQ = 256
KC = 512
COUNT_KEYS = 128
FFN_TM = 512
PROJ_TM = 512
MERGE_TM = 512

NEG_BIG = -1e30
QSCALE = HEAD_DIM ** -0.5 * math.log2(math.e)
F32 = jnp.float32
BF16 = jnp.bfloat16
HIGHEST = lax.Precision.HIGHEST

_O_AQ = 0
_O_AK = 256
_O_AV = 320
_O_IQ = 384
_O_IK = 640
_O_IW = 704
_O_BQ = 708
_O_BK = 964
_O_BV = 1220
_O_CQ = 1476
_O_CKV = 1988
_O_CG = 2756
_O_MG = 2780
_N_IN = 5852

P_AQ = 0
P_AK = 256
P_IK = 320
P_IQ = 384
P_BQ = 640
P_BK = 896
P_CKS = 1152
P_CKW = 1280
P_CQ = 1408
P_ROPE_END = 1920
P_BV = 1920
P_KCMP = 2176
P_VS = 2432
P_VW = 2560
P_AV = 2688
P_MISC = 2752
P_COLS = 2816
MISC_CG = 0
MISC_IW = 24


def _in_perm():
    perm = -np.ones((P_COLS,), np.int64)

    def put(dst, src, n):
        perm[dst:dst + n] = np.arange(src, src + n)

    put(P_AQ, _O_AQ, 256)
    put(P_AK, _O_AK, 64)
    put(P_IK, _O_IK, 64)
    put(P_IQ, _O_IQ, 256)
    put(P_BQ, _O_BQ, 256)
    put(P_BK, _O_BK, 256)
    ckv = lambda s, g: _O_CKV + (s * C_GROUPS + g) * HEAD_DIM
    for g in range(C_GROUPS):
        put(P_CKS + 64 * g, ckv(2, g), 64)
        put(P_CKW + 64 * g, ckv(4, g), 64)
        put(P_KCMP + 64 * g, ckv(0, g), 64)
        put(P_KCMP + 128 + 64 * g, ckv(1, g), 64)
        put(P_VS + 64 * g, ckv(3, g), 64)
        put(P_VW + 64 * g, ckv(5, g), 64)
    put(P_CQ, _O_CQ, 512)
    put(P_BV, _O_BV, 256)
    put(P_AV, _O_AV, 64)
    put(P_MISC + MISC_CG, _O_CG, 24)
    put(P_MISC + MISC_IW, _O_IW, 4)
    return perm


def _perm_runs():
    perm = _in_perm()
    runs, start = [], 0
    for i in range(1, len(perm) + 1):
        if i == len(perm) or (perm[i] != perm[i - 1] + 1 if perm[i - 1] >= 0 else perm[i] >= 0):
            runs.append((int(perm[start]), i - start))
            start = i
    return runs


def _cparams(sem):
    return pltpu.CompilerParams(dimension_semantics=sem, vmem_limit_bytes=VMEM_LIMIT)


def _rms(x, g):
    y = x * lax.rsqrt(jnp.mean(x * x, axis=-1, keepdims=True) + NORM_EPS)
    return y * g


def _dot_nt(a, b, precision=None):
    return lax.dot_general(a, b, (((1,), (1,)), ((), ())), precision=precision,
                           preferred_element_type=F32)


def _ffn_body(x_ref, g_ref, wg_ref, wu_ref, wd_ref, fg_ref, o_ref, *, final_norm):
    x = x_ref[...]
    h = _rms(x, g_ref[...]).astype(BF16)
    a = jnp.dot(h, wg_ref[...], preferred_element_type=F32)
    u = jnp.dot(h, wu_ref[...], preferred_element_type=F32)
    act = (a * jax.nn.sigmoid(a) * u).astype(BF16)
    y = x + 0.5 * jnp.dot(act, wd_ref[...], preferred_element_type=F32)
    if final_norm:
        y = _rms(y, fg_ref[...])
    o_ref[...] = y


def _ffn(x2, g, wg, wu, wd, fg, layer, final_norm):
    m, d = x2.shape
    dff = wg.shape[2]
    tm = FFN_TM
    return pl.pallas_call(
        functools.partial(_ffn_body, final_norm=final_norm),
        grid=(m // tm,),
        in_specs=[
            pl.BlockSpec((tm, d), lambda i: (i, 0)),
            pl.BlockSpec((1, d), lambda i: (0, 0)),
            pl.BlockSpec((None, d, dff), lambda i: (layer, 0, 0)),
            pl.BlockSpec((None, d, dff), lambda i: (layer, 0, 0)),
            pl.BlockSpec((None, dff, d), lambda i: (layer, 0, 0)),
            pl.BlockSpec((1, d), lambda i: (0, 0)),
        ],
        out_specs=pl.BlockSpec((tm, d), lambda i: (i, 0)),
        out_shape=jax.ShapeDtypeStruct((m, d), F32),
        compiler_params=_cparams(("parallel",)),
        name="ffn",
    )(x2, g, wg, wu, wd, fg)


def _lane_iota(shape):
    return lax.broadcasted_iota(jnp.int32, shape, len(shape) - 1)


def _split_hi_lo(x):
    hi = x.astype(BF16).astype(F32)
    return hi, x - hi


def _inproj_body(x_ref, g_ref, w_ref, cos_ref, sa_ref, sb_ref,
                 aq_ref, akv_ref, iq_ref, ik_ref, misc_ref, bq_ref, bqp_ref, bkv_ref, km_ref,
                 cqn_ref, cqr_ref, cslc_ref, cwin_ref, ccmp_ref, av1_ref, bv1_ref, cvs1_ref, cvw1_ref):
    tm = x_ref.shape[0]
    h = _rms(x_ref[...], g_ref[...]).astype(BF16)
    cos, sa, sb = cos_ref[...], sa_ref[...], sb_ref[...]
    lane = _lane_iota((tm, LANES))
    low = lane < HEAD_DIM

    raw = []
    rot = []
    for j in range(P_COLS // 256):
        z = jnp.dot(h, w_ref[:, 256 * j:256 * (j + 1)], preferred_element_type=F32)
        for half in range(2):
            p = z[:, LANES * half:LANES * (half + 1)]
            raw.append(p)
            if LANES * len(raw) <= P_ROPE_END:
                rot.append(p * cos + pltpu.roll(p, 32, axis=1) * sa + pltpu.roll(p, 96, axis=1) * sb)

    def pick(pieces, col):
        p = pieces[col // LANES]
        return pltpu.roll(p, HEAD_DIM, axis=1) if col % LANES else p

    def join(lo, hi=None):
        if hi is None:
            return jnp.where(low, lo, 0.0)
        return jnp.where(low, lo, pltpu.roll(hi, HEAD_DIM, axis=1))

    def one_v(v):
        return jnp.where(low, 1.0, pltpu.roll(v, HEAD_DIM, axis=1)).astype(BF16)

    scale = QSCALE

    for hh in range(A_HEADS):
        aq_ref[:, LANES * hh:LANES * (hh + 1)] = join(pick(rot, P_AQ + 64 * hh) * scale).astype(BF16)
        qhi, qlo = _split_hi_lo(pick(rot, P_IQ + 64 * hh))
        iq_ref[:, 256 * hh:256 * hh + LANES] = join(qhi, qlo).astype(BF16)
        iq_ref[:, 256 * hh + LANES:256 * (hh + 1)] = join(qhi).astype(BF16)
    akv_ref[...] = join(pick(rot, P_AK), pick(raw, P_AV)).astype(BF16)
    av1_ref[...] = one_v(pick(raw, P_AV))
    khi, klo = _split_hi_lo(pick(rot, P_IK))
    ik_ref[:, :LANES] = join(khi, khi).astype(BF16)
    ik_ref[:, LANES:] = join(klo).astype(BF16)
    misc_ref[...] = pick(raw, P_MISC)

    bq_ref[:, :LANES] = rot[P_BQ // LANES]
    bq_ref[:, LANES:] = rot[P_BQ // LANES + 1]
    for hh in range(B_HEADS):
        bqp_ref[:, LANES * hh:LANES * (hh + 1)] = join(pick(rot, P_BQ + 64 * hh) * scale).astype(BF16)
        bkv_ref[:, LANES * hh:LANES * (hh + 1)] = join(pick(rot, P_BK + 64 * hh),
                                                      pick(raw, P_BV + 64 * hh)).astype(BF16)
        bv1_ref[:, LANES * hh:LANES * (hh + 1)] = one_v(pick(raw, P_BV + 64 * hh))
    nblk = tm // MOBA_BLOCK
    km_ref[...] = jnp.zeros_like(km_ref)
    for half in range(2):
        kp = rot[P_BK // LANES + half]
        for b in range(nblk):
            km_ref[0, b:b + 1, LANES * half:LANES * (half + 1)] = jnp.mean(
                kp[MOBA_BLOCK * b:MOBA_BLOCK * (b + 1)], axis=0, keepdims=True)

    for hh in range(C_HEADS):
        nhi, nlo = _split_hi_lo(pick(raw, P_CQ + 64 * hh) * scale)
        cqn_ref[:, 256 * hh:256 * hh + LANES] = join(nhi, nlo).astype(BF16)
        cqn_ref[:, 256 * hh + LANES:256 * (hh + 1)] = join(nhi).astype(BF16)
        cqr_ref[:, LANES * hh:LANES * (hh + 1)] = join(pick(rot, P_CQ + 64 * hh) * scale).astype(BF16)
    for g in range(C_GROUPS):
        cslc_ref[:, LANES * g:LANES * (g + 1)] = join(pick(rot, P_CKS + 64 * g),
                                                     pick(raw, P_VS + 64 * g)).astype(BF16)
        cwin_ref[:, LANES * g:LANES * (g + 1)] = join(pick(rot, P_CKW + 64 * g),
                                                     pick(raw, P_VW + 64 * g)).astype(BF16)
        cvs1_ref[:, LANES * g:LANES * (g + 1)] = one_v(pick(raw, P_VS + 64 * g))
        cvw1_ref[:, LANES * g:LANES * (g + 1)] = one_v(pick(raw, P_VW + 64 * g))
    ccmp_ref[:, :LANES] = raw[P_KCMP // LANES]
    ccmp_ref[:, LANES:] = raw[P_KCMP // LANES + 1]


def _inproj(x2, g, w, layer, cos, sa, sb, seq):
    m, d = x2.shape
    tm = PROJ_TM
    nt = seq // tm
    row = lambda c: pl.BlockSpec((tm, c), lambda i: (i, 0))
    tab = pl.BlockSpec((tm, LANES), lambda i: (i % nt, 0))
    outs = [
        ("aq", 4 * LANES, BF16), ("akv", LANES, BF16), ("iq", 4 * 256, BF16), ("ik", 256, BF16),
        ("misc", LANES, F32), ("bq", 256, F32), ("bqp", 4 * LANES, BF16), ("bkv", 4 * LANES, BF16),
        ("km", None, F32),
        ("cqn", 8 * 256, BF16), ("cqr", 8 * LANES, BF16), ("cslc", 2 * LANES, BF16),
        ("cwin", 2 * LANES, BF16), ("ccmp", 256, F32),
        ("av1", LANES, BF16), ("bv1", 4 * LANES, BF16), ("cvs1", 2 * LANES, BF16), ("cvw1", 2 * LANES, BF16),
    ]
    out_specs, out_shape = [], []
    for name, c, dt in outs:
        if name == "km":
            out_specs.append(pl.BlockSpec((1, 8, 256), lambda i: (i, 0, 0)))
            out_shape.append(jax.ShapeDtypeStruct((m // tm, 8, 256), dt))
        else:
            out_specs.append(row(c))
            out_shape.append(jax.ShapeDtypeStruct((m, c), dt))
    res = pl.pallas_call(
        _inproj_body,
        grid=(m // tm,),
        in_specs=[row(d), pl.BlockSpec((1, d), lambda i: (0, 0)),
                  pl.BlockSpec((None, d, P_COLS), lambda i: (layer, 0, 0)), tab, tab, tab],
        out_specs=out_specs,
        out_shape=out_shape,
        compiler_params=_cparams(("parallel",)),
        name="inproj",
    )(x2, g, w, cos, sa, sb)
    return dict(zip([o[0] for o in outs], res))


def _compress_body(x_ref, w1_ref, w2_ref, pos_ref, ok_ref, ov_ref, *, n_valid):
    x = x_ref[0, 0]
    w1 = w1_ref[0]
    pre = jnp.dot(x, w1, precision=HIGHEST, preferred_element_type=F32)
    pp = jnp.dot(pos_ref[0], w1, precision=HIGHEST, preferred_element_type=F32)
    posb = pp[0:1, :CMP_HIDDEN] + pp[1:2, CMP_HIDDEN:]
    ncp = x.shape[0]
    upper = pre[:, :CMP_HIDDEN]
    lower_next = pltpu.roll(pre[:, CMP_HIDDEN:], ncp - 1, axis=0)
    hid = jax.nn.gelu(upper + lower_next + posb)
    out = jnp.dot(hid, w2_ref[0], precision=HIGHEST, preferred_element_type=F32)
    rows = lax.broadcasted_iota(jnp.int32, out.shape, 0)
    out = jnp.where(rows < n_valid, out, 0.0)
    low = _lane_iota(out.shape) < HEAD_DIM
    hi, lo = _split_hi_lo(out)
    ok_ref[0, 0, :, :LANES] = jnp.where(low, hi, pltpu.roll(hi, HEAD_DIM, axis=1)).astype(BF16)
    ok_ref[0, 0, :, LANES:] = lo.astype(BF16)
    ov_ref[0, 0] = pltpu.roll(out, HEAD_DIM, axis=1).astype(BF16)


def _compress(xc, w1, w2, pos, layer, n_valid):
    b, four, ncp, _ = xc.shape
    return pl.pallas_call(
        functools.partial(_compress_body, n_valid=n_valid),
        grid=(b, four),
        in_specs=[
            pl.BlockSpec((1, 1, ncp, 1024), lambda i, j: (i, j, 0, 0)),
            pl.BlockSpec((None, 1, 1024, 256), lambda i, j: (layer, j // 2, 0, 0)),
            pl.BlockSpec((None, 1, CMP_HIDDEN, LANES), lambda i, j: (layer, j // 2, 0, 0)),
            pl.BlockSpec((None, 1, 8, 1024), lambda i, j: (layer, j // 2, 0, 0)),
        ],
        out_specs=[pl.BlockSpec((1, 1, ncp, 256), lambda i, j: (i, j, 0, 0)),
                   pl.BlockSpec((1, 1, ncp, LANES), lambda i, j: (i, j, 0, 0))],
        out_shape=[jax.ShapeDtypeStruct((b, four, ncp, 256), BF16),
                   jax.ShapeDtypeStruct((b, four, ncp, LANES), BF16)],
        compiler_params=_cparams(("parallel", "parallel")),
        name="compress",
    )(xc, w1, w2, pos)


def _softmax_step(carry, s3, pv):
    m_old, acc = carry
    heads, tq = s3.shape[0], s3.shape[1]
    m_new = jnp.maximum(m_old, jnp.max(s3, axis=-1, keepdims=True))
    p = jnp.exp2((s3 - m_new).astype(BF16))
    alpha = jnp.exp2(m_old - m_new)
    return m_new, alpha.reshape(heads * tq, 1) * acc + pv(p)


def _online_init(heads, tq):
    return jnp.full((heads, tq, 1), NEG_BIG, F32), jnp.zeros((heads * tq, LANES), F32)


def _online_finish(carry):
    _, acc = carry
    return acc / acc[:, 0:1]


def _pv_shared(v1):
    return lambda p: jnp.dot(p.reshape(p.shape[0] * p.shape[1], p.shape[2]), v1, preferred_element_type=F32)


def _store_heads(o_ref, o, heads, col0=0):
    tq = o.shape[0] // heads
    low = _lane_iota((tq, LANES)) < HEAD_DIM
    for pair in range(heads // 2):
        even = o[tq * (2 * pair):tq * (2 * pair + 1)]
        odd = o[tq * (2 * pair + 1):tq * (2 * pair + 2)]
        piece = jnp.where(low, pltpu.roll(even, HEAD_DIM, axis=1), odd)
        o_ref[0, :, col0 + LANES * pair:col0 + LANES * (pair + 1)] = piece.astype(o_ref.dtype)


def _topn_mask(score, n_pick, axis=-1):
    axis = axis % score.ndim
    idx = lax.broadcasted_iota(jnp.int32, score.shape, axis)
    width = score.shape[axis]
    sel = jnp.zeros(score.shape, F32)
    for _ in range(n_pick):
        m = jnp.max(score, axis=axis, keepdims=True)
        first = jnp.min(jnp.where(score == m, idx, width), axis=axis, keepdims=True)
        hit = idx == first
        sel = jnp.where(hit, 1.0, sel)
        score = jnp.where(hit, -jnp.inf, score)
    return sel


_NEG_INF_KEY = -2139095041


def _key_of(x):
    bits = lax.bitcast_convert_type(x, jnp.int32)
    return jnp.where(bits < 0, bits ^ jnp.int32(0x7FFFFFFF), bits)


def _float_of(key):
    return lax.bitcast_convert_type(jnp.where(key < 0, key ^ jnp.int32(0x7FFFFFFF), key), F32)


def _dsa_body(iq_ref, ik_ref, misc_ref, aq_ref, akv_ref, av1_ref, o_ref, key_scr, *, topk, seq):
    TQ = DSA_TQ
    c = pl.program_id(1)
    q0 = c * TQ
    nchunk = (q0 + TQ + KC - 1) // KC
    qpos = q0 + _lane_iota((1, TQ))
    krow = lax.broadcasted_iota(jnp.int32, (KC, 1), 0)
    srow = lax.broadcasted_iota(jnp.int32, (COUNT_KEYS, 1), 0)
    idx_scale = (IDX_HEADS * IDX_DIM) ** -0.5

    iq = jnp.concatenate([iq_ref[0, :, 256 * h:256 * (h + 1)] for h in range(IDX_HEADS)], axis=0)
    misc_t = misc_ref[0].T
    iw = [misc_t[MISC_IW + h:MISC_IW + h + 1] * idx_scale for h in range(IDX_HEADS)]

    def score_body(j, carry):
        mx, mn = carry
        ks = pl.multiple_of(j * KC, KC)
        lg = jnp.maximum(_dot_nt(ik_ref[0, pl.ds(ks, KC), :], iq), 0.0)
        sc = iw[0] * lg[:, 0:TQ]
        for h in range(1, IDX_HEADS):
            sc = sc + iw[h] * lg[:, TQ * h:TQ * (h + 1)]
        sc = jnp.where(sc == 0.0, 0.0, sc)
        part = sc.reshape(KC // 8, 8, TQ)
        mx, mn = jnp.maximum(mx, jnp.max(part, axis=0)), jnp.minimum(mn, jnp.min(part, axis=0))
        key_scr[j] = _key_of(jnp.where(ks + krow <= qpos, sc, -jnp.inf))
        return mx, mn

    mx, mn = lax.fori_loop(0, nchunk, score_body,
                           (jnp.full((8, TQ), -jnp.inf, F32), jnp.full((8, TQ), jnp.inf, F32)))

    def count(*hits):
        def body(j, accs):
            accs = list(accs)
            for g in range(KC // COUNT_KEYS):
                k = key_scr[j, COUNT_KEYS * g:COUNT_KEYS * (g + 1), :]
                for i, hit in enumerate(hits):
                    one = jnp.where(hit(k, j * KC + COUNT_KEYS * g), 1.0, 0.0)
                    accs[i] = accs[i] + jnp.sum(one.reshape(COUNT_KEYS // 8, 8, TQ), axis=0)
            return tuple(accs)
        accs = lax.fori_loop(0, nchunk, body, tuple(jnp.zeros((8, TQ), F32) for _ in hits))
        out = tuple(jnp.sum(a, axis=0, keepdims=True) for a in accs)
        return out if len(out) > 1 else out[0]

    n_fin = (qpos + 1).astype(F32)
    crowded = n_fin > topk
    c_pos, c_nn = count(lambda k, p0: k > 0, lambda k, p0: k >= 0)
    positive = c_pos >= topk
    zero_tie = crowded & jnp.logical_not(positive) & (c_nn >= topk)
    lo0 = jnp.where(positive, 1, _key_of(jnp.min(mn, axis=0, keepdims=True)))
    hi0 = jnp.where(positive, _key_of(jnp.max(mx, axis=0, keepdims=True)) + 1, 0)
    c_lo0 = jnp.where(positive, c_pos, n_fin)
    c_hi0 = jnp.where(positive, 0.0, c_nn)
    lo0 = jnp.where(zero_tie, 0, lo0)
    c_hi0 = jnp.where(zero_tie, c_pos, c_hi0)
    found0 = positive & (c_pos == topk)
    act0 = jnp.where(crowded & jnp.logical_not(zero_tie | found0) & (lo0 + 1 < hi0), 1.0, 0.0)
    below0 = jnp.where(found0, 0, _NEG_INF_KEY)
    log_target = math.log(topk - 0.5)

    def search_round(st):
        it, _, lo, hi, c_lo, c_hi, act, found, below = st
        lo_f, hi_f = _float_of(lo), _float_of(hi)
        la, lb = jnp.log(c_lo), jnp.log(jnp.maximum(c_hi, 0.5))
        frac = jnp.clip((la - log_target) / (la - lb), 0.02, 0.98)
        phase = it % 3
        guess = jnp.where(phase == 0, lo_f + (hi_f - lo_f) * frac, (lo_f + hi_f) * 0.5)
        mid = (lo >> 1) + (hi >> 1) + (lo & hi & 1)
        cand = jnp.clip(jnp.where(phase == 2, mid, _key_of(guess)), lo + 1, hi - 1)
        c = count(lambda k, p0: k >= cand)
        live = act > 0.5
        exact = live & (c == topk)
        up = live & (c > topk)
        dn = live & (c < topk)
        below = jnp.where(exact, cand - 1, below)
        found = jnp.where(exact, 1.0, found)
        lo, c_lo = jnp.where(up, cand, lo), jnp.where(up, c, c_lo)
        hi, c_hi = jnp.where(dn, cand, hi), jnp.where(dn, c, c_hi)
        act = jnp.where(live & jnp.logical_not(exact) & (lo + 1 < hi), 1.0, 0.0)
        return it + 1, jnp.sum(act), lo, hi, c_lo, c_hi, act, found, below

    st = lax.while_loop(lambda st: (st[0] < 100) & (st[1] > 0.5), search_round,
                        (jnp.int32(0), jnp.sum(act0), lo0, hi0, c_lo0, c_hi0, act0,
                         jnp.where(found0, 1.0, 0.0), below0))
    _, _, lo, _, _, c_hi, _, found, below = st

    tied = crowded & (found < 0.5)
    thr = jnp.where(tied, lo, below)
    need = topk - c_hi
    any_tie = jnp.max(jnp.where(tied, 1.0, 0.0)) > 0.5

    def tie_cut():
        def pos_body(i, cut):
            cand = cut + lax.shift_left(jnp.int32(1), (seq.bit_length() - 1) - i)
            cnt = count(lambda k, p0: (k == thr) & (p0 + srow < cand))
            return jnp.where(cnt < need, cand, cut)
        return lax.fori_loop(0, seq.bit_length(), pos_body, jnp.zeros((1, TQ), jnp.int32))

    cut = lax.cond(any_tie, tie_cut, lambda: jnp.full((1, TQ), -1, jnp.int32))
    cut = jnp.where(tied, cut, -1)

    q = jnp.concatenate([aq_ref[0, :, LANES * h:LANES * (h + 1)] for h in range(A_HEADS)], axis=0)

    def att_body(j, carry):
        ks = pl.multiple_of(j * KC, KC)
        kv = akv_ref[0, pl.ds(ks, KC), :]
        k = key_scr[j]
        taken = (k > thr) | ((k == thr) & (ks + krow <= cut))
        bias = jnp.where(taken, 0.0, NEG_BIG).T
        s3 = _dot_nt(q, kv).reshape(A_HEADS, TQ, KC) + bias[None]
        return _softmax_step(carry, s3, _pv_shared(av1_ref[0, pl.ds(ks, KC), :]))

    carry = lax.fori_loop(0, nchunk, att_body, _online_init(A_HEADS, TQ))
    _store_heads(o_ref, _online_finish(carry), A_HEADS)


def _dsa(p, batch, seq):
    topk = min(DSA_TOPK_MAX, seq // 4)
    r3 = lambda a: a.reshape(batch, seq, a.shape[-1])
    qblk = lambda c: pl.BlockSpec((1, DSA_TQ, c), lambda b, i: (b, i, 0))
    full = lambda c: pl.BlockSpec((1, seq, c), lambda b, i: (b, 0, 0))
    return pl.pallas_call(
        functools.partial(_dsa_body, topk=topk, seq=seq),
        grid=(batch, seq // DSA_TQ),
        in_specs=[qblk(4 * 256), full(256), qblk(LANES), qblk(4 * LANES), full(LANES), full(LANES)],
        out_specs=qblk(A_HEADS * HEAD_DIM),
        out_shape=jax.ShapeDtypeStruct((batch, seq, A_HEADS * HEAD_DIM), BF16),
        scratch_shapes=[pltpu.VMEM((seq // KC, KC, DSA_TQ), jnp.int32)],
        compiler_params=_cparams(("parallel", "arbitrary")),
        name="dsa",
    )(r3(p["iq"]), r3(p["ik"]), r3(p["misc"]), r3(p["aq"]), r3(p["akv"]), r3(p["av1"]))


def _pv_per_head(v1s):
    return lambda p: jnp.concatenate(
        [jnp.dot(p[h], v1s[h], preferred_element_type=F32) for h in range(len(v1s))], axis=0)


def _moba_body(bq_ref, bqp_ref, bkv_ref, bv1_ref, km_ref, hot_ref, o_ref, *, nbp):
    TQ = MOBA_TQ
    c = pl.program_id(1)
    q0 = c * TQ
    own = q0 // MOBA_BLOCK
    row = q0 + lax.broadcasted_iota(jnp.int32, (TQ, 1), 0)
    col = _lane_iota((1, MOBA_BLOCK))
    ks_own = pl.multiple_of(own * MOBA_BLOCK, MOBA_BLOCK)
    heads = range(B_HEADS)
    kv_at = lambda ks, h: bkv_ref[0, pl.ds(ks, MOBA_BLOCK), LANES * h:LANES * (h + 1)]
    v1_at = lambda ks, h: bv1_ref[0, pl.ds(ks, MOBA_BLOCK), LANES * h:LANES * (h + 1)]

    blk = lax.broadcasted_iota(jnp.int32, (nbp, 1), 0)
    gate = jnp.concatenate(
        [_dot_nt(km_ref[0, :nbp, HEAD_DIM * h:HEAD_DIM * (h + 1)], bq_ref[0, :, HEAD_DIM * h:HEAD_DIM * (h + 1)],
                 precision=HIGHEST) for h in heads], axis=1)
    picked = (blk < own) & (_topn_mask(jnp.where(blk < own, gate, -jnp.inf), MOBA_TOPK, axis=0) > 0.0)
    bias_t = jnp.where(picked, 0.0, NEG_BIG)
    fill = jnp.zeros((LANES - nbp, TQ), F32)

    q_aug, s_own = [], []
    for h in heads:
        bias = jnp.concatenate([bias_t[:, TQ * h:TQ * (h + 1)], fill], axis=0).T
        q = bqp_ref[0, :, LANES * h:LANES * (h + 1)]
        q_aug.append(jnp.concatenate([q, bias.astype(BF16)], axis=1))
        s_own.append(jnp.where(ks_own + col <= row, _dot_nt(q, kv_at(ks_own, h)), NEG_BIG))

    stack = lambda xs: jnp.concatenate(xs, axis=0).reshape(B_HEADS, TQ, MOBA_BLOCK)
    carry = _softmax_step(_online_init(B_HEADS, TQ), stack(s_own),
                          _pv_per_head([v1_at(ks_own, h) for h in heads]))

    def scores(j):
        ks = pl.multiple_of(j * MOBA_BLOCK, MOBA_BLOCK)
        hot = hot_ref[pl.ds(ks, MOBA_BLOCK), :]
        return stack([_dot_nt(q_aug[h], jnp.concatenate([kv_at(ks, h), hot], axis=1)) for h in heads])

    def body(j, carry):
        state, s_cur = carry
        s_next = scores(jnp.minimum(j + 1, own - 1))
        ks = pl.multiple_of(j * MOBA_BLOCK, MOBA_BLOCK)
        return _softmax_step(state, s_cur, _pv_per_head([v1_at(ks, h) for h in heads])), s_next

    carry, _ = lax.fori_loop(0, own, body, (carry, scores(0)))
    _store_heads(o_ref, _online_finish(carry), B_HEADS)


def _moba(p, hot, batch, seq):
    nb = seq // MOBA_BLOCK
    r3 = lambda a: a.reshape(batch, seq, a.shape[-1])
    km = p["km"][:, :PROJ_TM // MOBA_BLOCK].reshape(batch, nb, 256)
    km = jnp.pad(km, ((0, 0), (0, LANES - nb), (0, 0)))
    qblk = lambda c: pl.BlockSpec((1, MOBA_TQ, c), lambda b, i: (b, i, 0))
    return pl.pallas_call(
        functools.partial(_moba_body, nbp=min(LANES, -(-nb // 8) * 8)),
        grid=(batch, seq // MOBA_TQ),
        in_specs=[qblk(256), qblk(4 * LANES),
                  pl.BlockSpec((1, seq, 4 * LANES), lambda b, i: (b, 0, 0)),
                  pl.BlockSpec((1, seq, 4 * LANES), lambda b, i: (b, 0, 0)),
                  pl.BlockSpec((1, LANES, 256), lambda b, i: (b, 0, 0)),
                  pl.BlockSpec((seq, LANES), lambda b, i: (0, 0))],
        out_specs=qblk(B_HEADS * HEAD_DIM),
        out_shape=jax.ShapeDtypeStruct((batch, seq, B_HEADS * HEAD_DIM), BF16),
        compiler_params=_cparams(("parallel", "arbitrary")),
        name="moba",
    )(r3(p["bq"]), r3(p["bqp"]), r3(p["bkv"]), r3(p["bv1"]), km, hot)


def _nsa_body(cqn_ref, cqr_ref, misc_ref, kc_ref, vc_ref, cslc_ref, cwin_ref, cvs1_ref, cvw1_ref, ov_ref, hot_ref,
              o_ref,
              *, ncp, n_sel):
    TQ = NSA_TQ
    c = pl.program_id(1)
    q0 = c * TQ
    last = (q0 + TQ - 1) // KC
    ks_last = pl.multiple_of(last * KC, KC)
    row = q0 + lax.broadcasted_iota(jnp.int32, (TQ, 1), 0)
    col = _lane_iota((1, KC))
    gates = jax.nn.sigmoid(misc_ref[0])
    cmp_vis = _lane_iota((1, ncp)) * CMP_STRIDE + (CMP_BLOCK - 1) <= row
    wstart = pl.multiple_of(jnp.maximum(q0 - WINDOW, 0), TQ)
    wlen = WINDOW + TQ
    wdiff = row - (wstart + _lane_iota((1, wlen)))
    wmask = (wdiff >= 0) & (wdiff < WINDOW)

    o_cmp, imp_t = [], []
    for g in range(C_GROUPS):
        qn = jnp.concatenate([cqn_ref[0, :, 256 * (C_REP * g + r):256 * (C_REP * g + r + 1)]
                              for r in range(C_REP)], axis=0)
        s3 = _dot_nt(qn, kc_ref[0, g]).reshape(C_REP, TQ, ncp)
        m = jnp.max(jnp.where(cmp_vis[None], s3, -jnp.inf), axis=-1, keepdims=True)
        m = jnp.where(m > -jnp.inf, m, 0.0)
        pc = jnp.where(cmp_vis[None], jnp.exp2(s3 - m), 0.0)
        den = jnp.sum(pc, axis=-1, keepdims=True)
        pc = pc / jnp.where(den > 0, den, 1.0)
        o_cmp.append(jnp.dot(pc.reshape(C_REP * TQ, ncp).astype(BF16), vc_ref[0, g],
                             preferred_element_type=F32))
        psum = pc[0]
        for r in range(1, C_REP):
            psum = psum + pc[r]
        hi = psum.astype(BF16)
        rest = psum - hi.astype(F32)
        mid = rest.astype(BF16)
        low = (rest - mid.astype(F32)).astype(BF16)
        imp = jnp.dot(jnp.concatenate([hi, mid, low], axis=1), ov_ref[...], preferred_element_type=F32)
        imp_t.append(imp.T)

    blk = lax.broadcasted_iota(jnp.int32, (LANES, 1), 0)
    jq = (q0 + _lane_iota((1, C_GROUPS * TQ)) % TQ) // SLC_BLOCK
    adm = blk <= jq
    forced = adm & ((blk == 0) | (blk == jq) | (blk == jq - 1))
    score = jnp.where(forced, FORCE_SCORE, jnp.where(adm, jnp.concatenate(imp_t, axis=1), -jnp.inf))
    picked = adm & (_topn_mask(score, n_sel, axis=0) > 0.0)
    bias_t = jnp.where(picked, 0.0, NEG_BIG)

    for g in range(C_GROUPS):
        o_c = o_cmp[g]
        bias = bias_t[:, TQ * g:TQ * (g + 1)].T.astype(BF16)

        qr = jnp.concatenate([cqr_ref[0, :, LANES * (C_REP * g + r):LANES * (C_REP * g + r + 1)]
                              for r in range(C_REP)], axis=0)
        qa = jnp.concatenate([qr, jnp.concatenate([bias] * C_REP, axis=0)], axis=1)
        kv_at = lambda j: cslc_ref[0, pl.ds(pl.multiple_of(j * KC, KC), KC), LANES * g:LANES * (g + 1)]
        v1_at = lambda j: cvs1_ref[0, pl.ds(pl.multiple_of(j * KC, KC), KC), LANES * g:LANES * (g + 1)]
        slc_scores = lambda j: _dot_nt(qa, jnp.concatenate(
            [kv_at(j), hot_ref[pl.ds(pl.multiple_of(j * KC, KC), KC), :]], axis=1))

        def slc_body(j, carry):
            return _softmax_step(carry, slc_scores(j).reshape(C_REP, TQ, KC), _pv_shared(v1_at(j)))

        carry = lax.fori_loop(0, last, slc_body, _online_init(C_REP, TQ))
        s3 = jnp.where((ks_last + col <= row)[None], slc_scores(last).reshape(C_REP, TQ, KC), NEG_BIG)
        o_s = _online_finish(_softmax_step(carry, s3, _pv_shared(v1_at(last))))

        kvw = cwin_ref[0, pl.ds(wstart, wlen), LANES * g:LANES * (g + 1)]
        s3 = jnp.where(wmask[None], _dot_nt(qr, kvw).reshape(C_REP, TQ, wlen), NEG_BIG)
        v1w = cvw1_ref[0, pl.ds(wstart, wlen), LANES * g:LANES * (g + 1)]
        o_w = _online_finish(_softmax_step(_online_init(C_REP, TQ), s3, _pv_shared(v1w)))

        outs = []
        for r in range(C_REP):
            hh = C_REP * g + r
            gcol = lambda j: gates[:, MISC_CG + 3 * hh + j:MISC_CG + 3 * hh + j + 1]
            rows = slice(TQ * r, TQ * (r + 1))
            outs.append(gcol(0) * o_c[rows] + gcol(1) * o_s[rows] + gcol(2) * o_w[rows])
        _store_heads(o_ref, jnp.concatenate(outs, axis=0), C_REP, col0=C_REP * HEAD_DIM * g)


def _nsa(p, kcmp, vcmp, overlap, hot, batch, seq):
    ncp = seq // CMP_STRIDE
    n_sel = min(SLC_TOPN, seq // SLC_BLOCK)
    r3 = lambda a: a.reshape(batch, seq, a.shape[-1])
    qblk = lambda c: pl.BlockSpec((1, NSA_TQ, c), lambda b, i: (b, i, 0))
    full = lambda c: pl.BlockSpec((1, seq, c), lambda b, i: (b, 0, 0))
    return pl.pallas_call(
        functools.partial(_nsa_body, ncp=ncp, n_sel=n_sel),
        grid=(batch, seq // NSA_TQ),
        in_specs=[qblk(8 * 256), qblk(8 * LANES), qblk(LANES),
                  pl.BlockSpec((1, C_GROUPS, ncp, 256), lambda b, i: (b, 0, 0, 0)),
                  pl.BlockSpec((1, C_GROUPS, ncp, LANES), lambda b, i: (b, 1, 0, 0)),
                  full(2 * LANES), full(2 * LANES), full(2 * LANES), full(2 * LANES),
                  pl.BlockSpec((3 * ncp, LANES), lambda b, i: (0, 0)),
                  pl.BlockSpec((seq, LANES), lambda b, i: (0, 0))],
        out_specs=qblk(C_HEADS * HEAD_DIM),
        out_shape=jax.ShapeDtypeStruct((batch, seq, C_HEADS * HEAD_DIM), BF16),
        compiler_params=_cparams(("parallel", "arbitrary")),
        name="nsa",
    )(r3(p["cqn"]), r3(p["cqr"]), r3(p["misc"]), kcmp, vcmp, r3(p["cslc"]), r3(p["cwin"]),
      r3(p["cvs1"]), r3(p["cvw1"]), overlap, hot)


def _merge_body(x_ref, g_ref, oa_ref, ob_ref, oc_ref, wm_ref, wa_ref, wb_ref, wc_ref, wo_ref, o_ref):
    x = x_ref[...]
    d = x.shape[1]
    h = _rms(x, g_ref[...]).astype(BF16)
    merged = None
    for i, (o_r, w_r) in enumerate(((oa_ref, wa_ref), (ob_ref, wb_ref), (oc_ref, wc_ref))):
        gate = jax.nn.sigmoid(jnp.dot(h, wm_ref[:, d * i:d * (i + 1)], preferred_element_type=F32))
        y = gate * jnp.dot(o_r[...], w_r[...], preferred_element_type=F32)
        merged = y if merged is None else merged + y
    o_ref[...] = x + jnp.dot(merged.astype(BF16), wo_ref[...], preferred_element_type=F32)


def _merge(x2, g, oa, ob, oc, wm, wa, wb, wc, wo, layer):
    m, d = x2.shape
    tm = MERGE_TM
    row = lambda c: pl.BlockSpec((tm, c), lambda i: (i, 0))
    const = lambda a: pl.BlockSpec(a.shape, lambda i: (0, 0))
    lay = lambda a: pl.BlockSpec((None,) + a.shape[1:], lambda i: (layer, 0, 0))
    return pl.pallas_call(
        _merge_body,
        grid=(m // tm,),
        in_specs=[row(d), const(g), row(oa.shape[1]), row(ob.shape[1]), row(oc.shape[1]),
                  lay(wm), lay(wa), lay(wb), lay(wc), lay(wo)],
        out_specs=row(d),
        out_shape=jax.ShapeDtypeStruct((m, d), F32),
        compiler_params=_cparams(("parallel",)),
        name="merge",
    )(x2, g, oa, ob, oc, wm, wa, wb, wc, wo)


def _rope_tables(seq):
    half = HEAD_DIM // 2
    inv = ROPE_THETA ** (-jnp.arange(half, dtype=F32) / half)
    ang = jnp.arange(seq, dtype=F32)[:, None] * inv[None, :]
    lane = np.arange(LANES)
    ang = ang[:, lane % half]
    second = jnp.asarray((lane % HEAD_DIM) >= half)[None, :]
    sin = jnp.sin(ang)
    return jnp.cos(ang), jnp.where(second, sin, 0.0), jnp.where(second, 0.0, -sin)


def _overlap(seq):
    ncp = seq // CMP_STRIDE
    nc = (seq - CMP_BLOCK) // CMP_STRIDE + 1
    ns = seq // SLC_BLOCK
    cs = np.arange(ncp) * CMP_STRIDE
    ss = np.arange(ns) * SLC_BLOCK
    ov = (cs[:, None] < ss[None, :] + SLC_BLOCK) & (ss[None, :] <= cs[:, None] + CMP_BLOCK - 1)
    ov &= (np.arange(ncp) < nc)[:, None]
    out = np.zeros((ncp, LANES), np.float32)
    out[:, :ns] = ov
    return jnp.asarray(np.tile(out, (3, 1)), BF16)


def _block_onehot(seq, block):
    return jnp.asarray(np.arange(seq)[:, None] // block == np.arange(LANES)[None, :], BF16)


def kernel(x, ffn1_norm, ffn1_w_gate, ffn1_w_up, ffn1_w_down, mix_norm, w_in, cmp_pos_k, cmp_w1_k, cmp_w2_k, cmp_pos_v, cmp_w1_v, cmp_w2_v, w_branch_a, w_branch_b, w_branch_c, w_out, ffn2_norm, ffn2_w_gate, ffn2_w_up, ffn2_w_down, final_norm):
    batch, seq, d = x.shape
    depth = w_in.shape[0]
    assert seq % max(KC, PROJ_TM) == 0 and WINDOW + NSA_TQ <= seq <= SLC_BLOCK * LANES and w_in.shape[2] == _N_IN
    nc = (seq - CMP_BLOCK) // CMP_STRIDE + 1
    ncp = seq // CMP_STRIDE

    bf = lambda a: a.astype(BF16)
    zero_cols = lambda n: jnp.zeros(w_in.shape[:2] + (n,), w_in.dtype)
    w_perm = bf(jnp.concatenate([w_in[:, :, a:a + n] if a >= 0 else zero_cols(n) for a, n in _perm_runs()], axis=2))
    w_mg = bf(w_in[:, :, _O_MG:])
    ffn1 = (bf(ffn1_w_gate), bf(ffn1_w_up), bf(ffn1_w_down))
    ffn2 = (bf(ffn2_w_gate), bf(ffn2_w_up), bf(ffn2_w_down))
    w_a, w_b, w_c, w_o = bf(w_branch_a), bf(w_branch_b), bf(w_branch_c), bf(w_out)
    half_rows = CMP_BLOCK * HEAD_DIM // 2
    w1 = jnp.stack([cmp_w1_k, cmp_w1_v], axis=1)
    w1 = jnp.concatenate([w1[:, :, :half_rows], w1[:, :, half_rows:]], axis=-1)
    w2 = jnp.pad(jnp.stack([cmp_w2_k, cmp_w2_v], axis=1), ((0, 0), (0, 0), (0, 0), (0, LANES - HEAD_DIM)))
    pos = jnp.stack([cmp_pos_k, cmp_pos_v], axis=1).reshape(depth, 2, 2, half_rows)
    pos = jnp.pad(pos, ((0, 0), (0, 0), (0, 6), (0, 0)))
    cos, sa, sb = _rope_tables(seq)
    overlap = _overlap(seq)
    hot_b = _block_onehot(seq, MOBA_BLOCK)
    hot_s = _block_onehot(seq, SLC_BLOCK)
    row = lambda a: a.reshape(1, d)

    x2 = x.reshape(batch * seq, d)
    for l in range(depth):
        x2 = _ffn(x2, row(ffn1_norm[l]), *ffn1, row(final_norm), l, False)
        p = _inproj(x2, row(mix_norm[l]), w_perm, l, cos, sa, sb, seq)
        xc = p["ccmp"].reshape(batch, seq, 4, HEAD_DIM).transpose(0, 2, 1, 3).reshape(batch, 4, ncp, 1024)
        kc, vc = _compress(xc, w1, w2, pos, l, nc)
        o_a = _dsa(p, batch, seq)
        o_b = _moba(p, hot_b, batch, seq)
        o_c = _nsa(p, kc, vc, overlap, hot_s, batch, seq)
        flat = lambda a: a.reshape(batch * seq, a.shape[-1])
        x2 = _merge(x2, row(mix_norm[l]), flat(o_a), flat(o_b), flat(o_c), w_mg, w_a, w_b, w_c, w_o, l)
        x2 = _ffn(x2, row(ffn2_norm[l]), *ffn2, row(final_norm), l, l == depth - 1)
    return x2.reshape(batch, seq, d)
```

```python
import functools
import math

import numpy as np
import jax
import jax.numpy as jnp
from jax import lax
from jax.experimental import pallas as pl
from jax.experimental.pallas import tpu as pltpu

HEAD_DIM = 64
ROPE_THETA = 10000.0
NORM_EPS = 1e-6
A_HEADS = 4
IDX_HEADS = 4
IDX_DIM = 64
DSA_TOPK_MAX = 256
B_HEADS = 4
MOBA_BLOCK = 256
MOBA_TOPK = 3
C_HEADS = 8
C_GROUPS = 2
C_REP = C_HEADS // C_GROUPS
CMP_BLOCK = 32
CMP_STRIDE = 16
CMP_HIDDEN = 128
SLC_BLOCK = 64
SLC_TOPN = 16
WINDOW = 512
FORCE_SCORE = 1e30

LANES = 128
VMEM_LIMIT = 56 * 1024 * 1024

DSA_TQ = 256
MOBA_TQ = 256
NSA_TQ = 256
KC = 512
NSA_KC = 1024
COUNT_KEYS = 128
FFN_TM = 512
PROJ_TM = 512
MERGE_TM = 512

NEG_BIG = -1e30
QSCALE = HEAD_DIM ** -0.5 * math.log2(math.e)
F32 = jnp.float32
BF16 = jnp.bfloat16
HIGHEST = lax.Precision.HIGHEST

_O_AQ = 0
_O_AK = 256
_O_AV = 320
_O_IQ = 384
_O_IK = 640
_O_IW = 704
_O_BQ = 708
_O_BK = 964
_O_BV = 1220
_O_CQ = 1476
_O_CKV = 1988
_O_CG = 2756
_O_MG = 2780
_N_IN = 5852

P_AQ = 0
P_AK = 256
P_IK = 320
P_IQ = 384
P_BQ = 640
P_BK = 896
P_CKS = 1152
P_CKW = 1280
P_CQ = 1408
P_ROPE_END = 1920
P_BV = 1920
P_KCMP = 2176
P_VS = 2432
P_VW = 2560
P_AV = 2688
P_MISC = 2752
P_COLS = 2816
MISC_CG = 0
MISC_IW = 24


def _in_perm():
    perm = -np.ones((P_COLS,), np.int64)

    def put(dst, src, n):
        perm[dst:dst + n] = np.arange(src, src + n)

    put(P_AQ, _O_AQ, 256)
    put(P_AK, _O_AK, 64)
    put(P_IK, _O_IK, 64)
    put(P_IQ, _O_IQ, 256)
    put(P_BQ, _O_BQ, 256)
    put(P_BK, _O_BK, 256)
    ckv = lambda s, g: _O_CKV + (s * C_GROUPS + g) * HEAD_DIM
    for g in range(C_GROUPS):
        put(P_CKS + 64 * g, ckv(2, g), 64)
        put(P_CKW + 64 * g, ckv(4, g), 64)
        put(P_KCMP + 64 * g, ckv(0, g), 64)
        put(P_KCMP + 128 + 64 * g, ckv(1, g), 64)
        put(P_VS + 64 * g, ckv(3, g), 64)
        put(P_VW + 64 * g, ckv(5, g), 64)
    put(P_CQ, _O_CQ, 512)
    put(P_BV, _O_BV, 256)
    put(P_AV, _O_AV, 64)
    put(P_MISC + MISC_CG, _O_CG, 24)
    put(P_MISC + MISC_IW, _O_IW, 4)
    return perm


def _perm_runs():
    perm = _in_perm()
    runs, start = [], 0
    for i in range(1, len(perm) + 1):
        if i == len(perm) or (perm[i] != perm[i - 1] + 1 if perm[i - 1] >= 0 else perm[i] >= 0):
            runs.append((int(perm[start]), i - start))
            start = i
    return runs


def _cparams(sem):
    return pltpu.CompilerParams(dimension_semantics=sem, vmem_limit_bytes=VMEM_LIMIT)


def _rms(x, g):
    y = x * lax.rsqrt(jnp.mean(x * x, axis=-1, keepdims=True) + NORM_EPS)
    return y * g


def _dot_nt(a, b, precision=None):
    return lax.dot_general(a, b, (((1,), (1,)), ((), ())), precision=precision,
                           preferred_element_type=F32)


def _ffn_body(x_ref, g_ref, wg_ref, wu_ref, wd_ref, fg_ref, o_ref, *, final_norm):
    x = x_ref[...]
    h = _rms(x, g_ref[...]).astype(BF16)
    a = jnp.dot(h, wg_ref[...], preferred_element_type=F32)
    u = jnp.dot(h, wu_ref[...], preferred_element_type=F32)
    act = (a * jax.nn.sigmoid(a) * u).astype(BF16)
    y = x + 0.5 * jnp.dot(act, wd_ref[...], preferred_element_type=F32)
    if final_norm:
        y = _rms(y, fg_ref[...])
    o_ref[...] = y


def _ffn(x2, g, wg, wu, wd, fg, layer, final_norm):
    m, d = x2.shape
    dff = wg.shape[2]
    tm = FFN_TM
    return pl.pallas_call(
        functools.partial(_ffn_body, final_norm=final_norm),
        grid=(m // tm,),
        in_specs=[
            pl.BlockSpec((tm, d), lambda i: (i, 0)),
            pl.BlockSpec((1, d), lambda i: (0, 0)),
            pl.BlockSpec((None, d, dff), lambda i: (layer, 0, 0)),
            pl.BlockSpec((None, d, dff), lambda i: (layer, 0, 0)),
            pl.BlockSpec((None, dff, d), lambda i: (layer, 0, 0)),
            pl.BlockSpec((1, d), lambda i: (0, 0)),
        ],
        out_specs=pl.BlockSpec((tm, d), lambda i: (i, 0)),
        out_shape=jax.ShapeDtypeStruct((m, d), F32),
        compiler_params=_cparams(("parallel",)),
        name="ffn",
    )(x2, g, wg, wu, wd, fg)


def _lane_iota(shape):
    return lax.broadcasted_iota(jnp.int32, shape, len(shape) - 1)


def _split_hi_lo(x):
    hi = x.astype(BF16).astype(F32)
    return hi, x - hi


def _inproj_body(x_ref, g_ref, w_ref, cos_ref, sa_ref, sb_ref,
                 aq_ref, akv_ref, iq_ref, ik_ref, misc_ref, bq_ref, bqp_ref, bkv_ref, km_ref,
                 cqn_ref, cqr_ref, cslc_ref, cwin_ref, ccmp_ref, av1_ref, bv1_ref, cvs1_ref, cvw1_ref):
    tm = x_ref.shape[0]
    h = _rms(x_ref[...], g_ref[...]).astype(BF16)
    cos, sa, sb = cos_ref[...], sa_ref[...], sb_ref[...]
    lane = _lane_iota((tm, LANES))
    low = lane < HEAD_DIM

    raw = []
    rot = []
    for j in range(P_COLS // 256):
        z = jnp.dot(h, w_ref[:, 256 * j:256 * (j + 1)], preferred_element_type=F32)
        for half in range(2):
            p = z[:, LANES * half:LANES * (half + 1)]
            raw.append(p)
            if LANES * len(raw) <= P_ROPE_END:
                rot.append(p * cos + pltpu.roll(p, 32, axis=1) * sa + pltpu.roll(p, 96, axis=1) * sb)

    def pick(pieces, col):
        p = pieces[col // LANES]
        return pltpu.roll(p, HEAD_DIM, axis=1) if col % LANES else p

    def join(lo, hi=None):
        if hi is None:
            return jnp.where(low, lo, 0.0)
        return jnp.where(low, lo, pltpu.roll(hi, HEAD_DIM, axis=1))

    def one_v(v):
        return jnp.where(low, 1.0, pltpu.roll(v, HEAD_DIM, axis=1)).astype(BF16)

    scale = QSCALE

    for hh in range(A_HEADS):
        aq_ref[:, LANES * hh:LANES * (hh + 1)] = join(pick(rot, P_AQ + 64 * hh) * scale).astype(BF16)
        qhi, qlo = _split_hi_lo(pick(rot, P_IQ + 64 * hh))
        iq_ref[:, 256 * hh:256 * hh + LANES] = join(qhi, qlo).astype(BF16)
        iq_ref[:, 256 * hh + LANES:256 * (hh + 1)] = join(qhi).astype(BF16)
    akv_ref[...] = join(pick(rot, P_AK), pick(raw, P_AV)).astype(BF16)
    av1_ref[...] = one_v(pick(raw, P_AV))
    khi, klo = _split_hi_lo(pick(rot, P_IK))
    ik_ref[:, :LANES] = join(khi, khi).astype(BF16)
    ik_ref[:, LANES:] = join(klo).astype(BF16)
    misc_ref[...] = pick(raw, P_MISC)

    bq_ref[:, :LANES] = rot[P_BQ // LANES]
    bq_ref[:, LANES:] = rot[P_BQ // LANES + 1]
    for hh in range(B_HEADS):
        bqp_ref[:, LANES * hh:LANES * (hh + 1)] = join(pick(rot, P_BQ + 64 * hh) * scale).astype(BF16)
        bkv_ref[:, LANES * hh:LANES * (hh + 1)] = join(pick(rot, P_BK + 64 * hh),
                                                      pick(raw, P_BV + 64 * hh)).astype(BF16)
        bv1_ref[:, LANES * hh:LANES * (hh + 1)] = one_v(pick(raw, P_BV + 64 * hh))
    nblk = tm // MOBA_BLOCK
    km_ref[...] = jnp.zeros_like(km_ref)
    for half in range(2):
        kp = rot[P_BK // LANES + half]
        for b in range(nblk):
            km_ref[0, b:b + 1, LANES * half:LANES * (half + 1)] = jnp.mean(
                kp[MOBA_BLOCK * b:MOBA_BLOCK * (b + 1)], axis=0, keepdims=True)

    for hh in range(C_HEADS):
        nhi, nlo = _split_hi_lo(pick(raw, P_CQ + 64 * hh) * scale)
        cqn_ref[:, 256 * hh:256 * hh + LANES] = join(nhi, nlo).astype(BF16)
        cqn_ref[:, 256 * hh + LANES:256 * (hh + 1)] = join(nhi).astype(BF16)
        cqr_ref[:, LANES * hh:LANES * (hh + 1)] = join(pick(rot, P_CQ + 64 * hh) * scale).astype(BF16)
    for g in range(C_GROUPS):
        cslc_ref[:, LANES * g:LANES * (g + 1)] = join(pick(rot, P_CKS + 64 * g),
                                                     pick(raw, P_VS + 64 * g)).astype(BF16)
        cwin_ref[:, LANES * g:LANES * (g + 1)] = join(pick(rot, P_CKW + 64 * g),
                                                     pick(raw, P_VW + 64 * g)).astype(BF16)
        cvs1_ref[:, LANES * g:LANES * (g + 1)] = one_v(pick(raw, P_VS + 64 * g))
        cvw1_ref[:, LANES * g:LANES * (g + 1)] = one_v(pick(raw, P_VW + 64 * g))
    ccmp_ref[:, :LANES] = raw[P_KCMP // LANES]
    ccmp_ref[:, LANES:] = raw[P_KCMP // LANES + 1]


def _inproj(x2, g, w, layer, cos, sa, sb, seq):
    m, d = x2.shape
    tm = PROJ_TM
    nt = seq // tm
    row = lambda c: pl.BlockSpec((tm, c), lambda i: (i, 0))
    tab = pl.BlockSpec((tm, LANES), lambda i: (i % nt, 0))
    outs = [
        ("aq", 4 * LANES, BF16), ("akv", LANES, BF16), ("iq", 4 * 256, BF16), ("ik", 256, BF16),
        ("misc", LANES, F32), ("bq", 256, F32), ("bqp", 4 * LANES, BF16), ("bkv", 4 * LANES, BF16),
        ("km", None, F32),
        ("cqn", 8 * 256, BF16), ("cqr", 8 * LANES, BF16), ("cslc", 2 * LANES, BF16),
        ("cwin", 2 * LANES, BF16), ("ccmp", 256, F32),
        ("av1", LANES, BF16), ("bv1", 4 * LANES, BF16), ("cvs1", 2 * LANES, BF16), ("cvw1", 2 * LANES, BF16),
    ]
    out_specs, out_shape = [], []
    for name, c, dt in outs:
        if name == "km":
            out_specs.append(pl.BlockSpec((1, 8, 256), lambda i: (i, 0, 0)))
            out_shape.append(jax.ShapeDtypeStruct((m // tm, 8, 256), dt))
        else:
            out_specs.append(row(c))
            out_shape.append(jax.ShapeDtypeStruct((m, c), dt))
    res = pl.pallas_call(
        _inproj_body,
        grid=(m // tm,),
        in_specs=[row(d), pl.BlockSpec((1, d), lambda i: (0, 0)),
                  pl.BlockSpec((None, d, P_COLS), lambda i: (layer, 0, 0)), tab, tab, tab],
        out_specs=out_specs,
        out_shape=out_shape,
        compiler_params=_cparams(("parallel",)),
        name="inproj",
    )(x2, g, w, cos, sa, sb)
    return dict(zip([o[0] for o in outs], res))


def _compress_body(x_ref, w1_ref, w2_ref, pos_ref, ok_ref, ov_ref, *, n_valid):
    x = x_ref[0, 0]
    w1 = w1_ref[0]
    pre = jnp.dot(x, w1, precision=HIGHEST, preferred_element_type=F32)
    pp = jnp.dot(pos_ref[0], w1, precision=HIGHEST, preferred_element_type=F32)
    posb = pp[0:1, :CMP_HIDDEN] + pp[1:2, CMP_HIDDEN:]
    ncp = x.shape[0]
    upper = pre[:, :CMP_HIDDEN]
    lower_next = pltpu.roll(pre[:, CMP_HIDDEN:], ncp - 1, axis=0)
    hid = jax.nn.gelu(upper + lower_next + posb)
    out = jnp.dot(hid, w2_ref[0], precision=HIGHEST, preferred_element_type=F32)
    rows = lax.broadcasted_iota(jnp.int32, out.shape, 0)
    out = jnp.where(rows < n_valid, out, 0.0)
    low = _lane_iota(out.shape) < HEAD_DIM
    hi, lo = _split_hi_lo(out)
    ok_ref[0, 0, :, :LANES] = jnp.where(low, hi, pltpu.roll(hi, HEAD_DIM, axis=1)).astype(BF16)
    ok_ref[0, 0, :, LANES:] = lo.astype(BF16)
    ov_ref[0, 0] = pltpu.roll(out, HEAD_DIM, axis=1).astype(BF16)


def _compress(xc, w1, w2, pos, layer, n_valid):
    b, four, ncp, _ = xc.shape
    return pl.pallas_call(
        functools.partial(_compress_body, n_valid=n_valid),
        grid=(b, four),
        in_specs=[
            pl.BlockSpec((1, 1, ncp, 1024), lambda i, j: (i, j, 0, 0)),
            pl.BlockSpec((None, 1, 1024, 256), lambda i, j: (layer, j // 2, 0, 0)),
            pl.BlockSpec((None, 1, CMP_HIDDEN, LANES), lambda i, j: (layer, j // 2, 0, 0)),
            pl.BlockSpec((None, 1, 8, 1024), lambda i, j: (layer, j // 2, 0, 0)),
        ],
        out_specs=[pl.BlockSpec((1, 1, ncp, 256), lambda i, j: (i, j, 0, 0)),
                   pl.BlockSpec((1, 1, ncp, LANES), lambda i, j: (i, j, 0, 0))],
        out_shape=[jax.ShapeDtypeStruct((b, four, ncp, 256), BF16),
                   jax.ShapeDtypeStruct((b, four, ncp, LANES), BF16)],
        compiler_params=_cparams(("parallel", "parallel")),
        name="compress",
    )(xc, w1, w2, pos)


def _softmax_step(carry, s3, pv):
    m_old, acc = carry
    heads, tq = s3.shape[0], s3.shape[1]
    m_new = jnp.maximum(m_old, jnp.max(s3, axis=-1, keepdims=True))
    p = jnp.exp2((s3 - m_new).astype(BF16))
    alpha = jnp.exp2(m_old - m_new)
    return m_new, alpha.reshape(heads * tq, 1) * acc + pv(p)


def _online_init(heads, tq):
    return jnp.full((heads, tq, 1), NEG_BIG, F32), jnp.zeros((heads * tq, LANES), F32)


def _online_finish(carry):
    _, acc = carry
    return acc / acc[:, 0:1]


def _pv_shared(v1):
    return lambda p: jnp.dot(p.reshape(p.shape[0] * p.shape[1], p.shape[2]), v1, preferred_element_type=F32)


def _store_heads(o_ref, o, heads, col0=0):
    tq = o.shape[0] // heads
    low = _lane_iota((tq, LANES)) < HEAD_DIM
    for pair in range(heads // 2):
        even = o[tq * (2 * pair):tq * (2 * pair + 1)]
        odd = o[tq * (2 * pair + 1):tq * (2 * pair + 2)]
        piece = jnp.where(low, pltpu.roll(even, HEAD_DIM, axis=1), odd)
        o_ref[0, :, col0 + LANES * pair:col0 + LANES * (pair + 1)] = piece.astype(o_ref.dtype)


def _topn_mask(score, n_pick, axis=-1):
    axis = axis % score.ndim
    idx = lax.broadcasted_iota(jnp.int32, score.shape, axis)
    width = score.shape[axis]
    sel = jnp.zeros(score.shape, F32)
    for _ in range(n_pick):
        m = jnp.max(score, axis=axis, keepdims=True)
        first = jnp.min(jnp.where(score == m, idx, width), axis=axis, keepdims=True)
        hit = idx == first
        sel = jnp.where(hit, 1.0, sel)
        score = jnp.where(hit, -jnp.inf, score)
    return sel


_NEG_INF_KEY = -2139095041


def _key_of(x):
    bits = lax.bitcast_convert_type(x, jnp.int32)
    return jnp.where(bits < 0, bits ^ jnp.int32(0x7FFFFFFF), bits)


def _float_of(key):
    return lax.bitcast_convert_type(jnp.where(key < 0, key ^ jnp.int32(0x7FFFFFFF), key), F32)


def _dsa_body(iq_ref, ik_ref, misc_ref, aq_ref, akv_ref, av1_ref, o_ref, key_scr, *, topk, seq):
    TQ = DSA_TQ
    c = pl.program_id(1)
    q0 = c * TQ
    nchunk = (q0 + TQ + KC - 1) // KC
    qpos = q0 + _lane_iota((1, TQ))
    krow = lax.broadcasted_iota(jnp.int32, (KC, 1), 0)
    srow = lax.broadcasted_iota(jnp.int32, (COUNT_KEYS, 1), 0)
    idx_scale = (IDX_HEADS * IDX_DIM) ** -0.5

    iq = jnp.concatenate([iq_ref[0, :, 256 * h:256 * (h + 1)] for h in range(IDX_HEADS)], axis=0)
    misc_t = misc_ref[0].T
    iw = [misc_t[MISC_IW + h:MISC_IW + h + 1] * idx_scale for h in range(IDX_HEADS)]

    def score_body(j, carry):
        mx, mn = carry
        ks = pl.multiple_of(j * KC, KC)
        lg = jnp.maximum(_dot_nt(ik_ref[0, pl.ds(ks, KC), :], iq), 0.0)
        sc = iw[0] * lg[:, 0:TQ]
        for h in range(1, IDX_HEADS):
            sc = sc + iw[h] * lg[:, TQ * h:TQ * (h + 1)]
        sc = jnp.where(sc == 0.0, 0.0, sc)
        part = sc.reshape(KC // 8, 8, TQ)
        mx, mn = jnp.maximum(mx, jnp.max(part, axis=0)), jnp.minimum(mn, jnp.min(part, axis=0))
        key_scr[j] = _key_of(jnp.where(ks + krow <= qpos, sc, -jnp.inf))
        return mx, mn

    mx, mn = lax.fori_loop(0, nchunk, score_body,
                           (jnp.full((8, TQ), -jnp.inf, F32), jnp.full((8, TQ), jnp.inf, F32)))

    def count(*hits):
        def body(j, accs):
            accs = list(accs)
            for g in range(KC // COUNT_KEYS):
                k = key_scr[j, COUNT_KEYS * g:COUNT_KEYS * (g + 1), :]
                for i, hit in enumerate(hits):
                    one = jnp.where(hit(k, j * KC + COUNT_KEYS * g), 1.0, 0.0)
                    accs[i] = accs[i] + jnp.sum(one.reshape(COUNT_KEYS // 8, 8, TQ), axis=0)
            return tuple(accs)
        accs = lax.fori_loop(0, nchunk, body, tuple(jnp.zeros((8, TQ), F32) for _ in hits))
        out = tuple(jnp.sum(a, axis=0, keepdims=True) for a in accs)
        return out if len(out) > 1 else out[0]

    n_fin = (qpos + 1).astype(F32)
    crowded = n_fin > topk
    c_pos, c_nn = count(lambda k, p0: k > 0, lambda k, p0: k >= 0)
    positive = c_pos >= topk
    zero_tie = crowded & jnp.logical_not(positive) & (c_nn >= topk)
    lo0 = jnp.where(positive, 1, _key_of(jnp.min(mn, axis=0, keepdims=True)))
    hi0 = jnp.where(positive, _key_of(jnp.max(mx, axis=0, keepdims=True)) + 1, 0)
    c_lo0 = jnp.where(positive, c_pos, n_fin)
    c_hi0 = jnp.where(positive, 0.0, c_nn)
    lo0 = jnp.where(zero_tie, 0, lo0)
    c_hi0 = jnp.where(zero_tie, c_pos, c_hi0)
    found0 = positive & (c_pos == topk)
    act0 = jnp.where(crowded & jnp.logical_not(zero_tie | found0) & (lo0 + 1 < hi0), 1.0, 0.0)
    below0 = jnp.where(found0, 0, _NEG_INF_KEY)
    log_target = math.log(topk - 0.5)

    def search_round(st):
        it, _, lo, hi, c_lo, c_hi, act, found, below = st
        lo_f, hi_f = _float_of(lo), _float_of(hi)
        la, lb = jnp.log(c_lo), jnp.log(jnp.maximum(c_hi, 0.5))
        frac = jnp.clip((la - log_target) / (la - lb), 0.02, 0.98)
        phase = it % 3
        guess = jnp.where(phase == 0, lo_f + (hi_f - lo_f) * frac, (lo_f + hi_f) * 0.5)
        mid = (lo >> 1) + (hi >> 1) + (lo & hi & 1)
        cand = jnp.clip(jnp.where(phase == 2, mid, _key_of(guess)), lo + 1, hi - 1)
        c = count(lambda k, p0: k >= cand)
        live = act > 0.5
        exact = live & (c == topk)
        up = live & (c > topk)
        dn = live & (c < topk)
        below = jnp.where(exact, cand - 1, below)
        found = jnp.where(exact, 1.0, found)
        lo, c_lo = jnp.where(up, cand, lo), jnp.where(up, c, c_lo)
        hi, c_hi = jnp.where(dn, cand, hi), jnp.where(dn, c, c_hi)
        act = jnp.where(live & jnp.logical_not(exact) & (lo + 1 < hi), 1.0, 0.0)
        return it + 1, jnp.sum(act), lo, hi, c_lo, c_hi, act, found, below

    st = lax.while_loop(lambda st: (st[0] < 100) & (st[1] > 0.5), lambda st: search_round(search_round(st)),
                        (jnp.int32(0), jnp.sum(act0), lo0, hi0, c_lo0, c_hi0, act0,
                         jnp.where(found0, 1.0, 0.0), below0))
    _, _, lo, _, _, c_hi, _, found, below = st

    tied = crowded & (found < 0.5)
    thr = jnp.where(tied, lo, below)
    need = topk - c_hi
    any_tie = jnp.max(jnp.where(tied, 1.0, 0.0)) > 0.5

    def tie_cut():
        def pos_body(i, cut):
            cand = cut + lax.shift_left(jnp.int32(1), (seq.bit_length() - 1) - i)
            cnt = count(lambda k, p0: (k == thr) & (p0 + srow < cand))
            return jnp.where(cnt < need, cand, cut)
        return lax.fori_loop(0, seq.bit_length(), pos_body, jnp.zeros((1, TQ), jnp.int32))

    cut = lax.cond(any_tie, tie_cut, lambda: jnp.full((1, TQ), -1, jnp.int32))
    cut = jnp.where(tied, cut, -1)

    q = jnp.concatenate([aq_ref[0, :, LANES * h:LANES * (h + 1)] for h in range(A_HEADS)], axis=0)

    def att_body(j, carry):
        ks = pl.multiple_of(j * KC, KC)
        kv = akv_ref[0, pl.ds(ks, KC), :]
        k = key_scr[j]
        taken = (k > thr) | ((k == thr) & (ks + krow <= cut))
        bias = jnp.where(taken, 0.0, NEG_BIG).T
        s3 = _dot_nt(q, kv).reshape(A_HEADS, TQ, KC) + bias[None]
        return _softmax_step(carry, s3, _pv_shared(av1_ref[0, pl.ds(ks, KC), :]))

    carry = lax.fori_loop(0, nchunk, att_body, _online_init(A_HEADS, TQ))
    _store_heads(o_ref, _online_finish(carry), A_HEADS)


def _dsa(p, batch, seq):
    topk = min(DSA_TOPK_MAX, seq // 4)
    r3 = lambda a: a.reshape(batch, seq, a.shape[-1])
    qblk = lambda c: pl.BlockSpec((1, DSA_TQ, c), lambda b, i: (b, i, 0))
    full = lambda c: pl.BlockSpec((1, seq, c), lambda b, i: (b, 0, 0))
    return pl.pallas_call(
        functools.partial(_dsa_body, topk=topk, seq=seq),
        grid=(batch, seq // DSA_TQ),
        in_specs=[qblk(4 * 256), full(256), qblk(LANES), qblk(4 * LANES), full(LANES), full(LANES)],
        out_specs=qblk(A_HEADS * HEAD_DIM),
        out_shape=jax.ShapeDtypeStruct((batch, seq, A_HEADS * HEAD_DIM), BF16),
        scratch_shapes=[pltpu.VMEM((seq // KC, KC, DSA_TQ), jnp.int32)],
        compiler_params=_cparams(("parallel", "arbitrary")),
        name="dsa",
    )(r3(p["iq"]), r3(p["ik"]), r3(p["misc"]), r3(p["aq"]), r3(p["akv"]), r3(p["av1"]))


def _pv_per_head(v1s):
    return lambda p: jnp.concatenate(
        [jnp.dot(p[h], v1s[h], preferred_element_type=F32) for h in range(len(v1s))], axis=0)


def _moba_body(bq_ref, bqp_ref, bkv_ref, bv1_ref, km_ref, hot_ref, o_ref, *, nbp):
    TQ = MOBA_TQ
    c = pl.program_id(1)
    q0 = c * TQ
    own = q0 // MOBA_BLOCK
    row = q0 + lax.broadcasted_iota(jnp.int32, (TQ, 1), 0)
    col = _lane_iota((1, MOBA_BLOCK))
    ks_own = pl.multiple_of(own * MOBA_BLOCK, MOBA_BLOCK)
    heads = range(B_HEADS)
    kv_at = lambda ks, h: bkv_ref[0, pl.ds(ks, MOBA_BLOCK), LANES * h:LANES * (h + 1)]
    v1_at = lambda ks, h: bv1_ref[0, pl.ds(ks, MOBA_BLOCK), LANES * h:LANES * (h + 1)]

    blk = lax.broadcasted_iota(jnp.int32, (nbp, 1), 0)
    gate = jnp.concatenate(
        [_dot_nt(km_ref[0, :nbp, HEAD_DIM * h:HEAD_DIM * (h + 1)], bq_ref[0, :, HEAD_DIM * h:HEAD_DIM * (h + 1)],
                 precision=HIGHEST) for h in heads], axis=1)
    picked = (blk < own) & (_topn_mask(jnp.where(blk < own, gate, -jnp.inf), MOBA_TOPK, axis=0) > 0.0)
    bias_t = jnp.where(picked, 0.0, NEG_BIG)
    fill = jnp.zeros((LANES - nbp, TQ), F32)

    q_aug, s_own = [], []
    for h in heads:
        bias = jnp.concatenate([bias_t[:, TQ * h:TQ * (h + 1)], fill], axis=0).T
        q = bqp_ref[0, :, LANES * h:LANES * (h + 1)]
        q_aug.append(jnp.concatenate([q, bias.astype(BF16)], axis=1))
        s_own.append(jnp.where(ks_own + col <= row, _dot_nt(q, kv_at(ks_own, h)), NEG_BIG))

    stack = lambda xs: jnp.concatenate(xs, axis=0).reshape(B_HEADS, TQ, MOBA_BLOCK)
    carry = _softmax_step(_online_init(B_HEADS, TQ), stack(s_own),
                          _pv_per_head([v1_at(ks_own, h) for h in heads]))

    def scores(j):
        ks = pl.multiple_of(j * MOBA_BLOCK, MOBA_BLOCK)
        hot = hot_ref[pl.ds(ks, MOBA_BLOCK), :]
        return stack([_dot_nt(q_aug[h], jnp.concatenate([kv_at(ks, h), hot], axis=1)) for h in heads])

    def body(j, carry):
        state, s_cur = carry
        s_next = scores(jnp.minimum(j + 1, own - 1))
        ks = pl.multiple_of(j * MOBA_BLOCK, MOBA_BLOCK)
        return _softmax_step(state, s_cur, _pv_per_head([v1_at(ks, h) for h in heads])), s_next

    carry, _ = lax.fori_loop(0, own, body, (carry, scores(0)))
    _store_heads(o_ref, _online_finish(carry), B_HEADS)


def _moba(p, hot, batch, seq):
    nb = seq // MOBA_BLOCK
    r3 = lambda a: a.reshape(batch, seq, a.shape[-1])
    km = p["km"][:, :PROJ_TM // MOBA_BLOCK].reshape(batch, nb, 256)
    km = jnp.pad(km, ((0, 0), (0, LANES - nb), (0, 0)))
    qblk = lambda c: pl.BlockSpec((1, MOBA_TQ, c), lambda b, i: (b, i, 0))
    return pl.pallas_call(
        functools.partial(_moba_body, nbp=min(LANES, -(-nb // 8) * 8)),
        grid=(batch, seq // MOBA_TQ),
        in_specs=[qblk(256), qblk(4 * LANES),
                  pl.BlockSpec((1, seq, 4 * LANES), lambda b, i: (b, 0, 0)),
                  pl.BlockSpec((1, seq, 4 * LANES), lambda b, i: (b, 0, 0)),
                  pl.BlockSpec((1, LANES, 256), lambda b, i: (b, 0, 0)),
                  pl.BlockSpec((seq, LANES), lambda b, i: (0, 0))],
        out_specs=qblk(B_HEADS * HEAD_DIM),
        out_shape=jax.ShapeDtypeStruct((batch, seq, B_HEADS * HEAD_DIM), BF16),
        compiler_params=_cparams(("parallel", "arbitrary")),
        name="moba",
    )(r3(p["bq"]), r3(p["bqp"]), r3(p["bkv"]), r3(p["bv1"]), km, hot)


def _nsa_body(cqn_ref, cqr_ref, misc_ref, kc_ref, vc_ref, cslc_ref, cwin_ref, cvs1_ref, cvw1_ref, ov_ref, hot_ref,
              o_ref,
              *, ncp, n_sel):
    TQ, KC = NSA_TQ, NSA_KC
    c = pl.program_id(1)
    q0 = c * TQ
    last = (q0 + TQ - 1) // KC
    ks_last = pl.multiple_of(last * KC, KC)
    row = q0 + lax.broadcasted_iota(jnp.int32, (TQ, 1), 0)
    col = _lane_iota((1, KC))
    gates = jax.nn.sigmoid(misc_ref[0])
    cmp_vis = _lane_iota((1, ncp)) * CMP_STRIDE + (CMP_BLOCK - 1) <= row
    wstart = pl.multiple_of(jnp.maximum(q0 - WINDOW, 0), TQ)
    wlen = WINDOW + TQ
    wdiff = row - (wstart + _lane_iota((1, wlen)))
    wmask = (wdiff >= 0) & (wdiff < WINDOW)

    o_cmp, imp_t = [], []
    for g in range(C_GROUPS):
        qn = jnp.concatenate([cqn_ref[0, :, 256 * (C_REP * g + r):256 * (C_REP * g + r + 1)]
                              for r in range(C_REP)], axis=0)
        s3 = _dot_nt(qn, kc_ref[0, g]).reshape(C_REP, TQ, ncp)
        m = jnp.max(jnp.where(cmp_vis[None], s3, -jnp.inf), axis=-1, keepdims=True)
        m = jnp.where(m > -jnp.inf, m, 0.0)
        pc = jnp.where(cmp_vis[None], jnp.exp2(s3 - m), 0.0)
        den = jnp.sum(pc, axis=-1, keepdims=True)
        pc = pc / jnp.where(den > 0, den, 1.0)
        o_cmp.append(jnp.dot(pc.reshape(C_REP * TQ, ncp).astype(BF16), vc_ref[0, g],
                             preferred_element_type=F32))
        psum = pc[0]
        for r in range(1, C_REP):
            psum = psum + pc[r]
        hi = psum.astype(BF16)
        rest = psum - hi.astype(F32)
        mid = rest.astype(BF16)
        low = (rest - mid.astype(F32)).astype(BF16)
        imp = jnp.dot(jnp.concatenate([hi, mid, low], axis=1), ov_ref[...], preferred_element_type=F32)
        imp_t.append(imp.T)

    blk = lax.broadcasted_iota(jnp.int32, (LANES, 1), 0)
    jq = (q0 + _lane_iota((1, C_GROUPS * TQ)) % TQ) // SLC_BLOCK
    adm = blk <= jq
    forced = adm & ((blk == 0) | (blk == jq) | (blk == jq - 1))
    score = jnp.where(forced, FORCE_SCORE, jnp.where(adm, jnp.concatenate(imp_t, axis=1), -jnp.inf))
    picked = adm & (_topn_mask(score, n_sel, axis=0) > 0.0)
    bias_t = jnp.where(picked, 0.0, NEG_BIG)

    for g in range(C_GROUPS):
        o_c = o_cmp[g]
        bias = bias_t[:, TQ * g:TQ * (g + 1)].T.astype(BF16)

        qr = jnp.concatenate([cqr_ref[0, :, LANES * (C_REP * g + r):LANES * (C_REP * g + r + 1)]
                              for r in range(C_REP)], axis=0)
        qa = jnp.concatenate([qr, jnp.concatenate([bias] * C_REP, axis=0)], axis=1)
        kv_at = lambda j: cslc_ref[0, pl.ds(pl.multiple_of(j * KC, KC), KC), LANES * g:LANES * (g + 1)]
        v1_at = lambda j: cvs1_ref[0, pl.ds(pl.multiple_of(j * KC, KC), KC), LANES * g:LANES * (g + 1)]
        slc_scores = lambda j: _dot_nt(qa, jnp.concatenate(
            [kv_at(j), hot_ref[pl.ds(pl.multiple_of(j * KC, KC), KC), :]], axis=1))

        def slc_body(j, carry):
            return _softmax_step(carry, slc_scores(j).reshape(C_REP, TQ, KC), _pv_shared(v1_at(j)))

        carry = lax.fori_loop(0, last, slc_body, _online_init(C_REP, TQ))
        s3 = jnp.where((ks_last + col <= row)[None], slc_scores(last).reshape(C_REP, TQ, KC), NEG_BIG)
        o_s = _online_finish(_softmax_step(carry, s3, _pv_shared(v1_at(last))))

        kvw = cwin_ref[0, pl.ds(wstart, wlen), LANES * g:LANES * (g + 1)]
        s3 = jnp.where(wmask[None], _dot_nt(qr, kvw).reshape(C_REP, TQ, wlen), NEG_BIG)
        v1w = cvw1_ref[0, pl.ds(wstart, wlen), LANES * g:LANES * (g + 1)]
        o_w = _online_finish(_softmax_step(_online_init(C_REP, TQ), s3, _pv_shared(v1w)))

        outs = []
        for r in range(C_REP):
            hh = C_REP * g + r
            gcol = lambda j: gates[:, MISC_CG + 3 * hh + j:MISC_CG + 3 * hh + j + 1]
            rows = slice(TQ * r, TQ * (r + 1))
            outs.append(gcol(0) * o_c[rows] + gcol(1) * o_s[rows] + gcol(2) * o_w[rows])
        _store_heads(o_ref, jnp.concatenate(outs, axis=0), C_REP, col0=C_REP * HEAD_DIM * g)


def _nsa(p, kcmp, vcmp, overlap, hot, batch, seq):
    ncp = seq // CMP_STRIDE
    n_sel = min(SLC_TOPN, seq // SLC_BLOCK)
    r3 = lambda a: a.reshape(batch, seq, a.shape[-1])
    qblk = lambda c: pl.BlockSpec((1, NSA_TQ, c), lambda b, i: (b, i, 0))
    full = lambda c: pl.BlockSpec((1, seq, c), lambda b, i: (b, 0, 0))
    return pl.pallas_call(
        functools.partial(_nsa_body, ncp=ncp, n_sel=n_sel),
        grid=(batch, seq // NSA_TQ),
        in_specs=[qblk(8 * 256), qblk(8 * LANES), qblk(LANES),
                  pl.BlockSpec((1, C_GROUPS, ncp, 256), lambda b, i: (b, 0, 0, 0)),
                  pl.BlockSpec((1, C_GROUPS, ncp, LANES), lambda b, i: (b, 1, 0, 0)),
                  full(2 * LANES), full(2 * LANES), full(2 * LANES), full(2 * LANES),
                  pl.BlockSpec((3 * ncp, LANES), lambda b, i: (0, 0)),
                  pl.BlockSpec((seq, LANES), lambda b, i: (0, 0))],
        out_specs=qblk(C_HEADS * HEAD_DIM),
        out_shape=jax.ShapeDtypeStruct((batch, seq, C_HEADS * HEAD_DIM), BF16),
        compiler_params=_cparams(("parallel", "arbitrary")),
        name="nsa",
    )(r3(p["cqn"]), r3(p["cqr"]), r3(p["misc"]), kcmp, vcmp, r3(p["cslc"]), r3(p["cwin"]),
      r3(p["cvs1"]), r3(p["cvw1"]), overlap, hot)


def _merge_body(x_ref, g_ref, oa_ref, ob_ref, oc_ref, wm_ref, wa_ref, wb_ref, wc_ref, wo_ref, o_ref):
    x = x_ref[...]
    d = x.shape[1]
    h = _rms(x, g_ref[...]).astype(BF16)
    merged = None
    for i, (o_r, w_r) in enumerate(((oa_ref, wa_ref), (ob_ref, wb_ref), (oc_ref, wc_ref))):
        gate = jax.nn.sigmoid(jnp.dot(h, wm_ref[:, d * i:d * (i + 1)], preferred_element_type=F32))
        y = gate * jnp.dot(o_r[...], w_r[...], preferred_element_type=F32)
        merged = y if merged is None else merged + y
    o_ref[...] = x + jnp.dot(merged.astype(BF16), wo_ref[...], preferred_element_type=F32)


def _merge(x2, g, oa, ob, oc, wm, wa, wb, wc, wo, layer):
    m, d = x2.shape
    tm = MERGE_TM
    row = lambda c: pl.BlockSpec((tm, c), lambda i: (i, 0))
    const = lambda a: pl.BlockSpec(a.shape, lambda i: (0, 0))
    lay = lambda a: pl.BlockSpec((None,) + a.shape[1:], lambda i: (layer, 0, 0))
    return pl.pallas_call(
        _merge_body,
        grid=(m // tm,),
        in_specs=[row(d), const(g), row(oa.shape[1]), row(ob.shape[1]), row(oc.shape[1]),
                  lay(wm), lay(wa), lay(wb), lay(wc), lay(wo)],
        out_specs=row(d),
        out_shape=jax.ShapeDtypeStruct((m, d), F32),
        compiler_params=_cparams(("parallel",)),
        name="merge",
    )(x2, g, oa, ob, oc, wm, wa, wb, wc, wo)


def _rope_tables(seq):
    half = HEAD_DIM // 2
    inv = ROPE_THETA ** (-jnp.arange(half, dtype=F32) / half)
    ang = jnp.arange(seq, dtype=F32)[:, None] * inv[None, :]
    lane = np.arange(LANES)
    ang = ang[:, lane % half]
    second = jnp.asarray((lane % HEAD_DIM) >= half)[None, :]
    sin = jnp.sin(ang)
    return jnp.cos(ang), jnp.where(second, sin, 0.0), jnp.where(second, 0.0, -sin)


def _overlap(seq):
    ncp = seq // CMP_STRIDE
    nc = (seq - CMP_BLOCK) // CMP_STRIDE + 1
    ns = seq // SLC_BLOCK
    cs = np.arange(ncp) * CMP_STRIDE
    ss = np.arange(ns) * SLC_BLOCK
    ov = (cs[:, None] < ss[None, :] + SLC_BLOCK) & (ss[None, :] <= cs[:, None] + CMP_BLOCK - 1)
    ov &= (np.arange(ncp) < nc)[:, None]
    out = np.zeros((ncp, LANES), np.float32)
    out[:, :ns] = ov
    return jnp.asarray(np.tile(out, (3, 1)), BF16)


def _block_onehot(seq, block):
    return jnp.asarray(np.arange(seq)[:, None] // block == np.arange(LANES)[None, :], BF16)


def kernel(x, ffn1_norm, ffn1_w_gate, ffn1_w_up, ffn1_w_down, mix_norm, w_in, cmp_pos_k, cmp_w1_k, cmp_w2_k, cmp_pos_v, cmp_w1_v, cmp_w2_v, w_branch_a, w_branch_b, w_branch_c, w_out, ffn2_norm, ffn2_w_gate, ffn2_w_up, ffn2_w_down, final_norm):
    batch, seq, d = x.shape
    depth = w_in.shape[0]
    assert seq % max(KC, NSA_KC, PROJ_TM) == 0 and WINDOW + NSA_TQ <= seq <= SLC_BLOCK * LANES and w_in.shape[2] == _N_IN
    nc = (seq - CMP_BLOCK) // CMP_STRIDE + 1
    ncp = seq // CMP_STRIDE

    bf = lambda a: a.astype(BF16)
    zero_cols = lambda n: jnp.zeros(w_in.shape[:2] + (n,), w_in.dtype)
    w_perm = bf(jnp.concatenate([w_in[:, :, a:a + n] if a >= 0 else zero_cols(n) for a, n in _perm_runs()], axis=2))
    w_mg = bf(w_in[:, :, _O_MG:])
    ffn1 = (bf(ffn1_w_gate), bf(ffn1_w_up), bf(ffn1_w_down))
    ffn2 = (bf(ffn2_w_gate), bf(ffn2_w_up), bf(ffn2_w_down))
    w_a, w_b, w_c, w_o = bf(w_branch_a), bf(w_branch_b), bf(w_branch_c), bf(w_out)
    half_rows = CMP_BLOCK * HEAD_DIM // 2
    w1 = jnp.stack([cmp_w1_k, cmp_w1_v], axis=1)
    w1 = jnp.concatenate([w1[:, :, :half_rows], w1[:, :, half_rows:]], axis=-1)
    w2 = jnp.pad(jnp.stack([cmp_w2_k, cmp_w2_v], axis=1), ((0, 0), (0, 0), (0, 0), (0, LANES - HEAD_DIM)))
    pos = jnp.stack([cmp_pos_k, cmp_pos_v], axis=1).reshape(depth, 2, 2, half_rows)
    pos = jnp.pad(pos, ((0, 0), (0, 0), (0, 6), (0, 0)))
    cos, sa, sb = _rope_tables(seq)
    overlap = _overlap(seq)
    hot_b = _block_onehot(seq, MOBA_BLOCK)
    hot_s = _block_onehot(seq, SLC_BLOCK)
    row = lambda a: a.reshape(1, d)

    x2 = x.reshape(batch * seq, d)
    for l in range(depth):
        x2 = _ffn(x2, row(ffn1_norm[l]), *ffn1, row(final_norm), l, False)
        p = _inproj(x2, row(mix_norm[l]), w_perm, l, cos, sa, sb, seq)
        xc = p["ccmp"].reshape(batch, seq, 4, HEAD_DIM).transpose(0, 2, 1, 3).reshape(batch, 4, ncp, 1024)
        kc, vc = _compress(xc, w1, w2, pos, l, nc)
        o_a = _dsa(p, batch, seq)
        o_b = _moba(p, hot_b, batch, seq)
        o_c = _nsa(p, kc, vc, overlap, hot_s, batch, seq)
        flat = lambda a: a.reshape(batch * seq, a.shape[-1])
        x2 = _merge(x2, row(mix_norm[l]), flat(o_a), flat(o_b), flat(o_c), w_mg, w_a, w_b, w_c, w_o, l)
        x2 = _ffn(x2, row(ffn2_norm[l]), *ffn2, row(final_norm), l, l == depth - 1)
    return x2.reshape(batch, seq, d)
```

```python
import functools
import math

import numpy as np
import jax
import jax.numpy as jnp
from jax import lax
from jax.experimental import pallas as pl
from jax.experimental.pallas import tpu as pltpu

HEAD_DIM = 64
ROPE_THETA = 10000.0
NORM_EPS = 1e-6
A_HEADS = 4
IDX_HEADS = 4
IDX_DIM = 64
DSA_TOPK_MAX = 256
B_HEADS = 4
MOBA_BLOCK = 256
MOBA_TOPK = 3
C_HEADS = 8
C_GROUPS = 2
C_REP = C_HEADS // C_GROUPS
CMP_BLOCK = 32
CMP_STRIDE = 16
CMP_HIDDEN = 128
SLC_BLOCK = 64
SLC_TOPN = 16
WINDOW = 512
FORCE_SCORE = 1e30

LANES = 128
VMEM_LIMIT = 56 * 1024 * 1024

DSA_TQ = 512
MOBA_TQ = 256
NSA_TQ = 256
KC = 512
NSA_KC = 1024
COUNT_KEYS = 128
FFN_TM = 512
PROJ_TM = 512
MERGE_TM = 512

NEG_BIG = -1e30
QSCALE = HEAD_DIM ** -0.5 * math.log2(math.e)
F32 = jnp.float32
BF16 = jnp.bfloat16
HIGHEST = lax.Precision.HIGHEST

_O_AQ = 0
_O_AK = 256
_O_AV = 320
_O_IQ = 384
_O_IK = 640
_O_IW = 704
_O_BQ = 708
_O_BK = 964
_O_BV = 1220
_O_CQ = 1476
_O_CKV = 1988
_O_CG = 2756
_O_MG = 2780
_N_IN = 5852

P_AQ = 0
P_AK = 256
P_IK = 320
P_IQ = 384
P_BQ = 640
P_BK = 896
P_CKS = 1152
P_CKW = 1280
P_CQ = 1408
P_ROPE_END = 1920
P_BV = 1920
P_KCMP = 2176
P_VS = 2432
P_VW = 2560
P_AV = 2688
P_MISC = 2752
P_COLS = 2816
MISC_CG = 0
MISC_IW = 24


def _in_perm():
    perm = -np.ones((P_COLS,), np.int64)

    def put(dst, src, n):
        perm[dst:dst + n] = np.arange(src, src + n)

    put(P_AQ, _O_AQ, 256)
    put(P_AK, _O_AK, 64)
    put(P_IK, _O_IK, 64)
    put(P_IQ, _O_IQ, 256)
    put(P_BQ, _O_BQ, 256)
    put(P_BK, _O_BK, 256)
    ckv = lambda s, g: _O_CKV + (s * C_GROUPS + g) * HEAD_DIM
    for g in range(C_GROUPS):
        put(P_CKS + 64 * g, ckv(2, g), 64)
        put(P_CKW + 64 * g, ckv(4, g), 64)
        put(P_KCMP + 64 * g, ckv(0, g), 64)
        put(P_KCMP + 128 + 64 * g, ckv(1, g), 64)
        put(P_VS + 64 * g, ckv(3, g), 64)
        put(P_VW + 64 * g, ckv(5, g), 64)
    put(P_CQ, _O_CQ, 512)
    put(P_BV, _O_BV, 256)
    put(P_AV, _O_AV, 64)
    put(P_MISC + MISC_CG, _O_CG, 24)
    put(P_MISC + MISC_IW, _O_IW, 4)
    return perm


def _perm_runs():
    perm = _in_perm()
    runs, start = [], 0
    for i in range(1, len(perm) + 1):
        if i == len(perm) or (perm[i] != perm[i - 1] + 1 if perm[i - 1] >= 0 else perm[i] >= 0):
            runs.append((int(perm[start]), i - start))
            start = i
    return runs


def _cparams(sem):
    return pltpu.CompilerParams(dimension_semantics=sem, vmem_limit_bytes=VMEM_LIMIT)


def _rms(x, g):
    y = x * lax.rsqrt(jnp.mean(x * x, axis=-1, keepdims=True) + NORM_EPS)
    return y * g


def _dot_nt(a, b, precision=None):
    return lax.dot_general(a, b, (((1,), (1,)), ((), ())), precision=precision,
                           preferred_element_type=F32)


def _ffn_body(x_ref, g_ref, wg_ref, wu_ref, wd_ref, fg_ref, o_ref, *, final_norm):
    x = x_ref[...]
    h = _rms(x, g_ref[...]).astype(BF16)
    a = jnp.dot(h, wg_ref[...], preferred_element_type=F32)
    u = jnp.dot(h, wu_ref[...], preferred_element_type=F32)
    act = (a * jax.nn.sigmoid(a) * u).astype(BF16)
    y = x + 0.5 * jnp.dot(act, wd_ref[...], preferred_element_type=F32)
    if final_norm:
        y = _rms(y, fg_ref[...])
    o_ref[...] = y


def _ffn(x2, g, wg, wu, wd, fg, layer, final_norm):
    m, d = x2.shape
    dff = wg.shape[2]
    tm = FFN_TM
    return pl.pallas_call(
        functools.partial(_ffn_body, final_norm=final_norm),
        grid=(m // tm,),
        in_specs=[
            pl.BlockSpec((tm, d), lambda i: (i, 0)),
            pl.BlockSpec((1, d), lambda i: (0, 0)),
            pl.BlockSpec((None, d, dff), lambda i: (layer, 0, 0)),
            pl.BlockSpec((None, d, dff), lambda i: (layer, 0, 0)),
            pl.BlockSpec((None, dff, d), lambda i: (layer, 0, 0)),
            pl.BlockSpec((1, d), lambda i: (0, 0)),
        ],
        out_specs=pl.BlockSpec((tm, d), lambda i: (i, 0)),
        out_shape=jax.ShapeDtypeStruct((m, d), F32),
        compiler_params=_cparams(("parallel",)),
        name="ffn",
    )(x2, g, wg, wu, wd, fg)


def _lane_iota(shape):
    return lax.broadcasted_iota(jnp.int32, shape, len(shape) - 1)


def _split_hi_lo(x):
    hi = x.astype(BF16).astype(F32)
    return hi, x - hi


def _inproj_body(x_ref, g_ref, w_ref, cos_ref, sa_ref, sb_ref,
                 aq_ref, akv_ref, iq_ref, ik_ref, misc_ref, bq_ref, bqp_ref, bkv_ref, km_ref,
                 cqn_ref, cqr_ref, cslc_ref, cwin_ref, ccmp_ref, av1_ref, bv1_ref, cvs1_ref, cvw1_ref):
    tm = x_ref.shape[0]
    h = _rms(x_ref[...], g_ref[...]).astype(BF16)
    cos, sa, sb = cos_ref[...], sa_ref[...], sb_ref[...]
    lane = _lane_iota((tm, LANES))
    low = lane < HEAD_DIM

    raw = []
    rot = []
    for j in range(P_COLS // 256):
        z = jnp.dot(h, w_ref[:, 256 * j:256 * (j + 1)], preferred_element_type=F32)
        for half in range(2):
            p = z[:, LANES * half:LANES * (half + 1)]
            raw.append(p)
            if LANES * len(raw) <= P_ROPE_END:
                rot.append(p * cos + pltpu.roll(p, 32, axis=1) * sa + pltpu.roll(p, 96, axis=1) * sb)

    def pick(pieces, col):
        p = pieces[col // LANES]
        return pltpu.roll(p, HEAD_DIM, axis=1) if col % LANES else p

    def join(lo, hi=None):
        if hi is None:
            return jnp.where(low, lo, 0.0)
        return jnp.where(low, lo, pltpu.roll(hi, HEAD_DIM, axis=1))

    def one_v(v):
        return jnp.where(low, 1.0, pltpu.roll(v, HEAD_DIM, axis=1)).astype(BF16)

    scale = QSCALE

    for hh in range(A_HEADS):
        aq_ref[:, LANES * hh:LANES * (hh + 1)] = join(pick(rot, P_AQ + 64 * hh) * scale).astype(BF16)
        qhi, qlo = _split_hi_lo(pick(rot, P_IQ + 64 * hh))
        iq_ref[:, 256 * hh:256 * hh + LANES] = join(qhi, qlo).astype(BF16)
        iq_ref[:, 256 * hh + LANES:256 * (hh + 1)] = join(qhi).astype(BF16)
    akv_ref[...] = join(pick(rot, P_AK), pick(raw, P_AV)).astype(BF16)
    av1_ref[...] = one_v(pick(raw, P_AV))
    khi, klo = _split_hi_lo(pick(rot, P_IK))
    ik_ref[:, :LANES] = join(khi, khi).astype(BF16)
    ik_ref[:, LANES:] = join(klo).astype(BF16)
    misc_ref[...] = pick(raw, P_MISC)

    bq_ref[:, :LANES] = rot[P_BQ // LANES]
    bq_ref[:, LANES:] = rot[P_BQ // LANES + 1]
    for hh in range(B_HEADS):
        bqp_ref[:, LANES * hh:LANES * (hh + 1)] = join(pick(rot, P_BQ + 64 * hh) * scale).astype(BF16)
        bkv_ref[:, LANES * hh:LANES * (hh + 1)] = join(pick(rot, P_BK + 64 * hh),
                                                      pick(raw, P_BV + 64 * hh)).astype(BF16)
        bv1_ref[:, LANES * hh:LANES * (hh + 1)] = one_v(pick(raw, P_BV + 64 * hh))
    nblk = tm // MOBA_BLOCK
    km_ref[...] = jnp.zeros_like(km_ref)
    for half in range(2):
        kp = rot[P_BK // LANES + half]
        for b in range(nblk):
            km_ref[0, b:b + 1, LANES * half:LANES * (half + 1)] = jnp.mean(
                kp[MOBA_BLOCK * b:MOBA_BLOCK * (b + 1)], axis=0, keepdims=True)

    for hh in range(C_HEADS):
        nhi, nlo = _split_hi_lo(pick(raw, P_CQ + 64 * hh) * scale)
        cqn_ref[:, 256 * hh:256 * hh + LANES] = join(nhi, nlo).astype(BF16)
        cqn_ref[:, 256 * hh + LANES:256 * (hh + 1)] = join(nhi).astype(BF16)
        cqr_ref[:, LANES * hh:LANES * (hh + 1)] = join(pick(rot, P_CQ + 64 * hh) * scale).astype(BF16)
    for g in range(C_GROUPS):
        cslc_ref[:, LANES * g:LANES * (g + 1)] = join(pick(rot, P_CKS + 64 * g),
                                                     pick(raw, P_VS + 64 * g)).astype(BF16)
        cwin_ref[:, LANES * g:LANES * (g + 1)] = join(pick(rot, P_CKW + 64 * g),
                                                     pick(raw, P_VW + 64 * g)).astype(BF16)
        cvs1_ref[:, LANES * g:LANES * (g + 1)] = one_v(pick(raw, P_VS + 64 * g))
        cvw1_ref[:, LANES * g:LANES * (g + 1)] = one_v(pick(raw, P_VW + 64 * g))
    ccmp_ref[:, :LANES] = raw[P_KCMP // LANES]
    ccmp_ref[:, LANES:] = raw[P_KCMP // LANES + 1]


def _inproj(x2, g, w, layer, cos, sa, sb, seq):
    m, d = x2.shape
    tm = PROJ_TM
    nt = seq // tm
    row = lambda c: pl.BlockSpec((tm, c), lambda i: (i, 0))
    tab = pl.BlockSpec((tm, LANES), lambda i: (i % nt, 0))
    outs = [
        ("aq", 4 * LANES, BF16), ("akv", LANES, BF16), ("iq", 4 * 256, BF16), ("ik", 256, BF16),
        ("misc", LANES, F32), ("bq", 256, F32), ("bqp", 4 * LANES, BF16), ("bkv", 4 * LANES, BF16),
        ("km", None, F32),
        ("cqn", 8 * 256, BF16), ("cqr", 8 * LANES, BF16), ("cslc", 2 * LANES, BF16),
        ("cwin", 2 * LANES, BF16), ("ccmp", 256, F32),
        ("av1", LANES, BF16), ("bv1", 4 * LANES, BF16), ("cvs1", 2 * LANES, BF16), ("cvw1", 2 * LANES, BF16),
    ]
    out_specs, out_shape = [], []
    for name, c, dt in outs:
        if name == "km":
            out_specs.append(pl.BlockSpec((1, 8, 256), lambda i: (i, 0, 0)))
            out_shape.append(jax.ShapeDtypeStruct((m // tm, 8, 256), dt))
        else:
            out_specs.append(row(c))
            out_shape.append(jax.ShapeDtypeStruct((m, c), dt))
    res = pl.pallas_call(
        _inproj_body,
        grid=(m // tm,),
        in_specs=[row(d), pl.BlockSpec((1, d), lambda i: (0, 0)),
                  pl.BlockSpec((None, d, P_COLS), lambda i: (layer, 0, 0)), tab, tab, tab],
        out_specs=out_specs,
        out_shape=out_shape,
        compiler_params=_cparams(("parallel",)),
        name="inproj",
    )(x2, g, w, cos, sa, sb)
    return dict(zip([o[0] for o in outs], res))


def _compress_body(x_ref, w1_ref, w2_ref, pos_ref, ok_ref, ov_ref, *, n_valid):
    x = x_ref[0, 0]
    w1 = w1_ref[0]
    pre = jnp.dot(x, w1, precision=HIGHEST, preferred_element_type=F32)
    pp = jnp.dot(pos_ref[0], w1, precision=HIGHEST, preferred_element_type=F32)
    posb = pp[0:1, :CMP_HIDDEN] + pp[1:2, CMP_HIDDEN:]
    ncp = x.shape[0]
    upper = pre[:, :CMP_HIDDEN]
    lower_next = pltpu.roll(pre[:, CMP_HIDDEN:], ncp - 1, axis=0)
    hid = jax.nn.gelu(upper + lower_next + posb)
    out = jnp.dot(hid, w2_ref[0], precision=HIGHEST, preferred_element_type=F32)
    rows = lax.broadcasted_iota(jnp.int32, out.shape, 0)
    out = jnp.where(rows < n_valid, out, 0.0)
    low = _lane_iota(out.shape) < HEAD_DIM
    hi, lo = _split_hi_lo(out)
    ok_ref[0, 0, :, :LANES] = jnp.where(low, hi, pltpu.roll(hi, HEAD_DIM, axis=1)).astype(BF16)
    ok_ref[0, 0, :, LANES:] = lo.astype(BF16)
    ov_ref[0, 0] = pltpu.roll(out, HEAD_DIM, axis=1).astype(BF16)


def _compress(xc, w1, w2, pos, layer, n_valid):
    b, four, ncp, _ = xc.shape
    return pl.pallas_call(
        functools.partial(_compress_body, n_valid=n_valid),
        grid=(b, four),
        in_specs=[
            pl.BlockSpec((1, 1, ncp, 1024), lambda i, j: (i, j, 0, 0)),
            pl.BlockSpec((None, 1, 1024, 256), lambda i, j: (layer, j // 2, 0, 0)),
            pl.BlockSpec((None, 1, CMP_HIDDEN, LANES), lambda i, j: (layer, j // 2, 0, 0)),
            pl.BlockSpec((None, 1, 8, 1024), lambda i, j: (layer, j // 2, 0, 0)),
        ],
        out_specs=[pl.BlockSpec((1, 1, ncp, 256), lambda i, j: (i, j, 0, 0)),
                   pl.BlockSpec((1, 1, ncp, LANES), lambda i, j: (i, j, 0, 0))],
        out_shape=[jax.ShapeDtypeStruct((b, four, ncp, 256), BF16),
                   jax.ShapeDtypeStruct((b, four, ncp, LANES), BF16)],
        compiler_params=_cparams(("parallel", "parallel")),
        name="compress",
    )(xc, w1, w2, pos)


def _softmax_step(carry, s3, pv):
    m_old, acc = carry
    heads, tq = s3.shape[0], s3.shape[1]
    m_new = jnp.maximum(m_old, jnp.max(s3, axis=-1, keepdims=True))
    p = jnp.exp2((s3 - m_new).astype(BF16))
    alpha = jnp.exp2(m_old - m_new)
    return m_new, alpha.reshape(heads * tq, 1) * acc + pv(p)


def _online_init(heads, tq):
    return jnp.full((heads, tq, 1), NEG_BIG, F32), jnp.zeros((heads * tq, LANES), F32)


def _online_finish(carry):
    _, acc = carry
    return acc / acc[:, 0:1]


def _pv_shared(v1):
    return lambda p: jnp.dot(p.reshape(p.shape[0] * p.shape[1], p.shape[2]), v1, preferred_element_type=F32)


def _store_heads(o_ref, o, heads, col0=0):
    tq = o.shape[0] // heads
    low = _lane_iota((tq, LANES)) < HEAD_DIM
    for pair in range(heads // 2):
        even = o[tq * (2 * pair):tq * (2 * pair + 1)]
        odd = o[tq * (2 * pair + 1):tq * (2 * pair + 2)]
        piece = jnp.where(low, pltpu.roll(even, HEAD_DIM, axis=1), odd)
        o_ref[0, :, col0 + LANES * pair:col0 + LANES * (pair + 1)] = piece.astype(o_ref.dtype)


def _topn_mask(score, n_pick, axis=-1):
    axis = axis % score.ndim
    idx = lax.broadcasted_iota(jnp.int32, score.shape, axis)
    width = score.shape[axis]
    sel = jnp.zeros(score.shape, F32)
    for _ in range(n_pick):
        m = jnp.max(score, axis=axis, keepdims=True)
        first = jnp.min(jnp.where(score == m, idx, width), axis=axis, keepdims=True)
        hit = idx == first
        sel = jnp.where(hit, 1.0, sel)
        score = jnp.where(hit, -jnp.inf, score)
    return sel


_NEG_INF_KEY = -2139095041


def _key_of(x):
    bits = lax.bitcast_convert_type(x, jnp.int32)
    return jnp.where(bits < 0, bits ^ jnp.int32(0x7FFFFFFF), bits)


def _float_of(key):
    return lax.bitcast_convert_type(jnp.where(key < 0, key ^ jnp.int32(0x7FFFFFFF), key), F32)


def _dsa_body(iq_ref, ik_ref, misc_ref, aq_ref, akv_ref, av1_ref, o_ref, key_scr, *, topk, seq):
    TQ = DSA_TQ
    c = pl.program_id(1)
    q0 = c * TQ
    nchunk = (q0 + TQ + KC - 1) // KC
    qpos = q0 + _lane_iota((1, TQ))
    krow = lax.broadcasted_iota(jnp.int32, (KC, 1), 0)
    srow = lax.broadcasted_iota(jnp.int32, (COUNT_KEYS, 1), 0)
    idx_scale = (IDX_HEADS * IDX_DIM) ** -0.5

    iq = jnp.concatenate([iq_ref[0, :, 256 * h:256 * (h + 1)] for h in range(IDX_HEADS)], axis=0)
    misc_t = misc_ref[0].T
    iw = [misc_t[MISC_IW + h:MISC_IW + h + 1] * idx_scale for h in range(IDX_HEADS)]

    def score_body(j, carry):
        mx, mn = carry
        ks = pl.multiple_of(j * KC, KC)
        lg = jnp.maximum(_dot_nt(ik_ref[0, pl.ds(ks, KC), :], iq), 0.0)
        sc = iw[0] * lg[:, 0:TQ]
        for h in range(1, IDX_HEADS):
            sc = sc + iw[h] * lg[:, TQ * h:TQ * (h + 1)]
        sc = jnp.where(sc == 0.0, 0.0, sc)
        part = sc.reshape(KC // 8, 8, TQ)
        mx, mn = jnp.maximum(mx, jnp.max(part, axis=0)), jnp.minimum(mn, jnp.min(part, axis=0))
        key_scr[j] = _key_of(jnp.where(ks + krow <= qpos, sc, -jnp.inf))
        return mx, mn

    mx, mn = lax.fori_loop(0, nchunk, score_body,
                           (jnp.full((8, TQ), -jnp.inf, F32), jnp.full((8, TQ), jnp.inf, F32)))

    def count(*hits):
        def body(j, accs):
            accs = list(accs)
            for g in range(KC // COUNT_KEYS):
                k = key_scr[j, COUNT_KEYS * g:COUNT_KEYS * (g + 1), :]
                for i, hit in enumerate(hits):
                    one = jnp.where(hit(k, j * KC + COUNT_KEYS * g), 1.0, 0.0)
                    accs[i] = accs[i] + jnp.sum(one.reshape(COUNT_KEYS // 8, 8, TQ), axis=0)
            return tuple(accs)
        accs = lax.fori_loop(0, nchunk, body, tuple(jnp.zeros((8, TQ), F32) for _ in hits))
        out = tuple(jnp.sum(a, axis=0, keepdims=True) for a in accs)
        return out if len(out) > 1 else out[0]

    n_fin = (qpos + 1).astype(F32)
    crowded = n_fin > topk
    c_pos, c_nn = count(lambda k, p0: k > 0, lambda k, p0: k >= 0)
    positive = c_pos >= topk
    zero_tie = crowded & jnp.logical_not(positive) & (c_nn >= topk)
    lo0 = jnp.where(positive, 1, _key_of(jnp.min(mn, axis=0, keepdims=True)))
    hi0 = jnp.where(positive, _key_of(jnp.max(mx, axis=0, keepdims=True)) + 1, 0)
    c_lo0 = jnp.where(positive, c_pos, n_fin)
    c_hi0 = jnp.where(positive, 0.0, c_nn)
    lo0 = jnp.where(zero_tie, 0, lo0)
    c_hi0 = jnp.where(zero_tie, c_pos, c_hi0)
    found0 = positive & (c_pos == topk)
    act0 = jnp.where(crowded & jnp.logical_not(zero_tie | found0) & (lo0 + 1 < hi0), 1.0, 0.0)
    below0 = jnp.where(found0, 0, _NEG_INF_KEY)
    log_target = math.log(topk - 0.5)

    def search_round(st):
        it, _, lo, hi, c_lo, c_hi, act, found, below = st
        lo_f, hi_f = _float_of(lo), _float_of(hi)
        la, lb = jnp.log(c_lo), jnp.log(jnp.maximum(c_hi, 0.5))
        frac = jnp.clip((la - log_target) / (la - lb), 0.02, 0.98)
        phase = it % 3
        guess = jnp.where(phase == 0, lo_f + (hi_f - lo_f) * frac, (lo_f + hi_f) * 0.5)
        mid = (lo >> 1) + (hi >> 1) + (lo & hi & 1)
        cand = jnp.clip(jnp.where(phase == 2, mid, _key_of(guess)), lo + 1, hi - 1)
        c = count(lambda k, p0: k >= cand)
        live = act > 0.5
        exact = live & (c == topk)
        up = live & (c > topk)
        dn = live & (c < topk)
        below = jnp.where(exact, cand - 1, below)
        found = jnp.where(exact, 1.0, found)
        lo, c_lo = jnp.where(up, cand, lo), jnp.where(up, c, c_lo)
        hi, c_hi = jnp.where(dn, cand, hi), jnp.where(dn, c, c_hi)
        act = jnp.where(live & jnp.logical_not(exact) & (lo + 1 < hi), 1.0, 0.0)
        return it + 1, jnp.sum(act), lo, hi, c_lo, c_hi, act, found, below

    st = lax.while_loop(lambda st: (st[0] < 100) & (st[1] > 0.5), lambda st: search_round(search_round(st)),
                        (jnp.int32(0), jnp.sum(act0), lo0, hi0, c_lo0, c_hi0, act0,
                         jnp.where(found0, 1.0, 0.0), below0))
    _, _, lo, _, _, c_hi, _, found, below = st

    tied = crowded & (found < 0.5)
    thr = jnp.where(tied, lo, below)
    need = topk - c_hi
    any_tie = jnp.max(jnp.where(tied, 1.0, 0.0)) > 0.5

    def tie_cut():
        def pos_body(i, cut):
            cand = cut + lax.shift_left(jnp.int32(1), (seq.bit_length() - 1) - i)
            cnt = count(lambda k, p0: (k == thr) & (p0 + srow < cand))
            return jnp.where(cnt < need, cand, cut)
        return lax.fori_loop(0, seq.bit_length(), pos_body, jnp.zeros((1, TQ), jnp.int32))

    cut = lax.cond(any_tie, tie_cut, lambda: jnp.full((1, TQ), -1, jnp.int32))
    cut = jnp.where(tied, cut, -1)

    q = jnp.concatenate([aq_ref[0, :, LANES * h:LANES * (h + 1)] for h in range(A_HEADS)], axis=0)

    def att_body(j, carry):
        ks = pl.multiple_of(j * KC, KC)
        kv = akv_ref[0, pl.ds(ks, KC), :]
        k = key_scr[j]
        taken = (k > thr) | ((k == thr) & (ks + krow <= cut))
        bias = jnp.where(taken, 0.0, NEG_BIG).T
        s3 = _dot_nt(q, kv).reshape(A_HEADS, TQ, KC) + bias[None]
        return _softmax_step(carry, s3, _pv_shared(av1_ref[0, pl.ds(ks, KC), :]))

    carry = lax.fori_loop(0, nchunk, att_body, _online_init(A_HEADS, TQ))
    _store_heads(o_ref, _online_finish(carry), A_HEADS)


def _dsa(p, batch, seq):
    topk = min(DSA_TOPK_MAX, seq // 4)
    r3 = lambda a: a.reshape(batch, seq, a.shape[-1])
    qblk = lambda c: pl.BlockSpec((1, DSA_TQ, c), lambda b, i: (b, i, 0))
    full = lambda c: pl.BlockSpec((1, seq, c), lambda b, i: (b, 0, 0))
    return pl.pallas_call(
        functools.partial(_dsa_body, topk=topk, seq=seq),
        grid=(batch, seq // DSA_TQ),
        in_specs=[qblk(4 * 256), full(256), qblk(LANES), qblk(4 * LANES), full(LANES), full(LANES)],
        out_specs=qblk(A_HEADS * HEAD_DIM),
        out_shape=jax.ShapeDtypeStruct((batch, seq, A_HEADS * HEAD_DIM), BF16),
        scratch_shapes=[pltpu.VMEM((seq // KC, KC, DSA_TQ), jnp.int32)],
        compiler_params=_cparams(("parallel", "arbitrary")),
        name="dsa",
    )(r3(p["iq"]), r3(p["ik"]), r3(p["misc"]), r3(p["aq"]), r3(p["akv"]), r3(p["av1"]))


def _pv_per_head(v1s):
    return lambda p: jnp.concatenate(
        [jnp.dot(p[h], v1s[h], preferred_element_type=F32) for h in range(len(v1s))], axis=0)


def _moba_body(bq_ref, bqp_ref, bkv_ref, bv1_ref, km_ref, hot_ref, o_ref, *, nbp):
    TQ = MOBA_TQ
    c = pl.program_id(1)
    q0 = c * TQ
    own = q0 // MOBA_BLOCK
    row = q0 + lax.broadcasted_iota(jnp.int32, (TQ, 1), 0)
    col = _lane_iota((1, MOBA_BLOCK))
    ks_own = pl.multiple_of(own * MOBA_BLOCK, MOBA_BLOCK)
    heads = range(B_HEADS)
    kv_at = lambda ks, h: bkv_ref[0, pl.ds(ks, MOBA_BLOCK), LANES * h:LANES * (h + 1)]
    v1_at = lambda ks, h: bv1_ref[0, pl.ds(ks, MOBA_BLOCK), LANES * h:LANES * (h + 1)]

    blk = lax.broadcasted_iota(jnp.int32, (nbp, 1), 0)
    gate = jnp.concatenate(
        [_dot_nt(km_ref[0, :nbp, HEAD_DIM * h:HEAD_DIM * (h + 1)], bq_ref[0, :, HEAD_DIM * h:HEAD_DIM * (h + 1)],
                 precision=HIGHEST) for h in heads], axis=1)
    picked = (blk < own) & (_topn_mask(jnp.where(blk < own, gate, -jnp.inf), MOBA_TOPK, axis=0) > 0.0)
    bias_t = jnp.where(picked, 0.0, NEG_BIG)
    fill = jnp.zeros((LANES - nbp, TQ), F32)

    q_aug, s_own = [], []
    for h in heads:
        bias = jnp.concatenate([bias_t[:, TQ * h:TQ * (h + 1)], fill], axis=0).T
        q = bqp_ref[0, :, LANES * h:LANES * (h + 1)]
        q_aug.append(jnp.concatenate([q, bias.astype(BF16)], axis=1))
        s_own.append(jnp.where(ks_own + col <= row, _dot_nt(q, kv_at(ks_own, h)), NEG_BIG))

    stack = lambda xs: jnp.concatenate(xs, axis=0).reshape(B_HEADS, TQ, xs[0].shape[-1])
    carry = _softmax_step(_online_init(B_HEADS, TQ), stack(s_own),
                          _pv_per_head([v1_at(ks_own, h) for h in heads]))

    pair = 2 * MOBA_BLOCK
    npair = (own + 1) // 2

    def scores(i):
        ks = pl.multiple_of(i * pair, pair)
        hot = hot_ref[pl.ds(ks, pair), :]
        return stack([_dot_nt(q_aug[h], jnp.concatenate(
            [bkv_ref[0, pl.ds(ks, pair), LANES * h:LANES * (h + 1)], hot], axis=1)) for h in heads])

    def body(i, carry):
        state, s_cur = carry
        s_next = scores(jnp.minimum(i + 1, npair - 1))
        ks = pl.multiple_of(i * pair, pair)
        v1s = [bv1_ref[0, pl.ds(ks, pair), LANES * h:LANES * (h + 1)] for h in heads]
        return _softmax_step(state, s_cur, _pv_per_head(v1s)), s_next

    carry, _ = lax.fori_loop(0, npair, body, (carry, scores(0)))
    _store_heads(o_ref, _online_finish(carry), B_HEADS)


def _moba(p, hot, batch, seq):
    nb = seq // MOBA_BLOCK
    r3 = lambda a: a.reshape(batch, seq, a.shape[-1])
    km = p["km"][:, :PROJ_TM // MOBA_BLOCK].reshape(batch, nb, 256)
    km = jnp.pad(km, ((0, 0), (0, LANES - nb), (0, 0)))
    qblk = lambda c: pl.BlockSpec((1, MOBA_TQ, c), lambda b, i: (b, i, 0))
    return pl.pallas_call(
        functools.partial(_moba_body, nbp=min(LANES, -(-nb // 8) * 8)),
        grid=(batch, seq // MOBA_TQ),
        in_specs=[qblk(256), qblk(4 * LANES),
                  pl.BlockSpec((1, seq, 4 * LANES), lambda b, i: (b, 0, 0)),
                  pl.BlockSpec((1, seq, 4 * LANES), lambda b, i: (b, 0, 0)),
                  pl.BlockSpec((1, LANES, 256), lambda b, i: (b, 0, 0)),
                  pl.BlockSpec((seq, LANES), lambda b, i: (0, 0))],
        out_specs=qblk(B_HEADS * HEAD_DIM),
        out_shape=jax.ShapeDtypeStruct((batch, seq, B_HEADS * HEAD_DIM), BF16),
        compiler_params=_cparams(("parallel", "arbitrary")),
        name="moba",
    )(r3(p["bq"]), r3(p["bqp"]), r3(p["bkv"]), r3(p["bv1"]), km, hot)


def _nsa_body(cqn_ref, cqr_ref, misc_ref, kc_ref, vc_ref, cslc_ref, cwin_ref, cvs1_ref, cvw1_ref, ov_ref, hot_ref,
              o_ref,
              *, ncp, n_sel):
    TQ, KC = NSA_TQ, NSA_KC
    c = pl.program_id(1)
    q0 = c * TQ
    last = (q0 + TQ - 1) // KC
    ks_last = pl.multiple_of(last * KC, KC)
    row = q0 + lax.broadcasted_iota(jnp.int32, (TQ, 1), 0)
    col = _lane_iota((1, KC))
    gates = jax.nn.sigmoid(misc_ref[0])
    cmp_vis = _lane_iota((1, ncp)) * CMP_STRIDE + (CMP_BLOCK - 1) <= row
    wstart = pl.multiple_of(jnp.maximum(q0 - WINDOW, 0), TQ)
    wlen = WINDOW + TQ
    wdiff = row - (wstart + _lane_iota((1, wlen)))
    wmask = (wdiff >= 0) & (wdiff < WINDOW)

    o_cmp, imp_t = [], []
    for g in range(C_GROUPS):
        qn = jnp.concatenate([cqn_ref[0, :, 256 * (C_REP * g + r):256 * (C_REP * g + r + 1)]
                              for r in range(C_REP)], axis=0)
        s3 = _dot_nt(qn, kc_ref[0, g]).reshape(C_REP, TQ, ncp)
        m = jnp.max(jnp.where(cmp_vis[None], s3, -jnp.inf), axis=-1, keepdims=True)
        m = jnp.where(m > -jnp.inf, m, 0.0)
        pc = jnp.where(cmp_vis[None], jnp.exp2(s3 - m), 0.0)
        den = jnp.sum(pc, axis=-1, keepdims=True)
        pc = pc / jnp.where(den > 0, den, 1.0)
        o_cmp.append(jnp.dot(pc.reshape(C_REP * TQ, ncp).astype(BF16), vc_ref[0, g],
                             preferred_element_type=F32))
        psum = pc[0]
        for r in range(1, C_REP):
            psum = psum + pc[r]
        hi = psum.astype(BF16)
        rest = psum - hi.astype(F32)
        mid = rest.astype(BF16)
        low = (rest - mid.astype(F32)).astype(BF16)
        imp = jnp.dot(jnp.concatenate([hi, mid, low], axis=1), ov_ref[...], preferred_element_type=F32)
        imp_t.append(imp.T)

    blk = lax.broadcasted_iota(jnp.int32, (LANES, 1), 0)
    jq = (q0 + _lane_iota((1, C_GROUPS * TQ)) % TQ) // SLC_BLOCK
    adm = blk <= jq
    forced = adm & ((blk == 0) | (blk == jq) | (blk == jq - 1))
    score = jnp.where(forced, FORCE_SCORE, jnp.where(adm, jnp.concatenate(imp_t, axis=1), -jnp.inf))
    picked = adm & (_topn_mask(score, n_sel, axis=0) > 0.0)
    bias_t = jnp.where(picked, 0.0, NEG_BIG)

    for g in range(C_GROUPS):
        o_c = o_cmp[g]
        bias = bias_t[:, TQ * g:TQ * (g + 1)].T.astype(BF16)

        qr = jnp.concatenate([cqr_ref[0, :, LANES * (C_REP * g + r):LANES * (C_REP * g + r + 1)]
                              for r in range(C_REP)], axis=0)
        qa = jnp.concatenate([qr, jnp.concatenate([bias] * C_REP, axis=0)], axis=1)
        kv_at = lambda j: cslc_ref[0, pl.ds(pl.multiple_of(j * KC, KC), KC), LANES * g:LANES * (g + 1)]
        v1_at = lambda j: cvs1_ref[0, pl.ds(pl.multiple_of(j * KC, KC), KC), LANES * g:LANES * (g + 1)]
        slc_scores = lambda j: _dot_nt(qa, jnp.concatenate(
            [kv_at(j), hot_ref[pl.ds(pl.multiple_of(j * KC, KC), KC), :]], axis=1))

        def slc_body(j, carry):
            return _softmax_step(carry, slc_scores(j).reshape(C_REP, TQ, KC), _pv_shared(v1_at(j)))

        carry = lax.fori_loop(0, last, slc_body, _online_init(C_REP, TQ))
        s3 = jnp.where((ks_last + col <= row)[None], slc_scores(last).reshape(C_REP, TQ, KC), NEG_BIG)
        o_s = _online_finish(_softmax_step(carry, s3, _pv_shared(v1_at(last))))

        kvw = cwin_ref[0, pl.ds(wstart, wlen), LANES * g:LANES * (g + 1)]
        s3 = jnp.where(wmask[None], _dot_nt(qr, kvw).reshape(C_REP, TQ, wlen), NEG_BIG)
        v1w = cvw1_ref[0, pl.ds(wstart, wlen), LANES * g:LANES * (g + 1)]
        o_w = _online_finish(_softmax_step(_online_init(C_REP, TQ), s3, _pv_shared(v1w)))

        outs = []
        for r in range(C_REP):
            hh = C_REP * g + r
            gcol = lambda j: gates[:, MISC_CG + 3 * hh + j:MISC_CG + 3 * hh + j + 1]
            rows = slice(TQ * r, TQ * (r + 1))
            outs.append(gcol(0) * o_c[rows] + gcol(1) * o_s[rows] + gcol(2) * o_w[rows])
        _store_heads(o_ref, jnp.concatenate(outs, axis=0), C_REP, col0=C_REP * HEAD_DIM * g)


def _nsa(p, kcmp, vcmp, overlap, hot, batch, seq):
    ncp = seq // CMP_STRIDE
    n_sel = min(SLC_TOPN, seq // SLC_BLOCK)
    r3 = lambda a: a.reshape(batch, seq, a.shape[-1])
    qblk = lambda c: pl.BlockSpec((1, NSA_TQ, c), lambda b, i: (b, i, 0))
    full = lambda c: pl.BlockSpec((1, seq, c), lambda b, i: (b, 0, 0))
    return pl.pallas_call(
        functools.partial(_nsa_body, ncp=ncp, n_sel=n_sel),
        grid=(batch, seq // NSA_TQ),
        in_specs=[qblk(8 * 256), qblk(8 * LANES), qblk(LANES),
                  pl.BlockSpec((1, C_GROUPS, ncp, 256), lambda b, i: (b, 0, 0, 0)),
                  pl.BlockSpec((1, C_GROUPS, ncp, LANES), lambda b, i: (b, 1, 0, 0)),
                  full(2 * LANES), full(2 * LANES), full(2 * LANES), full(2 * LANES),
                  pl.BlockSpec((3 * ncp, LANES), lambda b, i: (0, 0)),
                  pl.BlockSpec((seq, LANES), lambda b, i: (0, 0))],
        out_specs=qblk(C_HEADS * HEAD_DIM),
        out_shape=jax.ShapeDtypeStruct((batch, seq, C_HEADS * HEAD_DIM), BF16),
        compiler_params=_cparams(("parallel", "arbitrary")),
        name="nsa",
    )(r3(p["cqn"]), r3(p["cqr"]), r3(p["misc"]), kcmp, vcmp, r3(p["cslc"]), r3(p["cwin"]),
      r3(p["cvs1"]), r3(p["cvw1"]), overlap, hot)


def _merge_body(x_ref, g_ref, oa_ref, ob_ref, oc_ref, wm_ref, wa_ref, wb_ref, wc_ref, wo_ref, o_ref):
    x = x_ref[...]
    d = x.shape[1]
    h = _rms(x, g_ref[...]).astype(BF16)
    merged = None
    for i, (o_r, w_r) in enumerate(((oa_ref, wa_ref), (ob_ref, wb_ref), (oc_ref, wc_ref))):
        gate = jax.nn.sigmoid(jnp.dot(h, wm_ref[:, d * i:d * (i + 1)], preferred_element_type=F32))
        y = gate * jnp.dot(o_r[...], w_r[...], preferred_element_type=F32)
        merged = y if merged is None else merged + y
    o_ref[...] = x + jnp.dot(merged.astype(BF16), wo_ref[...], preferred_element_type=F32)


def _merge(x2, g, oa, ob, oc, wm, wa, wb, wc, wo, layer):
    m, d = x2.shape
    tm = MERGE_TM
    row = lambda c: pl.BlockSpec((tm, c), lambda i: (i, 0))
    const = lambda a: pl.BlockSpec(a.shape, lambda i: (0, 0))
    lay = lambda a: pl.BlockSpec((None,) + a.shape[1:], lambda i: (layer, 0, 0))
    return pl.pallas_call(
        _merge_body,
        grid=(m // tm,),
        in_specs=[row(d), const(g), row(oa.shape[1]), row(ob.shape[1]), row(oc.shape[1]),
                  lay(wm), lay(wa), lay(wb), lay(wc), lay(wo)],
        out_specs=row(d),
        out_shape=jax.ShapeDtypeStruct((m, d), F32),
        compiler_params=_cparams(("parallel",)),
        name="merge",
    )(x2, g, oa, ob, oc, wm, wa, wb, wc, wo)


def _rope_tables(seq):
    half = HEAD_DIM // 2
    inv = ROPE_THETA ** (-jnp.arange(half, dtype=F32) / half)
    ang = jnp.arange(seq, dtype=F32)[:, None] * inv[None, :]
    lane = np.arange(LANES)
    ang = ang[:, lane % half]
    second = jnp.asarray((lane % HEAD_DIM) >= half)[None, :]
    sin = jnp.sin(ang)
    return jnp.cos(ang), jnp.where(second, sin, 0.0), jnp.where(second, 0.0, -sin)


def _overlap(seq):
    ncp = seq // CMP_STRIDE
    nc = (seq - CMP_BLOCK) // CMP_STRIDE + 1
    ns = seq // SLC_BLOCK
    cs = np.arange(ncp) * CMP_STRIDE
    ss = np.arange(ns) * SLC_BLOCK
    ov = (cs[:, None] < ss[None, :] + SLC_BLOCK) & (ss[None, :] <= cs[:, None] + CMP_BLOCK - 1)
    ov &= (np.arange(ncp) < nc)[:, None]
    out = np.zeros((ncp, LANES), np.float32)
    out[:, :ns] = ov
    return jnp.asarray(np.tile(out, (3, 1)), BF16)


def _block_onehot(seq, block):
    return jnp.asarray(np.arange(seq)[:, None] // block == np.arange(LANES)[None, :], BF16)


def kernel(x, ffn1_norm, ffn1_w_gate, ffn1_w_up, ffn1_w_down, mix_norm, w_in, cmp_pos_k, cmp_w1_k, cmp_w2_k, cmp_pos_v, cmp_w1_v, cmp_w2_v, w_branch_a, w_branch_b, w_branch_c, w_out, ffn2_norm, ffn2_w_gate, ffn2_w_up, ffn2_w_down, final_norm):
    batch, seq, d = x.shape
    depth = w_in.shape[0]
    assert seq % max(KC, NSA_KC, PROJ_TM) == 0 and WINDOW + NSA_TQ <= seq <= SLC_BLOCK * LANES and w_in.shape[2] == _N_IN
    nc = (seq - CMP_BLOCK) // CMP_STRIDE + 1
    ncp = seq // CMP_STRIDE

    bf = lambda a: a.astype(BF16)
    zero_cols = lambda n: jnp.zeros(w_in.shape[:2] + (n,), w_in.dtype)
    w_perm = bf(jnp.concatenate([w_in[:, :, a:a + n] if a >= 0 else zero_cols(n) for a, n in _perm_runs()], axis=2))
    w_mg = bf(w_in[:, :, _O_MG:])
    ffn1 = (bf(ffn1_w_gate), bf(ffn1_w_up), bf(ffn1_w_down))
    ffn2 = (bf(ffn2_w_gate), bf(ffn2_w_up), bf(ffn2_w_down))
    w_a, w_b, w_c, w_o = bf(w_branch_a), bf(w_branch_b), bf(w_branch_c), bf(w_out)
    half_rows = CMP_BLOCK * HEAD_DIM // 2
    w1 = jnp.stack([cmp_w1_k, cmp_w1_v], axis=1)
    w1 = jnp.concatenate([w1[:, :, :half_rows], w1[:, :, half_rows:]], axis=-1)
    w2 = jnp.pad(jnp.stack([cmp_w2_k, cmp_w2_v], axis=1), ((0, 0), (0, 0), (0, 0), (0, LANES - HEAD_DIM)))
    pos = jnp.stack([cmp_pos_k, cmp_pos_v], axis=1).reshape(depth, 2, 2, half_rows)
    pos = jnp.pad(pos, ((0, 0), (0, 0), (0, 6), (0, 0)))
    cos, sa, sb = _rope_tables(seq)
    overlap = _overlap(seq)
    hot_b = _block_onehot(seq, MOBA_BLOCK)
    hot_s = _block_onehot(seq, SLC_BLOCK)
    row = lambda a: a.reshape(1, d)

    x2 = x.reshape(batch * seq, d)
    for l in range(depth):
        x2 = _ffn(x2, row(ffn1_norm[l]), *ffn1, row(final_norm), l, False)
        p = _inproj(x2, row(mix_norm[l]), w_perm, l, cos, sa, sb, seq)
        xc = p["ccmp"].reshape(batch, seq, 4, HEAD_DIM).transpose(0, 2, 1, 3).reshape(batch, 4, ncp, 1024)
        kc, vc = _compress(xc, w1, w2, pos, l, nc)
        o_a = _dsa(p, batch, seq)
        o_b = _moba(p, hot_b, batch, seq)
        o_c = _nsa(p, kc, vc, overlap, hot_s, batch, seq)
        flat = lambda a: a.reshape(batch * seq, a.shape[-1])
        x2 = _merge(x2, row(mix_norm[l]), flat(o_a), flat(o_b), flat(o_c), w_mg, w_a, w_b, w_c, w_o, l)
        x2 = _ffn(x2, row(ffn2_norm[l]), *ffn2, row(final_norm), l, l == depth - 1)
    return x2.reshape(batch, seq, d)
```
